```python
import jax, jax.numpy as jnp
from jax import lax
import numpy as np

D_MODEL = 1024
BATCH = 4
SEQ = 4096
DEPTH = 1

HEAD_DIM = 64
ROPE_THETA = 10000.0
NORM_EPS = 1e-6
NEG = -1e30
NSA_HEADS = 12
NSA_KV_GROUPS = 3
NSA_HPG = NSA_HEADS // NSA_KV_GROUPS
CMP_BLOCK = 32
CMP_STRIDE = 16
CMP_HIDDEN = 256
SLC_BLOCK = 64
SLC_TOPK = 16
WIN_SIZE = 512
SLC_Q_BLOCK = 64
WIN_Q_BLOCK = 128
DIL_PAIRS = ((128, 1), (512, 4), (2048, 16))
DIL_GROUPS = len(DIL_PAIRS)
DIL_HPG = 4
DIL_HEADS = DIL_GROUPS * DIL_HPG
DIL_NKEYS = max(w // d + 1 for w, d in DIL_PAIRS)
DIL_Q_BLOCK = 128
D_FF = 4 * D_MODEL
Q_A_COLS = NSA_HEADS * HEAD_DIM
KV_A_COLS = NSA_KV_GROUPS * HEAD_DIM
GATE_A_COLS = 3 * NSA_HEADS
QKV_B_COLS = DIL_HEADS * HEAD_DIM
IN_SIZES = (Q_A_COLS, KV_A_COLS, KV_A_COLS, KV_A_COLS, KV_A_COLS, KV_A_COLS, KV_A_COLS,
            GATE_A_COLS, QKV_B_COLS, QKV_B_COLS, QKV_B_COLS, D_MODEL, D_MODEL)
IN_COLS = sum(IN_SIZES)
SPLIT_POINTS = tuple(int(c) for c in np.cumsum(IN_SIZES)[:-1])

kernel_name = "hybrid_nsa_dilated_gated_block"


def _rmsnorm(x, g):
    xf = x.astype(jnp.float32)
    y = xf * lax.rsqrt(jnp.mean(xf * xf, axis=-1, keepdims=True) + NORM_EPS)
    return (y * g.astype(jnp.float32)).astype(x.dtype)


def _rope(x, pos):
    half = HEAD_DIM // 2
    inv_freq = jnp.power(ROPE_THETA, -jnp.arange(half, dtype=jnp.float32) / half)
    ang = jnp.asarray(pos, jnp.float32)[:, None] * inv_freq[None, :]
    shape = (1, ang.shape[0]) + (1,) * (x.ndim - 3) + (half,)
    cos = jnp.cos(ang).reshape(shape)
    sin = jnp.sin(ang).reshape(shape)
    xf = x.astype(jnp.float32)
    x1, x2 = xf[..., :half], xf[..., half:]
    return jnp.concatenate([x1 * cos - x2 * sin, x2 * cos + x1 * sin], axis=-1).astype(x.dtype)


def _masked_softmax(s, mask):
    s = jnp.where(mask, s.astype(jnp.float32), NEG)
    return jax.nn.softmax(s, axis=-1) * mask


def _compress(kv, pos_emb, w1, w2):
    B, S, G, dh = kv.shape
    n_cmp = (S - CMP_BLOCK) // CMP_STRIDE + 1
    idx = np.arange(n_cmp)[:, None] * CMP_STRIDE + np.arange(CMP_BLOCK)[None, :]
    blocks = kv[:, idx] + pos_emb[None, None, :, None, :]
    blocks = jnp.transpose(blocks, (0, 1, 3, 2, 4)).reshape(B, n_cmp, G, CMP_BLOCK * dh)
    return jax.nn.gelu(blocks @ w1) @ w2


def _cmp_to_slc_overlap(n_cmp, n_slc):
    cs = np.arange(n_cmp)[:, None] * CMP_STRIDE
    ss = np.arange(n_slc)[None, :] * SLC_BLOCK
    ov = np.minimum(cs + CMP_BLOCK, ss + SLC_BLOCK) - np.maximum(cs, ss)
    return (np.clip(ov, 0, None) / CMP_BLOCK).astype(np.float32)


def _nsa_mixer(q, k_cmp, v_cmp, k_slc, v_slc, k_win, v_win, gate,
               k_norm_cmp, cmp_k_pos, cmp_k_w1, cmp_k_w2, cmp_v_pos, cmp_v_w1, cmp_v_w2):
    B, S = q.shape[0], q.shape[1]
    G, R, dh = NSA_KV_GROUPS, NSA_HPG, HEAD_DIM
    scale = dh ** -0.5
    t = np.arange(S)

    kc = _compress(k_cmp, cmp_k_pos, cmp_k_w1, cmp_k_w2)
    vc = _compress(v_cmp, cmp_v_pos, cmp_v_w1, cmp_v_w2)
    n_cmp = kc.shape[1]
    c_end = np.arange(n_cmp) * CMP_STRIDE + CMP_BLOCK - 1
    kc = _rope(_rmsnorm(kc, k_norm_cmp), c_end)
    s_cmp = jnp.einsum("bsgrd,bcgd->bgrsc", q, kc) * scale
    p_cmp = _masked_softmax(s_cmp, c_end[None, :] <= t[:, None])
    o_cmp = jnp.einsum("bgrsc,bcgd->bsgrd", p_cmp.astype(vc.dtype), vc)

    n_slc = S // SLC_BLOCK
    top_n = min(SLC_TOPK, n_slc)
    overlap = jnp.asarray(_cmp_to_slc_overlap(n_cmp, n_slc))
    imp = jnp.einsum("bgrsc,cn->bgsn", p_cmp, overlap)
    blk = np.arange(n_slc)[None, :]
    cur = (t // SLC_BLOCK)[:, None]
    forced = (blk == 0) | (blk == cur) | (blk == cur - 1)
    score = jnp.where(forced, jnp.inf, jnp.where(blk <= cur, imp, -jnp.inf))
    top_score, top_idx = lax.top_k(score, top_n)
    top_ok = top_score > -jnp.inf

    kb = jnp.transpose(k_slc.reshape(B, n_slc, SLC_BLOCK, G, dh), (0, 3, 1, 2, 4))
    vb = jnp.transpose(v_slc.reshape(B, n_slc, SLC_BLOCK, G, dh), (0, 3, 1, 2, 4))
    gather_blocks = jax.vmap(jax.vmap(lambda kk, ii: kk[ii]))
    nqs = S // SLC_Q_BLOCK

    def slc_block(xs):
        qc, idx, ok, tq = xs
        ks = gather_blocks(kb, idx)
        vs = gather_blocks(vb, idx)
        s = jnp.einsum("bqgrd,bgqnkd->bgrqnk", qc, ks) * scale
        kpos = idx[..., None] * SLC_BLOCK + np.arange(SLC_BLOCK)
        mask = ok[..., None] & (kpos <= tq[:, None, None])
        s = s.reshape(B, G, R, SLC_Q_BLOCK, top_n * SLC_BLOCK)
        mask = mask.reshape(B, G, 1, SLC_Q_BLOCK, top_n * SLC_BLOCK)
        p = _masked_softmax(s, mask)
        vs = vs.reshape(B, G, SLC_Q_BLOCK, top_n * SLC_BLOCK, dh)
        return jnp.einsum("bgrqj,bgqjd->bqgrd", p.astype(vs.dtype), vs)

    qs = jnp.moveaxis(q.reshape(B, nqs, SLC_Q_BLOCK, G, R, dh), 1, 0)
    idx_s = jnp.moveaxis(top_idx.reshape(B, G, nqs, SLC_Q_BLOCK, top_n), 2, 0)
    ok_s = jnp.moveaxis(top_ok.reshape(B, G, nqs, SLC_Q_BLOCK, top_n), 2, 0)
    tq_s = jnp.arange(S, dtype=jnp.int32).reshape(nqs, SLC_Q_BLOCK)
    o_slc = lax.map(slc_block, (qs, idx_s, ok_s, tq_s))
    o_slc = jnp.moveaxis(o_slc, 0, 1).reshape(B, S, G, R, dh)

    span = WIN_SIZE + WIN_Q_BLOCK
    kw_pad = jnp.pad(k_win, ((0, 0), (WIN_SIZE, 0), (0, 0), (0, 0)))
    vw_pad = jnp.pad(v_win, ((0, 0), (WIN_SIZE, 0), (0, 0), (0, 0)))
    nqw = S // WIN_Q_BLOCK

    def win_block(xs):
        qc, start = xs
        ks = lax.dynamic_slice_in_dim(kw_pad, start, span, axis=1)
        vs = lax.dynamic_slice_in_dim(vw_pad, start, span, axis=1)
        s = jnp.einsum("bqgrd,bkgd->bgrqk", qc, ks) * scale
        tq = start + jnp.arange(WIN_Q_BLOCK)
        kp = start - WIN_SIZE + jnp.arange(span)
        diff = tq[:, None] - kp[None, :]
        mask = (diff >= 0) & (diff < WIN_SIZE) & (kp[None, :] >= 0)
        p = _masked_softmax(s, mask)
        return jnp.einsum("bgrqk,bkgd->bqgrd", p.astype(vs.dtype), vs)

    qw = jnp.moveaxis(q.reshape(B, nqw, WIN_Q_BLOCK, G, R, dh), 1, 0)
    starts = jnp.arange(nqw, dtype=jnp.int32) * WIN_Q_BLOCK
    o_win = lax.map(win_block, (qw, starts))
    o_win = jnp.moveaxis(o_win, 0, 1).reshape(B, S, G, R, dh)

    y = gate[..., 0:1] * o_cmp + gate[..., 1:2] * o_slc + gate[..., 2:3] * o_win
    return y.reshape(B, S, G * R * dh)


def _dilated_mixer(q, k, v):
    B, S = q.shape[0], q.shape[1]
    dh = HEAD_DIM
    scale = dh ** -0.5
    dil = np.array([d for _, d in DIL_PAIRS])
    win = np.array([w for w, _ in DIL_PAIRS])
    j = np.arange(DIL_NKEYS)
    in_window = (j[None, :] * dil[:, None]) <= win[:, None]
    kg = jnp.transpose(k, (2, 0, 1, 3, 4))
    vg = jnp.transpose(v, (2, 0, 1, 3, 4))
    gather_stride = jax.vmap(lambda kk, ii: kk[:, ii])
    nq = S // DIL_Q_BLOCK

    def dil_block(xs):
        qc, tq = xs
        kpos = tq[None, :, None] - j[None, None, :] * dil[:, None, None]
        mask = (kpos >= 0) & in_window[:, None, :]
        idx = jnp.maximum(kpos, 0)
        ks = gather_stride(kg, idx)
        vs = gather_stride(vg, idx)
        s = jnp.einsum("gbqhd,gbqjhd->gbhqj", qc, ks).astype(jnp.float32) * scale
        s = jnp.where(mask[:, None, None], s, NEG)
        m = jnp.max(s, axis=-1, keepdims=True)
        e = jnp.exp(s - m)
        den = jnp.sum(e, axis=-1, keepdims=True)
        o = jnp.einsum("gbhqj,gbqjhd->gbqhd", (e / den).astype(vs.dtype), vs)
        lse = (m + jnp.log(den))[..., 0]
        w = jax.nn.softmax(lse, axis=0)
        return jnp.einsum("gbhq,gbqhd->bqhd", w.astype(o.dtype), o)

    qd = jnp.moveaxis(jnp.transpose(q, (2, 0, 1, 3, 4)).reshape(DIL_GROUPS, B, nq, DIL_Q_BLOCK, DIL_HPG, dh), 2, 0)
    tq_d = jnp.arange(S, dtype=jnp.int32).reshape(nq, DIL_Q_BLOCK)
    o = lax.map(dil_block, (qd, tq_d))
    return jnp.moveaxis(o, 0, 1).reshape(B, S, DIL_HPG * dh)


def setup_inputs(seed: int = 0) -> dict:
    key = jax.random.key(seed)
    ks = jax.random.split(key, 21)
    L = DEPTH

    def nrm(k, shape, scale):
        return scale * jax.random.normal(k, shape, jnp.float32)

    def gain(k, n):
        return 1.0 + 0.02 * jax.random.normal(k, (L, n), jnp.float32)

    return {
        "x": jax.random.normal(ks[0], (BATCH, SEQ, D_MODEL), jnp.float32),
        "norm1_g": gain(ks[1], D_MODEL),
        "w_in": nrm(ks[2], (L, D_MODEL, IN_COLS), D_MODEL ** -0.5),
        "q_norm_a": gain(ks[3], HEAD_DIM),
        "k_norm_cmp": gain(ks[4], HEAD_DIM),
        "k_norm_slc": gain(ks[5], HEAD_DIM),
        "k_norm_win": gain(ks[6], HEAD_DIM),
        "cmp_k_pos": nrm(ks[7], (L, CMP_BLOCK, HEAD_DIM), 0.2),
        "cmp_k_w1": nrm(ks[8], (L, CMP_BLOCK * HEAD_DIM, CMP_HIDDEN), (CMP_BLOCK * HEAD_DIM) ** -0.5),
        "cmp_k_w2": nrm(ks[9], (L, CMP_HIDDEN, HEAD_DIM), CMP_HIDDEN ** -0.5),
        "cmp_v_pos": nrm(ks[10], (L, CMP_BLOCK, HEAD_DIM), 0.2),
        "cmp_v_w1": nrm(ks[11], (L, CMP_BLOCK * HEAD_DIM, CMP_HIDDEN), (CMP_BLOCK * HEAD_DIM) ** -0.5),
        "cmp_v_w2": nrm(ks[12], (L, CMP_HIDDEN, HEAD_DIM), CMP_HIDDEN ** -0.5),
        "q_norm_b": gain(ks[13], HEAD_DIM),
        "k_norm_b": gain(ks[14], HEAD_DIM),
        "w_o_a": nrm(ks[15], (L, Q_A_COLS, D_MODEL), Q_A_COLS ** -0.5),
        "w_o_b": nrm(ks[16], (L, DIL_HPG * HEAD_DIM, D_MODEL), (DIL_HPG * HEAD_DIM) ** -0.5),
        "w_out": nrm(ks[17], (L, D_MODEL, D_MODEL), D_MODEL ** -0.5),
        "norm2_g": gain(ks[18], D_MODEL),
        "w_up": nrm(ks[19], (L, D_MODEL, D_FF), D_MODEL ** -0.5),
        "w_down": nrm(ks[20], (L, D_FF, D_MODEL), D_FF ** -0.5),
    }


def reference(x, norm1_g, w_in, q_norm_a, k_norm_cmp, k_norm_slc, k_norm_win,
              cmp_k_pos, cmp_k_w1, cmp_k_w2, cmp_v_pos, cmp_v_w1, cmp_v_w2,
              q_norm_b, k_norm_b, w_o_a, w_o_b, w_out, norm2_g, w_up, w_down):
    B, S = x.shape[0], x.shape[1]
    G, R, dh = NSA_KV_GROUPS, NSA_HPG, HEAD_DIM
    pos = np.arange(S)
    for i in range(DEPTH):
        h = _rmsnorm(x, norm1_g[i])
        proj = h @ w_in[i]
        (q_a, k_c, v_c, k_s, v_s, k_w, v_w, g_nsa,
         q_b, k_b, v_b, g_ma, g_mb) = jnp.split(proj, SPLIT_POINTS, axis=-1)

        q_a = _rope(_rmsnorm(q_a.reshape(B, S, NSA_HEADS, dh), q_norm_a[i]), pos).reshape(B, S, G, R, dh)
        k_c = k_c.reshape(B, S, G, dh)
        v_c = v_c.reshape(B, S, G, dh)
        k_s = _rope(_rmsnorm(k_s.reshape(B, S, G, dh), k_norm_slc[i]), pos)
        v_s = v_s.reshape(B, S, G, dh)
        k_w = _rope(_rmsnorm(k_w.reshape(B, S, G, dh), k_norm_win[i]), pos)
        v_w = v_w.reshape(B, S, G, dh)
        g_nsa = jax.nn.sigmoid(g_nsa).reshape(B, S, G, R, 3)
        y_a = _nsa_mixer(q_a, k_c, v_c, k_s, v_s, k_w, v_w, g_nsa,
                         k_norm_cmp[i], cmp_k_pos[i], cmp_k_w1[i], cmp_k_w2[i],
                         cmp_v_pos[i], cmp_v_w1[i], cmp_v_w2[i])

        q_b = _rope(_rmsnorm(q_b.reshape(B, S, DIL_HEADS, dh), q_norm_b[i]), pos)
        k_b = _rope(_rmsnorm(k_b.reshape(B, S, DIL_HEADS, dh), k_norm_b[i]), pos)
        y_b = _dilated_mixer(q_b.reshape(B, S, DIL_GROUPS, DIL_HPG, dh),
                             k_b.reshape(B, S, DIL_GROUPS, DIL_HPG, dh),
                             v_b.reshape(B, S, DIL_GROUPS, DIL_HPG, dh))

        mixed = jax.nn.sigmoid(g_ma) * (y_a @ w_o_a[i]) + jax.nn.sigmoid(g_mb) * (y_b @ w_o_b[i])
        x = x + mixed @ w_out[i]

        h2 = _rmsnorm(x, norm2_g[i])
        x = x + jnp.square(jax.nn.relu(h2 @ w_up[i])) @ w_down[i]
    return x
```

```python
import functools

import numpy as np
import jax
import jax.numpy as jnp
from jax import lax
from jax.experimental import pallas as pl
from jax.experimental.pallas import tpu as pltpu

HEAD_DIM = 64
HALF = HEAD_DIM // 2
ROPE_THETA = 10000.0
NORM_EPS = 1e-6
NEG = -1e30
NSA_HEADS = 12
NSA_GROUPS = 3
NSA_HPG = NSA_HEADS // NSA_GROUPS
CMP_BLOCK = 32
CMP_STRIDE = 16
SLC_BLOCK = 64
SLC_TOPK = 16
WIN_SIZE = 512
DIL_PAIRS = ((128, 1), (512, 4), (2048, 16))
DIL_GROUPS = len(DIL_PAIRS)
DIL_HPG = 4
DIL_HEADS = DIL_GROUPS * DIL_HPG
DIL_SPAN = 128

LANES = 128
MXU_N = 256
VMEM_LIMIT = 56 * 1024 * 1024

BF16 = jnp.bfloat16
F32 = jnp.float32

ROPE_SEGS = (("qa", 768), ("qb", 768), ("kb", 768), ("ks", 192), ("kw", 192))
PLAIN_SEGS = (("kc", 192), ("vc", 192), ("vs", 192), ("vw", 192), ("vb", 768))
SIG_SEGS = (("gma", 1024), ("gmb", 1024), ("gns", 384))
ROPE_W = sum(w for _, w in ROPE_SEGS)
PLAIN_W = sum(w for _, w in PLAIN_SEGS)
SIG_W = sum(w for _, w in SIG_SEGS)
HEAD_MAJOR = ("qa", "ks", "kw", "kc", "vc", "vs", "vw")


def _dot(a, b):
    return jnp.dot(a, b, preferred_element_type=F32)


def _dot_nt(a, b):
    return lax.dot_general(a, b, (((1,), (1,)), ((), ())), preferred_element_type=F32)


def _chunks(total, width):
    out, off = [], 0
    while off < total:
        w = min(width, total - off)
        out.append((off, w))
        off += w
    return out


def _seg_lookup(segs, col):
    off = 0
    for name, w in segs:
        if col < off + w:
            return name, col - off
        off += w
    raise ValueError(col)


def _inproj_kernel(x_ref, g1_ref, w_ref, gain_ref, bd_ref, cos_ref, sin_ref,
                   qa_ref, ks_ref, kw_ref, qb_ref, kb_ref,
                   kc_ref, vc_ref, vs_ref, vw_ref, vb_ref,
                   gma_ref, gmb_ref, gns_ref):
    outs = dict(qa=qa_ref, ks=ks_ref, kw=kw_ref, qb=qb_ref, kb=kb_ref, kc=kc_ref, vc=vc_ref,
                vs=vs_ref, vw=vw_ref, vb=vb_ref, gma=gma_ref, gmb=gmb_ref, gns=gns_ref)

    def emit(segs, col, val):
        name, rel = _seg_lookup(segs, col)
        ref = outs[name]
        if name in HEAD_MAJOR:
            for p in range(LANES // HEAD_DIM):
                nm, r = _seg_lookup(segs, col + p * HEAD_DIM)
                outs[nm][r // HEAD_DIM] = val[:, p * HEAD_DIM:(p + 1) * HEAD_DIM].astype(outs[nm].dtype)
        else:
            nm2, _ = _seg_lookup(segs, col + HEAD_DIM)
            if nm2 != name:
                raise ValueError("unaligned dense segment")
            ref[:, rel:rel + LANES] = val.astype(ref.dtype)

    x = x_ref[...]
    ms = jnp.mean(x * x, axis=-1, keepdims=True)
    h = (x * lax.rsqrt(ms + NORM_EPS) * g1_ref[...]).astype(BF16)

    lane = lax.broadcasted_iota(jnp.int32, (x.shape[0], LANES), 1)
    first_half = (lane % HEAD_DIM) < HALF
    cos = cos_ref[...]
    sin = sin_ref[...]
    bd = bd_ref[...]

    for off, w in _chunks(ROPE_W, MXU_N):
        y = _dot(h, w_ref[:, off:off + w])
        for u in range(w // LANES):
            c0 = off + u * LANES
            yy = y[:, u * LANES:(u + 1) * LANES]
            msq = _dot((yy * yy).astype(BF16), bd)
            yn = yy * lax.rsqrt(msq + NORM_EPS) * gain_ref[:, c0:c0 + LANES]
            rot = jnp.where(first_half, pltpu.roll(yn, LANES - HALF, 1), pltpu.roll(yn, HALF, 1))
            emit(ROPE_SEGS, c0, yn * cos + rot * sin)

    base = ROPE_W
    for off, w in _chunks(PLAIN_W, MXU_N):
        y = _dot(h, w_ref[:, base + off:base + off + w])
        for u in range(w // LANES):
            emit(PLAIN_SEGS, off + u * LANES, y[:, u * LANES:(u + 1) * LANES])

    base = ROPE_W + PLAIN_W
    for off, w in _chunks(SIG_W, MXU_N):
        y = jax.nn.sigmoid(_dot(h, w_ref[:, base + off:base + off + w]))
        for u in range(w // LANES):
            emit(SIG_SEGS, off + u * LANES, y[:, u * LANES:(u + 1) * LANES])


def _rope_tables(positions, width):
    inv_freq = np.power(ROPE_THETA, -np.arange(HALF, dtype=np.float64) / HALF)
    ang = np.asarray(positions, np.float64)[:, None] * inv_freq[None, :]
    reps = width // HEAD_DIM
    cos = np.tile(np.concatenate([np.cos(ang), np.cos(ang)], axis=1), (1, reps))
    sin = np.tile(np.concatenate([-np.sin(ang), np.sin(ang)], axis=1), (1, reps))
    return jnp.asarray(cos, F32), jnp.asarray(sin, F32)


def _inproj(x, norm1_g, w_in, q_norm_a, k_norm_slc, k_norm_win, q_norm_b, k_norm_b, tm):
    B, S, D = x.shape
    scale = HEAD_DIM ** -0.5
    sp = np.cumsum((0, 768, 192, 192, 192, 192, 192, 192, 36, 768, 768, 768, 1024, 1024))
    names = ("qa", "kc", "vc", "ks", "vs", "kw", "vw", "gns", "qb", "kb", "vb", "gma", "gmb")
    col = {n: w_in[:, int(sp[i]):int(sp[i + 1])] for i, n in enumerate(names)}
    gpg = NSA_HPG * 3
    gns = jnp.concatenate(
        [jnp.pad(col["gns"][:, g * gpg:(g + 1) * gpg], ((0, 0), (0, LANES - gpg))) for g in range(NSA_GROUPS)],
        axis=1)
    col["gns"] = gns
    w_p = jnp.concatenate([col[n] for n, _ in ROPE_SEGS + PLAIN_SEGS + SIG_SEGS], axis=1).astype(BF16)
    ncol = ROPE_W + PLAIN_W + SIG_W
    assert w_p.shape == (D, ncol)

    gain = jnp.concatenate([
        jnp.tile(q_norm_a * scale, NSA_HEADS), jnp.tile(q_norm_b * scale, DIL_HEADS),
        jnp.tile(k_norm_b, DIL_HEADS), jnp.tile(k_norm_slc, NSA_GROUPS), jnp.tile(k_norm_win, NSA_GROUPS)
    ]).reshape(1, ROPE_W).astype(F32)
    bd = jnp.asarray(np.kron(np.eye(LANES // HEAD_DIM), np.full((HEAD_DIM, HEAD_DIM), 1.0 / HEAD_DIM)), BF16)
    cos, sin = _rope_tables(np.arange(S), LANES)

    def hm(nh, dt):
        return (jax.ShapeDtypeStruct((B, nh, S, HEAD_DIM), dt),
                pl.BlockSpec((None, nh, tm, HEAD_DIM), lambda b, m: (b, 0, m, 0)))

    def dense(w, dt):
        return (jax.ShapeDtypeStruct((B, S, w), dt), pl.BlockSpec((None, tm, w), lambda b, m: (b, m, 0)))

    outs = [hm(NSA_HEADS, BF16), hm(NSA_GROUPS, BF16), hm(NSA_GROUPS, BF16),
            dense(768, BF16), dense(768, BF16),
            hm(NSA_GROUPS, F32), hm(NSA_GROUPS, F32),
            hm(NSA_GROUPS, BF16), hm(NSA_GROUPS, BF16), dense(768, BF16),
            dense(1024, BF16), dense(1024, BF16), dense(384, F32)]
    const = lambda b, m: (0, 0)
    return pl.pallas_call(
        _inproj_kernel,
        grid=(B, S // tm),
        in_specs=[pl.BlockSpec((None, tm, D), lambda b, m: (b, m, 0)),
                  pl.BlockSpec((1, D), const),
                  pl.BlockSpec((D, ncol), const),
                  pl.BlockSpec((1, ROPE_W), const),
                  pl.BlockSpec((LANES, LANES), const),
                  pl.BlockSpec((tm, LANES), lambda b, m: (m, 0)),
                  pl.BlockSpec((tm, LANES), lambda b, m: (m, 0))],
        out_specs=[o[1] for o in outs],
        out_shape=[o[0] for o in outs],
        compiler_params=pltpu.CompilerParams(dimension_semantics=("parallel", "arbitrary"),
                                             vmem_limit_bytes=VMEM_LIMIT),
        name="inproj",
    )(x, norm1_g.reshape(1, D), w_p, gain, bd, cos, sin)


def _gelu_tanh(x):
    return 0.5 * x * (1.0 + jnp.tanh(np.sqrt(2.0 / np.pi) * (x + 0.044715 * (x * x * x))))


def _compress_kernel(kr_ref, vr_ref, pk_ref, pv_ref, w1k_ref, w2k_ref, w1v_ref, w2v_ref,
                     gain_ref, cos_ref, sin_ref, kc_ref, vc_ref):
    nc = kr_ref.shape[0]
    half_w = CMP_STRIDE * HEAD_DIM

    def compress(xr, pos_ref, w1_ref, w2_ref):
        xa = (xr + pos_ref[0:1, :]).astype(BF16)
        xb = (xr + pos_ref[1:2, :]).astype(BF16)
        a = _dot(xa, w1_ref[0:half_w, :])
        b = _dot(xb, w1_ref[half_w:2 * half_w, :])
        pre = a + pltpu.roll(b, nc - 1, 0)
        return _dot(_gelu_tanh(pre).astype(BF16), w2_ref[...])

    kc = compress(kr_ref[...], pk_ref, w1k_ref, w2k_ref)
    vc = compress(vr_ref[...], pv_ref, w1v_ref, w2v_ref)
    ms = jnp.mean(kc * kc, axis=-1, keepdims=True)
    kn = kc * lax.rsqrt(ms + NORM_EPS) * gain_ref[...]
    rot = jnp.concatenate([kn[:, HALF:], kn[:, :HALF]], axis=-1)
    kc_ref[...] = (kn * cos_ref[...] + rot * sin_ref[...]).astype(kc_ref.dtype)
    vc_ref[...] = vc.astype(vc_ref.dtype)


def _compress(kc_raw, vc_raw, k_norm_cmp, kpos, kw1, kw2, vpos, vw1, vw2):
    B, G, S, dh = kc_raw.shape
    nc = S // CMP_STRIDE
    half_w = CMP_STRIDE * dh
    kr = kc_raw.reshape(B, G, nc, half_w)
    vr = vc_raw.reshape(B, G, nc, half_w)
    pk = kpos.reshape(2, half_w)
    pv = vpos.reshape(2, half_w)
    cos, sin = _rope_tables(np.arange(nc) * CMP_STRIDE + CMP_BLOCK - 1, dh)
    const = lambda b, g: (0, 0)
    blk = pl.BlockSpec((None, None, nc, half_w), lambda b, g: (b, g, 0, 0))
    oblk = pl.BlockSpec((None, None, nc, dh), lambda b, g: (b, g, 0, 0))
    hid = kw1.shape[1]
    return pl.pallas_call(
        _compress_kernel,
        grid=(B, G),
        in_specs=[blk, blk,
                  pl.BlockSpec((2, half_w), const), pl.BlockSpec((2, half_w), const),
                  pl.BlockSpec((2 * half_w, hid), const), pl.BlockSpec((hid, dh), const),
                  pl.BlockSpec((2 * half_w, hid), const), pl.BlockSpec((hid, dh), const),
                  pl.BlockSpec((1, dh), const), pl.BlockSpec((nc, dh), const), pl.BlockSpec((nc, dh), const)],
        out_specs=[oblk, oblk],
        out_shape=[jax.ShapeDtypeStruct((B, G, nc, dh), BF16)] * 2,
        compiler_params=pltpu.CompilerParams(dimension_semantics=("parallel", "arbitrary"),
                                             vmem_limit_bytes=VMEM_LIMIT),
        name="compress",
    )(kr, vr, pk, pv, kw1.astype(BF16), kw2.astype(BF16), vw1.astype(BF16), vw2.astype(BF16),
      k_norm_cmp.reshape(1, dh).astype(F32), cos, sin)


def _nsa_kernel(q_ref, kc_ref, vc_ref, ks_ref, vs_ref, kw_ref, vw_ref, gate_ref, ovt_ref, exp_ref,
                o_ref, score_ref, *, tq, tk, n_cmp, n_slc, seq):
    R = NSA_HPG
    qt = pl.program_id(2)
    t0 = qt * tq
    q2 = q_ref[...].reshape(R * tq, HEAD_DIM)
    ncp = kc_ref.shape[0]
    t_col = t0 + lax.broadcasted_iota(jnp.int32, (tq, 1), 0)

    s = _dot_nt(q2, kc_ref[...])
    c_idx = lax.broadcasted_iota(jnp.int32, (tq, ncp), 1)
    cmask = ((c_idx * CMP_STRIDE + (CMP_BLOCK - 1)) <= t_col) & (c_idx < n_cmp)
    ps = []
    for r in range(R):
        sr = jnp.where(cmask, s[r * tq:(r + 1) * tq], NEG)
        m = jnp.max(sr, axis=-1, keepdims=True)
        e = jnp.where(cmask, jnp.exp(sr - m), 0.0)
        den = jnp.sum(e, axis=-1, keepdims=True)
        ps.append(e / jnp.where(den > 0.0, den, 1.0))
    o_cmp = _dot(jnp.concatenate(ps, axis=0).astype(BF16), vc_ref[...])

    psum = ps[0]
    for r in range(1, R):
        psum = psum + ps[r]
    p_hi = psum.astype(BF16)
    p_lo = (psum - p_hi.astype(F32)).astype(BF16)
    imp_t = _dot_nt(ovt_ref[...], p_hi) + _dot_nt(ovt_ref[...], p_lo)
    n_idx = lax.broadcasted_iota(jnp.int32, (LANES, tq), 0)
    cur = (t0 + lax.broadcasted_iota(jnp.int32, (LANES, tq), 1)) // SLC_BLOCK
    forced = (n_idx == 0) | (n_idx == cur) | (n_idx == cur - 1)
    score = jnp.where(forced, jnp.inf, jnp.where(n_idx <= cur, imp_t, -jnp.inf))
    score_ref[...] = score

    def rank_body(mi, cnt):
        row = score_ref[pl.ds(mi, 1), :]
        beats = (row > score) | ((row == score) & (mi < n_idx))
        return cnt + jnp.where(beats, 1.0, 0.0)

    n_live = jnp.minimum((t0 + tq - 1) // SLC_BLOCK + 1, n_slc)
    cnt = lax.fori_loop(0, n_live, rank_body, jnp.zeros((LANES, tq), F32))
    sel_t = jnp.where((cnt < float(min(SLC_TOPK, n_slc))) & (score > -jnp.inf), 1.0, 0.0)
    sel = sel_t.T.astype(BF16)

    kpos0 = lax.broadcasted_iota(jnp.int32, (tq, tk), 1)

    def slc_body(kt, carry):
        ms_, ls_, acc = carry
        k0 = pl.multiple_of(kt * tk, tk)
        sexp = _dot(sel, exp_ref[:, pl.ds(k0, tk)])
        ok = (sexp > 0.5) & ((kpos0 + k0) <= t_col)
        bias = jnp.where(ok, 0.0, NEG)
        sk = _dot_nt(q2, ks_ref[pl.ds(k0, tk), :])
        new_m, new_l, p_all, alphas = [], [], [], []
        for r in range(R):
            sr = sk[r * tq:(r + 1) * tq] + bias
            m_new = jnp.maximum(ms_[r], jnp.max(sr, axis=-1, keepdims=True))
            alpha = jnp.exp(ms_[r] - m_new)
            p = jnp.exp(sr - m_new)
            new_l.append(alpha * ls_[r] + jnp.sum(p, axis=-1, keepdims=True))
            new_m.append(m_new)
            alphas.append(alpha)
            p_all.append(p)
        pv = _dot(jnp.concatenate(p_all, axis=0).astype(BF16), vs_ref[pl.ds(k0, tk), :])
        acc = jnp.concatenate(alphas, axis=0) * acc + pv
        return tuple(new_m), tuple(new_l), acc

    init = (tuple(jnp.full((tq, 1), NEG, F32) for _ in range(R)),
            tuple(jnp.zeros((tq, 1), F32) for _ in range(R)),
            jnp.zeros((R * tq, HEAD_DIM), F32))
    n_kt = t0 // tk + 1
    _, l_fin, acc = lax.fori_loop(0, n_kt, slc_body, init)
    o_slc = acc / jnp.concatenate(l_fin, axis=0)

    span = min(WIN_SIZE + tq, seq)
    start = pl.multiple_of(jnp.maximum(t0 - WIN_SIZE, 0), tq)
    sw = _dot_nt(q2, kw_ref[pl.ds(start, span), :])
    diff = t_col - (start + lax.broadcasted_iota(jnp.int32, (tq, span), 1))
    wmask = (diff >= 0) & (diff < WIN_SIZE)
    pw = []
    for r in range(R):
        sr = jnp.where(wmask, sw[r * tq:(r + 1) * tq], NEG)
        m = jnp.max(sr, axis=-1, keepdims=True)
        e = jnp.exp(sr - m)
        pw.append(e / jnp.sum(e, axis=-1, keepdims=True))
    o_win = _dot(jnp.concatenate(pw, axis=0).astype(BF16), vw_ref[pl.ds(start, span), :])

    gate = gate_ref[...]
    for r in range(R):
        rows = slice(r * tq, (r + 1) * tq)
        y = (gate[:, 3 * r:3 * r + 1] * o_cmp[rows] + gate[:, 3 * r + 1:3 * r + 2] * o_slc[rows]
             + gate[:, 3 * r + 2:3 * r + 3] * o_win[rows])
        o_ref[:, r * HEAD_DIM:(r + 1) * HEAD_DIM] = y.astype(o_ref.dtype)


def _overlap_t(ncp, n_cmp, n_slc):
    cs = np.arange(ncp)[None, :] * CMP_STRIDE
    ss = np.arange(LANES)[:, None] * SLC_BLOCK
    ov = np.clip(np.minimum(cs + CMP_BLOCK, ss + SLC_BLOCK) - np.maximum(cs, ss), 0, None) / CMP_BLOCK
    ov = ov * (np.arange(ncp)[None, :] < n_cmp) * (np.arange(LANES)[:, None] < n_slc)
    return jnp.asarray(ov, BF16)


def _nsa(qa, kc, vc, ks, vs, kw, vw, gns, tq, tk):
    B, H, S, dh = qa.shape
    G = NSA_GROUPS
    ncp = kc.shape[2]
    n_cmp = (S - CMP_BLOCK) // CMP_STRIDE + 1
    n_slc = S // SLC_BLOCK
    assert n_slc <= LANES and S % tk == 0 and tk % tq == 0
    ovt = _overlap_t(ncp, n_cmp, n_slc)
    expand = jnp.asarray(np.arange(LANES)[:, None] == (np.arange(S)[None, :] // SLC_BLOCK), BF16)
    kv_c = pl.BlockSpec((None, None, ncp, dh), lambda b, g, t: (b, g, 0, 0))
    kv_s = pl.BlockSpec((None, None, S, dh), lambda b, g, t: (b, g, 0, 0))
    const = lambda b, g, t: (0, 0)
    kern = functools.partial(_nsa_kernel, tq=tq, tk=tk, n_cmp=n_cmp, n_slc=n_slc, seq=S)
    return pl.pallas_call(
        kern,
        grid=(B, G, S // tq),
        in_specs=[pl.BlockSpec((None, NSA_HPG, tq, dh), lambda b, g, t: (b, g, t, 0)),
                  kv_c, kv_c, kv_s, kv_s, kv_s, kv_s,
                  pl.BlockSpec((None, tq, LANES), lambda b, g, t: (b, t, g)),
                  pl.BlockSpec((LANES, ncp), const),
                  pl.BlockSpec((LANES, S), const)],
        out_specs=pl.BlockSpec((None, tq, NSA_HPG * dh), lambda b, g, t: (b, t, g)),
        out_shape=jax.ShapeDtypeStruct((B, S, H * dh), BF16),
        scratch_shapes=[pltpu.VMEM((LANES, tq), F32)],
        compiler_params=pltpu.CompilerParams(dimension_semantics=("parallel", "parallel", "arbitrary"),
                                             vmem_limit_bytes=VMEM_LIMIT),
        name="nsa",
    )(qa, kc, vc, ks, vs, kw, vw, gns, ovt, expand)


def _dilated_kernel(q_ref, k_ref, v_ref, o_ref, lse_ref, *, tq, span):
    sd = q_ref.shape[0]

    def tile(i, _):
        r0 = pl.multiple_of(i * tq, tq)
        start = pl.multiple_of(jnp.maximum(r0 - (span - tq), 0), tq)
        qb = q_ref[pl.ds(r0, tq), :]
        kb = k_ref[pl.ds(start, span), :]
        vb = v_ref[pl.ds(start, span), :]
        qi = r0 + lax.broadcasted_iota(jnp.int32, (tq, 1), 0)
        diff = qi - (start + lax.broadcasted_iota(jnp.int32, (tq, span), 1))
        mask = (diff >= 0) & (diff <= DIL_SPAN)
        for h in range(DIL_HPG):
            cols = slice(h * HEAD_DIM, (h + 1) * HEAD_DIM)
            s = jnp.where(mask, _dot_nt(qb[:, cols], kb[:, cols]), NEG)
            m = jnp.max(s, axis=-1, keepdims=True)
            e = jnp.exp(s - m)
            den = jnp.sum(e, axis=-1, keepdims=True)
            o = _dot((e / den).astype(BF16), vb[:, cols])
            o_ref[pl.ds(r0, tq), cols] = o
            lse_ref[pl.ds(r0, tq), cols] = jnp.broadcast_to(m + jnp.log(den), (tq, HEAD_DIM))
        return 0

    lax.fori_loop(0, sd // tq, tile, 0)


def _dilated(qb, kb, vb, g):
    B, S, _ = qb.shape
    d = DIL_PAIRS[g][1]
    assert DIL_PAIRS[g][0] // d == DIL_SPAN and S % d == 0
    sd = S // d
    tq = min(LANES, sd)
    span = min(tq + DIL_SPAN, sd)
    gw = DIL_HPG * HEAD_DIM
    ncols = DIL_HEADS * HEAD_DIM
    ins = [a.reshape(B, sd, d * ncols) for a in (qb, kb, vb)]
    in_blk = pl.BlockSpec((None, sd, gw), lambda b, r: (b, 0, r * DIL_GROUPS + g))
    out_blk = pl.BlockSpec((None, sd, gw), lambda b, r: (b, 0, r))
    o, lse = pl.pallas_call(
        functools.partial(_dilated_kernel, tq=tq, span=span),
        grid=(B, d),
        in_specs=[in_blk, in_blk, in_blk],
        out_specs=[out_blk, out_blk],
        out_shape=[jax.ShapeDtypeStruct((B, sd, d * gw), F32)] * 2,
        compiler_params=pltpu.CompilerParams(dimension_semantics=("parallel", "parallel"),
                                             vmem_limit_bytes=VMEM_LIMIT),
        name=f"dilated{g}",
    )(*ins)
    return o.reshape(B, S, gw), lse.reshape(B, S, gw)


def _mixout_kernel(x_ref, ya_ref, o0_ref, o1_ref, o2_ref, l0_ref, l1_ref, l2_ref, gma_ref, gmb_ref,
                   woa_ref, wob_ref, wout_ref, out_ref):
    l0, l1, l2 = l0_ref[...], l1_ref[...], l2_ref[...]
    mx = jnp.maximum(jnp.maximum(l0, l1), l2)
    e0, e1, e2 = jnp.exp(l0 - mx), jnp.exp(l1 - mx), jnp.exp(l2 - mx)
    yb = (e0 * o0_ref[...] + e1 * o1_ref[...] + e2 * o2_ref[...]) / (e0 + e1 + e2)
    ta = _dot(ya_ref[...], woa_ref[...])
    tb = _dot(yb.astype(BF16), wob_ref[...])
    mixed = gma_ref[...].astype(F32) * ta + gmb_ref[...].astype(F32) * tb
    out_ref[...] = x_ref[...] + _dot(mixed.astype(BF16), wout_ref[...])


def _mixout(x, ya, dil, gma, gmb, w_o_a, w_o_b, w_out, tm):
    B, S, D = x.shape
    row = lambda w: pl.BlockSpec((None, tm, w), lambda b, m: (b, m, 0))
    full = lambda a: pl.BlockSpec(a.shape, lambda b, m: (0, 0))
    gw = DIL_HPG * HEAD_DIM
    wa, wb, wo = w_o_a.astype(BF16), w_o_b.astype(BF16), w_out.astype(BF16)
    return pl.pallas_call(
        _mixout_kernel,
        grid=(B, S // tm),
        in_specs=[row(D), row(ya.shape[-1])] + [row(gw)] * 6 + [row(D), row(D), full(wa), full(wb), full(wo)],
        out_specs=row(D),
        out_shape=jax.ShapeDtypeStruct((B, S, D), F32),
        compiler_params=pltpu.CompilerParams(dimension_semantics=("parallel", "parallel"),
                                             vmem_limit_bytes=VMEM_LIMIT),
        name="mixout",
    )(x, ya, dil[0][0], dil[1][0], dil[2][0], dil[0][1], dil[1][1], dil[2][1], gma, gmb, wa, wb, wo)


def _mlp_kernel(x_ref, g_ref, wup_ref, wdn_ref, out_ref, *, fc):
    x = x_ref[...]
    ms = jnp.mean(x * x, axis=-1, keepdims=True)
    h = (x * lax.rsqrt(ms + NORM_EPS) * g_ref[...]).astype(BF16)
    acc = x
    for off, w in _chunks(wup_ref.shape[1], fc):
        u = jnp.maximum(_dot(h, wup_ref[:, off:off + w]), 0.0)
        acc = acc + _dot((u * u).astype(BF16), wdn_ref[off:off + w, :])
    out_ref[...] = acc


def _mlp(x, norm2_g, w_up, w_down, tm, fc):
    B, S, D = x.shape
    F = w_up.shape[1]
    row = pl.BlockSpec((None, tm, D), lambda b, m: (b, m, 0))
    const = lambda b, m: (0, 0)
    return pl.pallas_call(
        functools.partial(_mlp_kernel, fc=fc),
        grid=(B, S // tm),
        in_specs=[row, pl.BlockSpec((1, D), const), pl.BlockSpec((D, F), const), pl.BlockSpec((F, D), const)],
        out_specs=row,
        out_shape=jax.ShapeDtypeStruct((B, S, D), F32),
        compiler_params=pltpu.CompilerParams(dimension_semantics=("parallel", "parallel"),
                                             vmem_limit_bytes=VMEM_LIMIT),
        name="mlp",
    )(x, norm2_g.reshape(1, D), w_up.astype(BF16), w_down.astype(BF16))


def kernel(x, norm1_g, w_in, q_norm_a, k_norm_cmp, k_norm_slc, k_norm_win, cmp_k_pos, cmp_k_w1, cmp_k_w2,
           cmp_v_pos, cmp_v_w1, cmp_v_w2, q_norm_b, k_norm_b, w_o_a, w_o_b, w_out, norm2_g, w_up, w_down):
    depth = w_in.shape[0]
    for i in range(depth):
        (qa, ks, kw, qb, kb, kc_raw, vc_raw, vs, vw, vb, gma, gmb, gns) = _inproj(
            x, norm1_g[i], w_in[i], q_norm_a[i], k_norm_slc[i], k_norm_win[i], q_norm_b[i], k_norm_b[i], tm=256)
        kc, vc = _compress(kc_raw, vc_raw, k_norm_cmp[i], cmp_k_pos[i], cmp_k_w1[i], cmp_k_w2[i],
                           cmp_v_pos[i], cmp_v_w1[i], cmp_v_w2[i])
        ya = _nsa(qa, kc, vc, ks, vs, kw, vw, gns, tq=128, tk=512)
        dil = [_dilated(qb, kb, vb, g) for g in range(DIL_GROUPS)]
        x = _mixout(x, ya, dil, gma, gmb, w_o_a[i], w_o_b[i], w_out[i], tm=256)
        x = _mlp(x, norm2_g[i], w_up[i], w_down[i], tm=512, fc=1024)
    return x
```

```python
import functools

import numpy as np
import jax
import jax.numpy as jnp
from jax import lax
from jax.experimental import pallas as pl
from jax.experimental.pallas import tpu as pltpu

HEAD_DIM = 64
HALF = HEAD_DIM // 2
ROPE_THETA = 10000.0
NORM_EPS = 1e-6
NEG = -1e30
NSA_HEADS = 12
NSA_GROUPS = 3
NSA_HPG = NSA_HEADS // NSA_GROUPS
CMP_BLOCK = 32
CMP_STRIDE = 16
SLC_BLOCK = 64
SLC_TOPK = 16
WIN_SIZE = 512
DIL_PAIRS = ((128, 1), (512, 4), (2048, 16))
DIL_GROUPS = len(DIL_PAIRS)
DIL_HPG = 4
DIL_HEADS = DIL_GROUPS * DIL_HPG
DIL_SPAN = 128
LOG2E = float(np.log2(np.e))
ONES_ROWS = 16

LANES = 128
MXU_N = 256
VMEM_LIMIT = 56 * 1024 * 1024

BF16 = jnp.bfloat16
F32 = jnp.float32

ROPE_SEGS = (("qa", 768), ("qb", 768), ("kb", 768), ("ks", 192), ("kw", 192))
PLAIN_SEGS = (("kc", 192), ("vc", 192), ("vs", 192), ("vw", 192), ("vb", 768))
SIG_SEGS = (("gma", 1024), ("gmb", 1024), ("gns", 384))
ROPE_W = sum(w for _, w in ROPE_SEGS)
PLAIN_W = sum(w for _, w in PLAIN_SEGS)
SIG_W = sum(w for _, w in SIG_SEGS)
HEAD_MAJOR = ("qa", "ks", "kw", "kc", "vc", "vs", "vw")


def _dot(a, b):
    return jnp.dot(a, b, preferred_element_type=F32)


def _dot_nt(a, b):
    return lax.dot_general(a, b, (((1,), (1,)), ((), ())), preferred_element_type=F32)


def _chunks(total, width):
    out, off = [], 0
    while off < total:
        w = min(width, total - off)
        out.append((off, w))
        off += w
    return out


def _seg_lookup(segs, col):
    off = 0
    for name, w in segs:
        if col < off + w:
            return name, col - off
        off += w
    raise ValueError(col)


def _inproj_kernel(x_ref, g1_ref, w_ref, gain_ref, bd_ref, cos_ref, sin_ref,
                   qa_ref, ks_ref, kw_ref, qb_ref, kb_ref,
                   kc_ref, vc_ref, vs_ref, vw_ref, vb_ref,
                   gma_ref, gmb_ref, gns_ref):
    outs = dict(qa=qa_ref, ks=ks_ref, kw=kw_ref, qb=qb_ref, kb=kb_ref, kc=kc_ref, vc=vc_ref,
                vs=vs_ref, vw=vw_ref, vb=vb_ref, gma=gma_ref, gmb=gmb_ref, gns=gns_ref)

    def emit(segs, col, val):
        name, rel = _seg_lookup(segs, col)
        ref = outs[name]
        if name in HEAD_MAJOR:
            for p in range(LANES // HEAD_DIM):
                nm, r = _seg_lookup(segs, col + p * HEAD_DIM)
                outs[nm][r // HEAD_DIM] = val[:, p * HEAD_DIM:(p + 1) * HEAD_DIM].astype(outs[nm].dtype)
        else:
            nm2, _ = _seg_lookup(segs, col + HEAD_DIM)
            if nm2 != name:
                raise ValueError("unaligned dense segment")
            ref[:, rel:rel + LANES] = val.astype(ref.dtype)

    x = x_ref[...]
    ms = jnp.mean(x * x, axis=-1, keepdims=True)
    h = (x * lax.rsqrt(ms + NORM_EPS) * g1_ref[...]).astype(BF16)

    lane = lax.broadcasted_iota(jnp.int32, (x.shape[0], LANES), 1)
    first_half = (lane % HEAD_DIM) < HALF
    cos = cos_ref[...]
    sin = sin_ref[...]
    bd = bd_ref[...]

    for off, w in _chunks(ROPE_W, MXU_N):
        y = _dot(h, w_ref[:, off:off + w])
        for u in range(w // LANES):
            c0 = off + u * LANES
            yy = y[:, u * LANES:(u + 1) * LANES]
            msq = _dot((yy * yy).astype(BF16), bd)
            yn = yy * lax.rsqrt(msq + NORM_EPS) * gain_ref[:, c0:c0 + LANES]
            rot = jnp.where(first_half, pltpu.roll(yn, LANES - HALF, 1), pltpu.roll(yn, HALF, 1))
            emit(ROPE_SEGS, c0, yn * cos + rot * sin)

    base = ROPE_W
    for off, w in _chunks(PLAIN_W, MXU_N):
        y = _dot(h, w_ref[:, base + off:base + off + w])
        for u in range(w // LANES):
            emit(PLAIN_SEGS, off + u * LANES, y[:, u * LANES:(u + 1) * LANES])

    base = ROPE_W + PLAIN_W
    for off, w in _chunks(SIG_W, MXU_N):
        y = jax.nn.sigmoid(_dot(h, w_ref[:, base + off:base + off + w]))
        for u in range(w // LANES):
            emit(SIG_SEGS, off + u * LANES, y[:, u * LANES:(u + 1) * LANES])


def _rope_tables(positions, width):
    inv_freq = np.power(ROPE_THETA, -np.arange(HALF, dtype=np.float64) / HALF)
    ang = np.asarray(positions, np.float64)[:, None] * inv_freq[None, :]
    reps = width // HEAD_DIM
    cos = np.tile(np.concatenate([np.cos(ang), np.cos(ang)], axis=1), (1, reps))
    sin = np.tile(np.concatenate([-np.sin(ang), np.sin(ang)], axis=1), (1, reps))
    return jnp.asarray(cos, F32), jnp.asarray(sin, F32)


def _inproj(x, norm1_g, w_in, q_norm_a, k_norm_slc, k_norm_win, q_norm_b, k_norm_b, tm):
    B, S, D = x.shape
    scale = HEAD_DIM ** -0.5
    sp = np.cumsum((0, 768, 192, 192, 192, 192, 192, 192, 36, 768, 768, 768, 1024, 1024))
    names = ("qa", "kc", "vc", "ks", "vs", "kw", "vw", "gns", "qb", "kb", "vb", "gma", "gmb")
    col = {n: w_in[:, int(sp[i]):int(sp[i + 1])] for i, n in enumerate(names)}
    gpg = NSA_HPG * 3
    gns = jnp.concatenate(
        [jnp.pad(col["gns"][:, g * gpg:(g + 1) * gpg], ((0, 0), (0, LANES - gpg))) for g in range(NSA_GROUPS)],
        axis=1)
    col["gns"] = gns
    w_p = jnp.concatenate([col[n] for n, _ in ROPE_SEGS + PLAIN_SEGS + SIG_SEGS], axis=1).astype(BF16)
    ncol = ROPE_W + PLAIN_W + SIG_W
    assert w_p.shape == (D, ncol)

    gain = jnp.concatenate([
        jnp.tile(q_norm_a * (scale * LOG2E), NSA_HEADS), jnp.tile(q_norm_b * scale, DIL_HEADS),
        jnp.tile(k_norm_b, DIL_HEADS), jnp.tile(k_norm_slc, NSA_GROUPS), jnp.tile(k_norm_win, NSA_GROUPS)
    ]).reshape(1, ROPE_W).astype(F32)
    bd = jnp.asarray(np.kron(np.eye(LANES // HEAD_DIM), np.full((HEAD_DIM, HEAD_DIM), 1.0 / HEAD_DIM)), BF16)
    cos, sin = _rope_tables(np.arange(S), LANES)

    def hm(nh, dt):
        return (jax.ShapeDtypeStruct((B, nh, S, HEAD_DIM), dt),
                pl.BlockSpec((None, nh, tm, HEAD_DIM), lambda b, m: (b, 0, m, 0)))

    def dense(w, dt):
        return (jax.ShapeDtypeStruct((B, S, w), dt), pl.BlockSpec((None, tm, w), lambda b, m: (b, m, 0)))

    outs = [hm(NSA_HEADS, BF16), hm(NSA_GROUPS, BF16), hm(NSA_GROUPS, BF16),
            dense(768, BF16), dense(768, BF16),
            hm(NSA_GROUPS, F32), hm(NSA_GROUPS, F32),
            hm(NSA_GROUPS, BF16), hm(NSA_GROUPS, BF16), dense(768, BF16),
            dense(1024, BF16), dense(1024, BF16), dense(384, F32)]
    const = lambda b, m: (0, 0)
    return pl.pallas_call(
        _inproj_kernel,
        grid=(B, S // tm),
        in_specs=[pl.BlockSpec((None, tm, D), lambda b, m: (b, m, 0)),
                  pl.BlockSpec((1, D), const),
                  pl.BlockSpec((D, ncol), const),
                  pl.BlockSpec((1, ROPE_W), const),
                  pl.BlockSpec((LANES, LANES), const),
                  pl.BlockSpec((tm, LANES), lambda b, m: (m, 0)),
                  pl.BlockSpec((tm, LANES), lambda b, m: (m, 0))],
        out_specs=[o[1] for o in outs],
        out_shape=[o[0] for o in outs],
        compiler_params=pltpu.CompilerParams(dimension_semantics=("parallel", "arbitrary"),
                                             vmem_limit_bytes=VMEM_LIMIT),
        name="inproj",
    )(x, norm1_g.reshape(1, D), w_p, gain, bd, cos, sin)


def _gelu_tanh(x):
    return 0.5 * x * (1.0 + jnp.tanh(np.sqrt(2.0 / np.pi) * (x + 0.044715 * (x * x * x))))


def _compress_kernel(kr_ref, vr_ref, pk_ref, pv_ref, w1k_ref, w2k_ref, w1v_ref, w2v_ref,
                     gain_ref, cos_ref, sin_ref, kc_ref, vc_ref):
    nc = kr_ref.shape[0]
    half_w = CMP_STRIDE * HEAD_DIM

    def compress(xr, pos_ref, w1_ref, w2_ref):
        xa = (xr + pos_ref[0:1, :]).astype(BF16)
        xb = (xr + pos_ref[1:2, :]).astype(BF16)
        a = _dot(xa, w1_ref[0:half_w, :])
        b = _dot(xb, w1_ref[half_w:2 * half_w, :])
        pre = a + pltpu.roll(b, nc - 1, 0)
        return _dot(_gelu_tanh(pre).astype(BF16), w2_ref[...])

    kc = compress(kr_ref[...], pk_ref, w1k_ref, w2k_ref)
    vc = compress(vr_ref[...], pv_ref, w1v_ref, w2v_ref)
    ms = jnp.mean(kc * kc, axis=-1, keepdims=True)
    kn = kc * lax.rsqrt(ms + NORM_EPS) * gain_ref[...]
    rot = jnp.concatenate([kn[:, HALF:], kn[:, :HALF]], axis=-1)
    kc_ref[...] = (kn * cos_ref[...] + rot * sin_ref[...]).astype(kc_ref.dtype)
    vc_ref[...] = vc.astype(vc_ref.dtype)


def _compress(kc_raw, vc_raw, k_norm_cmp, kpos, kw1, kw2, vpos, vw1, vw2):
    B, G, S, dh = kc_raw.shape
    nc = S // CMP_STRIDE
    half_w = CMP_STRIDE * dh
    kr = kc_raw.reshape(B, G, nc, half_w)
    vr = vc_raw.reshape(B, G, nc, half_w)
    pk = kpos.reshape(2, half_w)
    pv = vpos.reshape(2, half_w)
    cos, sin = _rope_tables(np.arange(nc) * CMP_STRIDE + CMP_BLOCK - 1, dh)
    const = lambda b, g: (0, 0)
    blk = pl.BlockSpec((None, None, nc, half_w), lambda b, g: (b, g, 0, 0))
    oblk = pl.BlockSpec((None, None, nc, dh), lambda b, g: (b, g, 0, 0))
    hid = kw1.shape[1]
    return pl.pallas_call(
        _compress_kernel,
        grid=(B, G),
        in_specs=[blk, blk,
                  pl.BlockSpec((2, half_w), const), pl.BlockSpec((2, half_w), const),
                  pl.BlockSpec((2 * half_w, hid), const), pl.BlockSpec((hid, dh), const),
                  pl.BlockSpec((2 * half_w, hid), const), pl.BlockSpec((hid, dh), const),
                  pl.BlockSpec((1, dh), const), pl.BlockSpec((nc, dh), const), pl.BlockSpec((nc, dh), const)],
        out_specs=[oblk, oblk],
        out_shape=[jax.ShapeDtypeStruct((B, G, nc, dh), BF16)] * 2,
        compiler_params=pltpu.CompilerParams(dimension_semantics=("parallel", "arbitrary"),
                                             vmem_limit_bytes=VMEM_LIMIT),
        name="compress",
    )(kr, vr, pk, pv, kw1.astype(BF16), kw2.astype(BF16), vw1.astype(BF16), vw2.astype(BF16),
      k_norm_cmp.reshape(1, dh).astype(F32), cos, sin)


def _nsa_kernel(q_ref, kc_ref, vct_ref, ks_ref, vst_ref, kw_ref, vwt_ref, gate_ref, ovt_ref,
                o_ref, score_ref, bias_ref, s_ref, *, tq, tk, n_cmp, n_slc, seq):
    R = NSA_HPG
    qt = pl.program_id(2)
    t0 = qt * tq
    q2 = q_ref[...].reshape(R * tq, HEAD_DIM)
    ncp = kc_ref.shape[0]
    t_lane = t0 + lax.broadcasted_iota(jnp.int32, (1, tq), 1)

    def heads(a):
        return [a[:, r * tq:(r + 1) * tq] for r in range(R)]

    s_c = _dot_nt(kc_ref[...], q2)
    c_idx = lax.broadcasted_iota(jnp.int32, (ncp, tq), 0)
    cmask = ((c_idx * CMP_STRIDE + (CMP_BLOCK - 1)) <= t_lane) & (c_idx < n_cmp)
    ps = []
    for sr in heads(s_c):
        sr = jnp.where(cmask, sr, NEG)
        m = jnp.max(sr, axis=0, keepdims=True)
        e = jnp.where(cmask, jnp.exp2(sr - m), 0.0)
        den = jnp.sum(e, axis=0, keepdims=True)
        ps.append(e / jnp.where(den > 0.0, den, 1.0))
    o_cmp = _dot(vct_ref[...], jnp.concatenate(ps, axis=1).astype(BF16))

    psum = ps[0]
    for r in range(1, R):
        psum = psum + ps[r]
    p_hi = psum.astype(BF16)
    p_lo = (psum - p_hi.astype(F32)).astype(BF16)
    imp = _dot(ovt_ref[...], p_hi) + _dot(ovt_ref[...], p_lo)
    n_idx = lax.broadcasted_iota(jnp.int32, (LANES, tq), 0)
    cur = (t0 + lax.broadcasted_iota(jnp.int32, (LANES, tq), 1)) // SLC_BLOCK
    forced = (n_idx == 0) | (n_idx == cur) | (n_idx == cur - 1)
    score = jnp.where(forced, jnp.inf, jnp.where(n_idx <= cur, imp, -jnp.inf))
    score_ref[...] = score

    def rank_body(mi, cnt):
        row = score_ref[pl.ds(mi, 1), :]
        beats = (row > score) | ((row == score) & (mi < n_idx))
        return cnt + jnp.where(beats, 1.0, 0.0)

    n_live = jnp.minimum((t0 + tq - 1) // SLC_BLOCK + 1, n_slc)
    cnt = lax.fori_loop(0, n_live, rank_body, jnp.zeros((LANES, tq), F32))
    selected = (cnt < float(min(SLC_TOPK, n_slc))) & (score > -jnp.inf) & (n_idx < t0 // SLC_BLOCK)
    bias_ref[...] = jnp.where(selected, 0.0, NEG)

    t0a = pl.multiple_of(t0, tq)
    s_d = _dot_nt(ks_ref[pl.ds(t0a, tq), :], q2)
    tri = lax.broadcasted_iota(jnp.int32, (tq, tq), 0) <= lax.broadcasted_iota(jnp.int32, (tq, tq), 1)
    m_d, p_d = [], []
    for sr in heads(s_d):
        sr = jnp.where(tri, sr, NEG)
        m = jnp.max(sr, axis=0, keepdims=True)
        m_d.append(m)
        p_d.append(jnp.exp2(sr - m).astype(BF16))
    carry = (jnp.concatenate(m_d, axis=1), _dot(vst_ref[:, pl.ds(t0a, tq)], jnp.concatenate(p_d, axis=1)))

    bpt = tk // SLC_BLOCK
    last_tile = seq // tk - 1

    def qk(kt, slot):
        k0 = pl.multiple_of(jnp.minimum(kt, last_tile) * tk, tk)
        s_ref[slot] = _dot_nt(ks_ref[pl.ds(k0, tk), :], q2)

    def update(kt, slot, carry):
        m_old, acc = carry
        k0 = pl.multiple_of(kt * tk, tk)
        brows = [bias_ref[pl.ds(kt * bpt + j, 1), :] for j in range(bpt)]
        m_new, p_all, alphas = [], [], []
        for r in range(R):
            cols = slice(r * tq, (r + 1) * tq)
            mo = m_old[:, cols]
            mn = mo
            for j in range(bpt):
                blk = s_ref[slot, j * SLC_BLOCK:(j + 1) * SLC_BLOCK, cols]
                mn = jnp.maximum(mn, jnp.max(blk + brows[j], axis=0, keepdims=True))
            p_all.append(jnp.concatenate(
                [jnp.exp2(s_ref[slot, j * SLC_BLOCK:(j + 1) * SLC_BLOCK, cols] + (brows[j] - mn)).astype(BF16)
                 for j in range(bpt)], axis=0))
            alphas.append(jnp.exp2(mo - mn))
            m_new.append(mn)
        pv = _dot(vst_ref[:, pl.ds(k0, tk)], jnp.concatenate(p_all, axis=1))
        return jnp.concatenate(m_new, axis=1), jnp.concatenate(alphas, axis=1) * acc + pv

    def pair_body(ii, carry):
        a = 2 * ii
        qk(a + 1, 1)
        carry = update(a, 0, carry)
        qk(a + 2, 0)
        return update(a + 1, 1, carry)

    n_main = (t0 + tk - 1) // tk
    qk(0, 0)
    _, acc = lax.fori_loop(0, (n_main + 1) // 2, pair_body, carry)
    o_slc = acc[:HEAD_DIM] / acc[HEAD_DIM:HEAD_DIM + 1]

    span = min(WIN_SIZE + tq, seq)
    start = pl.multiple_of(jnp.maximum(t0 - WIN_SIZE, 0), tq)
    s_w = _dot_nt(kw_ref[pl.ds(start, span), :], q2)
    diff = t_lane - (start + lax.broadcasted_iota(jnp.int32, (span, tq), 0))
    wmask = (diff >= 0) & (diff < WIN_SIZE)
    pw = []
    for sr in heads(s_w):
        sr = jnp.where(wmask, sr, NEG)
        m = jnp.max(sr, axis=0, keepdims=True)
        e = jnp.exp2(sr - m)
        pw.append((e / jnp.sum(e, axis=0, keepdims=True)).astype(BF16))
    o_win = _dot(vwt_ref[:, pl.ds(start, span)], jnp.concatenate(pw, axis=1))

    gate_t = gate_ref[...].T
    ys = []
    for r in range(R):
        cols = slice(r * tq, (r + 1) * tq)
        ys.append(gate_t[3 * r:3 * r + 1] * o_cmp[:, cols] + gate_t[3 * r + 1:3 * r + 2] * o_slc[:, cols]
                  + gate_t[3 * r + 2:3 * r + 3] * o_win[:, cols])
    o_ref[...] = jnp.concatenate(ys, axis=0).T.astype(o_ref.dtype)


def _overlap_t(ncp, n_cmp, n_slc):
    cs = np.arange(ncp)[None, :] * CMP_STRIDE
    ss = np.arange(LANES)[:, None] * SLC_BLOCK
    ov = np.clip(np.minimum(cs + CMP_BLOCK, ss + SLC_BLOCK) - np.maximum(cs, ss), 0, None) / CMP_BLOCK
    ov = ov * (np.arange(ncp)[None, :] < n_cmp) * (np.arange(LANES)[:, None] < n_slc)
    return jnp.asarray(ov, BF16)


def _nsa(qa, kc, vct, ks, vst, kw, vwt, gns, tq, tk):
    B, H, S, dh = qa.shape
    ncp = kc.shape[2]
    n_cmp = (S - CMP_BLOCK) // CMP_STRIDE + 1
    n_slc = S // SLC_BLOCK
    assert n_slc <= LANES and S % (2 * tk) == 0 and tk % SLC_BLOCK == 0 and tq == 2 * SLC_BLOCK
    ovt = _overlap_t(ncp, n_cmp, n_slc)
    k_c = pl.BlockSpec((None, None, ncp, dh), lambda b, g, t: (b, g, 0, 0))
    v_c = pl.BlockSpec((None, None, dh, ncp), lambda b, g, t: (b, g, 0, 0))
    k_s = pl.BlockSpec((None, None, S, dh), lambda b, g, t: (b, g, 0, 0))
    v_s = pl.BlockSpec((None, None, vst.shape[2], S), lambda b, g, t: (b, g, 0, 0))
    v_w = pl.BlockSpec((None, None, dh, S), lambda b, g, t: (b, g, 0, 0))
    const = lambda b, g, t: (0, 0)
    kern = functools.partial(_nsa_kernel, tq=tq, tk=tk, n_cmp=n_cmp, n_slc=n_slc, seq=S)
    return pl.pallas_call(
        kern,
        grid=(B, NSA_GROUPS, S // tq),
        in_specs=[pl.BlockSpec((None, NSA_HPG, tq, dh), lambda b, g, t: (b, g, t, 0)),
                  k_c, v_c, k_s, v_s, k_s, v_w,
                  pl.BlockSpec((None, tq, LANES), lambda b, g, t: (b, t, g)),
                  pl.BlockSpec((LANES, ncp), const)],
        out_specs=pl.BlockSpec((None, tq, NSA_HPG * dh), lambda b, g, t: (b, t, g)),
        out_shape=jax.ShapeDtypeStruct((B, S, H * dh), BF16),
        scratch_shapes=[pltpu.VMEM((LANES, tq), F32), pltpu.VMEM((LANES, tq), F32),
                        pltpu.VMEM((2, tk, NSA_HPG * tq), F32)],
        compiler_params=pltpu.CompilerParams(dimension_semantics=("parallel", "parallel", "arbitrary"),
                                             vmem_limit_bytes=VMEM_LIMIT),
        name="nsa",
    )(qa, kc, vct, ks, vst, kw, vwt, gns, ovt)


def _dilated_kernel(q_ref, k_ref, v_ref, o_ref, lse_ref, *, tq, span):
    sd = q_ref.shape[0]

    def tile(i, _):
        r0 = pl.multiple_of(i * tq, tq)
        start = pl.multiple_of(jnp.maximum(r0 - (span - tq), 0), tq)
        qb = q_ref[pl.ds(r0, tq), :]
        kb = k_ref[pl.ds(start, span), :]
        vb = v_ref[pl.ds(start, span), :]
        qi = r0 + lax.broadcasted_iota(jnp.int32, (tq, 1), 0)
        diff = qi - (start + lax.broadcasted_iota(jnp.int32, (tq, span), 1))
        mask = (diff >= 0) & (diff <= DIL_SPAN)
        for h in range(DIL_HPG):
            cols = slice(h * HEAD_DIM, (h + 1) * HEAD_DIM)
            s = jnp.where(mask, _dot_nt(qb[:, cols], kb[:, cols]), NEG)
            m = jnp.max(s, axis=-1, keepdims=True)
            e = jnp.exp(s - m)
            den = jnp.sum(e, axis=-1, keepdims=True)
            o = _dot((e / den).astype(BF16), vb[:, cols])
            o_ref[pl.ds(r0, tq), cols] = o
            lse_ref[pl.ds(r0, tq), cols] = jnp.broadcast_to(m + jnp.log(den), (tq, HEAD_DIM))
        return 0

    lax.fori_loop(0, sd // tq, tile, 0)


def _dilated(qb, kb, vb, g):
    B, S, _ = qb.shape
    d = DIL_PAIRS[g][1]
    assert DIL_PAIRS[g][0] // d == DIL_SPAN and S % d == 0
    sd = S // d
    tq = min(LANES, sd)
    span = min(tq + DIL_SPAN, sd)
    gw = DIL_HPG * HEAD_DIM
    ncols = DIL_HEADS * HEAD_DIM
    ins = [a.reshape(B, sd, d * ncols) for a in (qb, kb, vb)]
    in_blk = pl.BlockSpec((None, sd, gw), lambda b, r: (b, 0, r * DIL_GROUPS + g))
    out_blk = pl.BlockSpec((None, sd, gw), lambda b, r: (b, 0, r))
    o, lse = pl.pallas_call(
        functools.partial(_dilated_kernel, tq=tq, span=span),
        grid=(B, d),
        in_specs=[in_blk, in_blk, in_blk],
        out_specs=[out_blk, out_blk],
        out_shape=[jax.ShapeDtypeStruct((B, sd, d * gw), F32)] * 2,
        compiler_params=pltpu.CompilerParams(dimension_semantics=("parallel", "parallel"),
                                             vmem_limit_bytes=VMEM_LIMIT),
        name=f"dilated{g}",
    )(*ins)
    return o.reshape(B, S, gw), lse.reshape(B, S, gw)


def _mixout_kernel(x_ref, ya_ref, o0_ref, o1_ref, o2_ref, l0_ref, l1_ref, l2_ref, gma_ref, gmb_ref,
                   woa_ref, wob_ref, wout_ref, out_ref):
    l0, l1, l2 = l0_ref[...], l1_ref[...], l2_ref[...]
    mx = jnp.maximum(jnp.maximum(l0, l1), l2)
    e0, e1, e2 = jnp.exp(l0 - mx), jnp.exp(l1 - mx), jnp.exp(l2 - mx)
    yb = (e0 * o0_ref[...] + e1 * o1_ref[...] + e2 * o2_ref[...]) / (e0 + e1 + e2)
    ta = _dot(ya_ref[...], woa_ref[...])
    tb = _dot(yb.astype(BF16), wob_ref[...])
    mixed = gma_ref[...].astype(F32) * ta + gmb_ref[...].astype(F32) * tb
    out_ref[...] = x_ref[...] + _dot(mixed.astype(BF16), wout_ref[...])


def _mixout(x, ya, dil, gma, gmb, w_o_a, w_o_b, w_out, tm):
    B, S, D = x.shape
    row = lambda w: pl.BlockSpec((None, tm, w), lambda b, m: (b, m, 0))
    full = lambda a: pl.BlockSpec(a.shape, lambda b, m: (0, 0))
    gw = DIL_HPG * HEAD_DIM
    wa, wb, wo = w_o_a.astype(BF16), w_o_b.astype(BF16), w_out.astype(BF16)
    return pl.pallas_call(
        _mixout_kernel,
        grid=(B, S // tm),
        in_specs=[row(D), row(ya.shape[-1])] + [row(gw)] * 6 + [row(D), row(D), full(wa), full(wb), full(wo)],
        out_specs=row(D),
        out_shape=jax.ShapeDtypeStruct((B, S, D), F32),
        compiler_params=pltpu.CompilerParams(dimension_semantics=("parallel", "parallel"),
                                             vmem_limit_bytes=VMEM_LIMIT),
        name="mixout",
    )(x, ya, dil[0][0], dil[1][0], dil[2][0], dil[0][1], dil[1][1], dil[2][1], gma, gmb, wa, wb, wo)


def _mlp_kernel(x_ref, g_ref, wup_ref, wdn_ref, out_ref, *, fc):
    x = x_ref[...]
    ms = jnp.mean(x * x, axis=-1, keepdims=True)
    h = (x * lax.rsqrt(ms + NORM_EPS) * g_ref[...]).astype(BF16)
    acc = x
    for off, w in _chunks(wup_ref.shape[1], fc):
        u = jnp.maximum(_dot(h, wup_ref[:, off:off + w]), 0.0)
        acc = acc + _dot((u * u).astype(BF16), wdn_ref[off:off + w, :])
    out_ref[...] = acc


def _mlp(x, norm2_g, w_up, w_down, tm, fc):
    B, S, D = x.shape
    F = w_up.shape[1]
    row = pl.BlockSpec((None, tm, D), lambda b, m: (b, m, 0))
    const = lambda b, m: (0, 0)
    return pl.pallas_call(
        functools.partial(_mlp_kernel, fc=fc),
        grid=(B, S // tm),
        in_specs=[row, pl.BlockSpec((1, D), const), pl.BlockSpec((D, F), const), pl.BlockSpec((F, D), const)],
        out_specs=row,
        out_shape=jax.ShapeDtypeStruct((B, S, D), F32),
        compiler_params=pltpu.CompilerParams(dimension_semantics=("parallel", "parallel"),
                                             vmem_limit_bytes=VMEM_LIMIT),
        name="mlp",
    )(x, norm2_g.reshape(1, D), w_up.astype(BF16), w_down.astype(BF16))


def kernel(x, norm1_g, w_in, q_norm_a, k_norm_cmp, k_norm_slc, k_norm_win, cmp_k_pos, cmp_k_w1, cmp_k_w2,
           cmp_v_pos, cmp_v_w1, cmp_v_w2, q_norm_b, k_norm_b, w_o_a, w_o_b, w_out, norm2_g, w_up, w_down):
    depth = w_in.shape[0]
    for i in range(depth):
        (qa, ks, kw, qb, kb, kc_raw, vc_raw, vs, vw, vb, gma, gmb, gns) = _inproj(
            x, norm1_g[i], w_in[i], q_norm_a[i], k_norm_slc[i], k_norm_win[i], q_norm_b[i], k_norm_b[i], tm=256)
        kc, vc = _compress(kc_raw, vc_raw, k_norm_cmp[i], cmp_k_pos[i], cmp_k_w1[i], cmp_k_w2[i],
                           cmp_v_pos[i], cmp_v_w1[i], cmp_v_w2[i])
        tr = lambda a: jnp.swapaxes(a, 2, 3)
        vst = jnp.concatenate([tr(vs), jnp.ones(vs.shape[:2] + (ONES_ROWS, vs.shape[2]), vs.dtype)], axis=2)
        ya = _nsa(qa, kc, tr(vc), ks, vst, kw, tr(vw), gns, tq=128, tk=512)
        dil = [_dilated(qb, kb, vb, g) for g in range(DIL_GROUPS)]
        x = _mixout(x, ya, dil, gma, gmb, w_o_a[i], w_o_b[i], w_out[i], tm=256)
        x = _mlp(x, norm2_g[i], w_up[i], w_down[i], tm=512, fc=1024)
    return x
```

```python
import functools

import numpy as np
import jax
import jax.numpy as jnp
from jax import lax
from jax.experimental import pallas as pl
from jax.experimental.pallas import tpu as pltpu

HEAD_DIM = 64
HALF = HEAD_DIM // 2
ROPE_THETA = 10000.0
NORM_EPS = 1e-6
NEG = -1e30
NSA_HEADS = 12
NSA_GROUPS = 3
NSA_HPG = NSA_HEADS // NSA_GROUPS
CMP_BLOCK = 32
CMP_STRIDE = 16
SLC_BLOCK = 64
SLC_TOPK = 16
WIN_SIZE = 512
DIL_PAIRS = ((128, 1), (512, 4), (2048, 16))
DIL_GROUPS = len(DIL_PAIRS)
DIL_HPG = 4
DIL_HEADS = DIL_GROUPS * DIL_HPG
DIL_SPAN = 128
LOG2E = float(np.log2(np.e))
ONES_ROWS = 16

LANES = 128
MXU_N = 256
VMEM_LIMIT = 56 * 1024 * 1024

BF16 = jnp.bfloat16
F32 = jnp.float32

ROPE_SEGS = (("qa", 768), ("qb", 768), ("kb", 768), ("ks", 192), ("kw", 192))
PLAIN_SEGS = (("kvc", 384), ("vs", 192), ("vw", 192), ("vb", 768))
SIG_SEGS = (("gma", 1024), ("gmb", 1024), ("gns", 384))
ROPE_W = sum(w for _, w in ROPE_SEGS)
PLAIN_W = sum(w for _, w in PLAIN_SEGS)
SIG_W = sum(w for _, w in SIG_SEGS)
HEAD_MAJOR = ("qa", "ks", "kw", "vs", "vw")


def _dot(a, b):
    return jnp.dot(a, b, preferred_element_type=F32)


def _dot_nt(a, b):
    return lax.dot_general(a, b, (((1,), (1,)), ((), ())), preferred_element_type=F32)


def _chunks(total, width):
    out, off = [], 0
    while off < total:
        w = min(width, total - off)
        out.append((off, w))
        off += w
    return out


def _seg_lookup(segs, col):
    off = 0
    for name, w in segs:
        if col < off + w:
            return name, col - off
        off += w
    raise ValueError(col)


def _inproj_kernel(x_ref, g1_ref, w_ref, gain_ref, bd_ref, cos_ref, sin_ref,
                   qa_ref, ks_ref, kw_ref, qb_ref, kb_ref,
                   kvc_ref, vs_ref, vw_ref, vb_ref,
                   gma_ref, gmb_ref, gns_ref):
    outs = dict(qa=qa_ref, ks=ks_ref, kw=kw_ref, qb=qb_ref, kb=kb_ref, kvc=kvc_ref,
                vs=vs_ref, vw=vw_ref, vb=vb_ref, gma=gma_ref, gmb=gmb_ref, gns=gns_ref)

    def emit(segs, col, val):
        name, rel = _seg_lookup(segs, col)
        ref = outs[name]
        if name in HEAD_MAJOR:
            for p in range(LANES // HEAD_DIM):
                nm, r = _seg_lookup(segs, col + p * HEAD_DIM)
                outs[nm][r // HEAD_DIM] = val[:, p * HEAD_DIM:(p + 1) * HEAD_DIM].astype(outs[nm].dtype)
        else:
            nm2, _ = _seg_lookup(segs, col + HEAD_DIM)
            if nm2 != name:
                raise ValueError("unaligned dense segment")
            ref[:, rel:rel + LANES] = val.astype(ref.dtype)

    x = x_ref[...]
    ms = jnp.mean(x * x, axis=-1, keepdims=True)
    h = (x * lax.rsqrt(ms + NORM_EPS) * g1_ref[...]).astype(BF16)

    lane = lax.broadcasted_iota(jnp.int32, (x.shape[0], LANES), 1)
    first_half = (lane % HEAD_DIM) < HALF
    cos = cos_ref[...]
    sin = sin_ref[...]
    bd = bd_ref[...]

    for off, w in _chunks(ROPE_W, MXU_N):
        y = _dot(h, w_ref[:, off:off + w])
        for u in range(w // LANES):
            c0 = off + u * LANES
            yy = y[:, u * LANES:(u + 1) * LANES]
            msq = _dot((yy * yy).astype(BF16), bd)
            yn = yy * lax.rsqrt(msq + NORM_EPS) * gain_ref[:, c0:c0 + LANES]
            rot = jnp.where(first_half, pltpu.roll(yn, LANES - HALF, 1), pltpu.roll(yn, HALF, 1))
            emit(ROPE_SEGS, c0, yn * cos + rot * sin)

    base = ROPE_W
    for off, w in _chunks(PLAIN_W, MXU_N):
        y = _dot(h, w_ref[:, base + off:base + off + w])
        for u in range(w // LANES):
            emit(PLAIN_SEGS, off + u * LANES, y[:, u * LANES:(u + 1) * LANES])

    base = ROPE_W + PLAIN_W
    for off, w in _chunks(SIG_W, MXU_N):
        y = jax.nn.sigmoid(_dot(h, w_ref[:, base + off:base + off + w]))
        for u in range(w // LANES):
            emit(SIG_SEGS, off + u * LANES, y[:, u * LANES:(u + 1) * LANES])


def _rope_tables(positions, width):
    inv_freq = np.power(ROPE_THETA, -np.arange(HALF, dtype=np.float64) / HALF)
    ang = np.asarray(positions, np.float64)[:, None] * inv_freq[None, :]
    reps = width // HEAD_DIM
    cos = np.tile(np.concatenate([np.cos(ang), np.cos(ang)], axis=1), (1, reps))
    sin = np.tile(np.concatenate([-np.sin(ang), np.sin(ang)], axis=1), (1, reps))
    return jnp.asarray(cos, F32), jnp.asarray(sin, F32)


def _inproj(x, norm1_g, w_in, q_norm_a, k_norm_slc, k_norm_win, q_norm_b, k_norm_b, tm):
    B, S, D = x.shape
    scale = HEAD_DIM ** -0.5
    sp = np.cumsum((0, 768, 192, 192, 192, 192, 192, 192, 36, 768, 768, 768, 1024, 1024))
    names = ("qa", "kc", "vc", "ks", "vs", "kw", "vw", "gns", "qb", "kb", "vb", "gma", "gmb")
    w_bf = w_in.astype(BF16)
    col = {n: w_bf[:, int(sp[i]):int(sp[i + 1])] for i, n in enumerate(names)}
    gpg = NSA_HPG * 3
    gns = jnp.concatenate(
        [jnp.pad(col["gns"][:, g * gpg:(g + 1) * gpg], ((0, 0), (0, LANES - gpg))) for g in range(NSA_GROUPS)],
        axis=1)
    col["gns"] = gns
    col["kvc"] = jnp.concatenate([col["kc"], col["vc"]], axis=1)
    w_p = jnp.concatenate([col[n] for n, _ in ROPE_SEGS + PLAIN_SEGS + SIG_SEGS], axis=1)
    ncol = ROPE_W + PLAIN_W + SIG_W
    assert w_p.shape == (D, ncol)

    gain = jnp.concatenate([
        jnp.tile(q_norm_a * (scale * LOG2E), NSA_HEADS), jnp.tile(q_norm_b * scale, DIL_HEADS),
        jnp.tile(k_norm_b, DIL_HEADS), jnp.tile(k_norm_slc, NSA_GROUPS), jnp.tile(k_norm_win, NSA_GROUPS)
    ]).reshape(1, ROPE_W).astype(F32)
    bd = jnp.asarray(np.kron(np.eye(LANES // HEAD_DIM), np.full((HEAD_DIM, HEAD_DIM), 1.0 / HEAD_DIM)), BF16)
    cos, sin = _rope_tables(np.arange(S), LANES)

    def hm(nh, dt):
        return (jax.ShapeDtypeStruct((B, nh, S, HEAD_DIM), dt),
                pl.BlockSpec((None, nh, tm, HEAD_DIM), lambda b, m: (b, 0, m, 0)))

    def dense(w, dt):
        return (jax.ShapeDtypeStruct((B, S, w), dt), pl.BlockSpec((None, tm, w), lambda b, m: (b, m, 0)))

    outs = [hm(NSA_HEADS, BF16), hm(NSA_GROUPS, BF16), hm(NSA_GROUPS, BF16),
            dense(768, F32), dense(768, F32),
            dense(384, F32),
            hm(NSA_GROUPS, BF16), hm(NSA_GROUPS, BF16), dense(768, F32),
            dense(1024, BF16), dense(1024, BF16), dense(384, F32)]
    const = lambda b, m: (0, 0)
    return pl.pallas_call(
        _inproj_kernel,
        grid=(B, S // tm),
        in_specs=[pl.BlockSpec((None, tm, D), lambda b, m: (b, m, 0)),
                  pl.BlockSpec((1, D), const),
                  pl.BlockSpec((D, ncol), const),
                  pl.BlockSpec((1, ROPE_W), const),
                  pl.BlockSpec((LANES, LANES), const),
                  pl.BlockSpec((tm, LANES), lambda b, m: (m, 0)),
                  pl.BlockSpec((tm, LANES), lambda b, m: (m, 0))],
        out_specs=[o[1] for o in outs],
        out_shape=[o[0] for o in outs],
        compiler_params=pltpu.CompilerParams(dimension_semantics=("parallel", "arbitrary"),
                                             vmem_limit_bytes=VMEM_LIMIT),
        name="inproj",
    )(x, norm1_g.reshape(1, D), w_p, gain, bd, cos, sin)


def _gelu_tanh(x):
    return 0.5 * x * (1.0 + jnp.tanh(np.sqrt(2.0 / np.pi) * (x + 0.044715 * (x * x * x))))


def _compress_kernel(x0_ref, x1_ref, x2_ref, pk_ref, pv_ref, w1k_ref, w2k_ref, w1v_ref, w2vt_ref,
                     gain_ref, cos_ref, sin_ref, kc_ref, vct_ref):
    nc = x0_ref.shape[0] // CMP_STRIDE
    half = CMP_STRIDE * HEAD_DIM
    per_blk = LANES // HEAD_DIM
    xs = [[x_ref[pl.ds(l, nc, stride=CMP_STRIDE), :] for l in range(CMP_STRIDE)] for x_ref in (x0_ref, x1_ref, x2_ref)]

    def hidden(head, pos_ref, w1_ref):
        blk, sub = divmod(head, per_blk)
        cols = slice(sub * HEAD_DIM, (sub + 1) * HEAD_DIM)
        x = [xs[blk][l][:, cols] for l in range(CMP_STRIDE)]
        xa = jnp.concatenate([x[l] + pos_ref[l:l + 1, :] for l in range(CMP_STRIDE)], axis=1)
        xb = jnp.concatenate([x[l] + pos_ref[CMP_STRIDE + l:CMP_STRIDE + l + 1, :] for l in range(CMP_STRIDE)], axis=1)
        a = _dot(xa.astype(BF16), w1_ref[0:half, :])
        b = _dot(xb.astype(BF16), w1_ref[half:2 * half, :])
        pre = a + pltpu.roll(b, nc - 1, 0)
        return _gelu_tanh(pre).astype(BF16)

    for g in range(NSA_GROUPS):
        kc = _dot(hidden(g, pk_ref, w1k_ref), w2k_ref[...])
        vct = _dot_nt(w2vt_ref[...], hidden(NSA_GROUPS + g, pv_ref, w1v_ref))
        ms = jnp.mean(kc * kc, axis=-1, keepdims=True)
        kn = kc * lax.rsqrt(ms + NORM_EPS) * gain_ref[...]
        rot = jnp.concatenate([kn[:, HALF:], kn[:, :HALF]], axis=-1)
        kc_ref[g] = (kn * cos_ref[...] + rot * sin_ref[...]).astype(kc_ref.dtype)
        vct_ref[g] = vct.astype(vct_ref.dtype)


def _compress(kvc_raw, k_norm_cmp, kpos, kw1, kw2, vpos, vw1, vw2):
    B, S, w = kvc_raw.shape
    G, dh = NSA_GROUPS, HEAD_DIM
    assert w == 2 * G * dh == 3 * LANES
    nc = S // CMP_STRIDE
    cos, sin = _rope_tables(np.arange(nc) * CMP_STRIDE + CMP_BLOCK - 1, dh)
    const = lambda b: (0, 0)
    xblk = lambda j: pl.BlockSpec((None, S, LANES), lambda b: (b, 0, j))
    hid = kw1.shape[1]
    return pl.pallas_call(
        _compress_kernel,
        grid=(B,),
        in_specs=[xblk(0), xblk(1), xblk(2),
                  pl.BlockSpec((CMP_BLOCK, dh), const), pl.BlockSpec((CMP_BLOCK, dh), const),
                  pl.BlockSpec((CMP_BLOCK * dh, hid), const), pl.BlockSpec((hid, dh), const),
                  pl.BlockSpec((CMP_BLOCK * dh, hid), const), pl.BlockSpec((dh, hid), const),
                  pl.BlockSpec((1, dh), const), pl.BlockSpec((nc, dh), const), pl.BlockSpec((nc, dh), const)],
        out_specs=[pl.BlockSpec((None, G, nc, dh), lambda b: (b, 0, 0, 0)),
                   pl.BlockSpec((None, G, dh, nc), lambda b: (b, 0, 0, 0))],
        out_shape=[jax.ShapeDtypeStruct((B, G, nc, dh), BF16), jax.ShapeDtypeStruct((B, G, dh, nc), BF16)],
        compiler_params=pltpu.CompilerParams(dimension_semantics=("parallel",), vmem_limit_bytes=VMEM_LIMIT),
        name="compress",
    )(kvc_raw, kvc_raw, kvc_raw, kpos, vpos, kw1.astype(BF16), kw2.astype(BF16), vw1.astype(BF16),
      vw2.T.astype(BF16), k_norm_cmp.reshape(1, dh).astype(F32), cos, sin)


def _nsa_kernel(q_ref, kc_ref, vct_ref, ks_ref, vst_ref, kw_ref, vwt_ref, gate_ref, ovt_ref,
                o_ref, score_ref, bias_ref, s_ref, *, tq, tk, n_cmp, n_slc, seq):
    R = NSA_HPG
    qt = pl.program_id(2)
    t0 = qt * tq
    q2 = q_ref[...].reshape(R * tq, HEAD_DIM)
    ncp = kc_ref.shape[0]
    t_lane = t0 + lax.broadcasted_iota(jnp.int32, (1, tq), 1)

    def heads(a):
        return [a[:, r * tq:(r + 1) * tq] for r in range(R)]

    s_c = _dot_nt(kc_ref[...], q2)
    c_idx = lax.broadcasted_iota(jnp.int32, (ncp, tq), 0)
    cmask = ((c_idx * CMP_STRIDE + (CMP_BLOCK - 1)) <= t_lane) & (c_idx < n_cmp)
    ps = []
    for sr in heads(s_c):
        sr = jnp.where(cmask, sr, NEG)
        m = jnp.max(sr, axis=0, keepdims=True)
        e = jnp.where(cmask, jnp.exp2(sr - m), 0.0)
        den = jnp.sum(e, axis=0, keepdims=True)
        ps.append(e / jnp.where(den > 0.0, den, 1.0))
    o_cmp = _dot(vct_ref[...], jnp.concatenate(ps, axis=1).astype(BF16))

    psum = ps[0]
    for r in range(1, R):
        psum = psum + ps[r]
    p_hi = psum.astype(BF16)
    p_lo = (psum - p_hi.astype(F32)).astype(BF16)
    imp = _dot(ovt_ref[...], p_hi) + _dot(ovt_ref[...], p_lo)
    n_idx = lax.broadcasted_iota(jnp.int32, (LANES, tq), 0)
    cur = (t0 + lax.broadcasted_iota(jnp.int32, (LANES, tq), 1)) // SLC_BLOCK
    forced = (n_idx == 0) | (n_idx == cur) | (n_idx == cur - 1)
    score = jnp.where(forced, jnp.inf, jnp.where(n_idx <= cur, imp, -jnp.inf))
    score_ref[...] = score

    def rank_body(mi, cnt):
        row = score_ref[pl.ds(mi, 1), :]
        beats = (row > score) | ((row == score) & (mi < n_idx))
        return cnt + jnp.where(beats, 1.0, 0.0)

    n_live = jnp.minimum((t0 + tq - 1) // SLC_BLOCK + 1, n_slc)
    cnt = lax.fori_loop(0, n_live, rank_body, jnp.zeros((LANES, tq), F32))
    selected = (cnt < float(min(SLC_TOPK, n_slc))) & (score > -jnp.inf) & (n_idx < t0 // SLC_BLOCK)
    bias_ref[...] = jnp.where(selected, 0.0, NEG)

    t0a = pl.multiple_of(t0, tq)
    s_d = _dot_nt(ks_ref[pl.ds(t0a, tq), :], q2)
    tri = lax.broadcasted_iota(jnp.int32, (tq, tq), 0) <= lax.broadcasted_iota(jnp.int32, (tq, tq), 1)
    m_d, p_d = [], []
    for sr in heads(s_d):
        sr = jnp.where(tri, sr, NEG)
        m = jnp.max(sr, axis=0, keepdims=True)
        m_d.append(m)
        p_d.append(jnp.exp2(sr - m).astype(BF16))
    carry = (jnp.concatenate(m_d, axis=1), _dot(vst_ref[:, pl.ds(t0a, tq)], jnp.concatenate(p_d, axis=1)))

    bpt = tk // SLC_BLOCK
    last_tile = seq // tk - 1

    def qk(kt, slot):
        k0 = pl.multiple_of(jnp.minimum(kt, last_tile) * tk, tk)
        s_ref[slot] = _dot_nt(ks_ref[pl.ds(k0, tk), :], q2)

    def update(kt, slot, carry):
        m_old, acc = carry
        k0 = pl.multiple_of(kt * tk, tk)
        brows = [bias_ref[pl.ds(kt * bpt + j, 1), :] for j in range(bpt)]
        m_new, p_all, alphas = [], [], []
        for r in range(R):
            cols = slice(r * tq, (r + 1) * tq)
            mo = m_old[:, cols]
            mn = mo
            for j in range(bpt):
                blk = s_ref[slot, j * SLC_BLOCK:(j + 1) * SLC_BLOCK, cols]
                mn = jnp.maximum(mn, jnp.max(blk + brows[j], axis=0, keepdims=True))
            p_all.append(jnp.concatenate(
                [jnp.exp2(s_ref[slot, j * SLC_BLOCK:(j + 1) * SLC_BLOCK, cols] + (brows[j] - mn)).astype(BF16)
                 for j in range(bpt)], axis=0))
            alphas.append(jnp.exp2(mo - mn))
            m_new.append(mn)
        pv = _dot(vst_ref[:, pl.ds(k0, tk)], jnp.concatenate(p_all, axis=1))
        return jnp.concatenate(m_new, axis=1), jnp.concatenate(alphas, axis=1) * acc + pv

    def pair_body(ii, carry):
        a = 2 * ii
        qk(a + 1, 1)
        carry = update(a, 0, carry)
        qk(a + 2, 0)
        return update(a + 1, 1, carry)

    n_main = (t0 + tk - 1) // tk
    qk(0, 0)
    _, acc = lax.fori_loop(0, (n_main + 1) // 2, pair_body, carry)
    o_slc = acc[:HEAD_DIM] / acc[HEAD_DIM:HEAD_DIM + 1]

    span = min(WIN_SIZE + tq, seq)
    start = pl.multiple_of(jnp.maximum(t0 - WIN_SIZE, 0), tq)
    s_w = _dot_nt(kw_ref[pl.ds(start, span), :], q2)
    diff = t_lane - (start + lax.broadcasted_iota(jnp.int32, (span, tq), 0))
    wmask = (diff >= 0) & (diff < WIN_SIZE)
    pw = []
    for sr in heads(s_w):
        sr = jnp.where(wmask, sr, NEG)
        m = jnp.max(sr, axis=0, keepdims=True)
        e = jnp.exp2(sr - m)
        pw.append((e / jnp.sum(e, axis=0, keepdims=True)).astype(BF16))
    o_win = _dot(vwt_ref[:, pl.ds(start, span)], jnp.concatenate(pw, axis=1))

    gate_t = gate_ref[...].T
    ys = []
    for r in range(R):
        cols = slice(r * tq, (r + 1) * tq)
        ys.append(gate_t[3 * r:3 * r + 1] * o_cmp[:, cols] + gate_t[3 * r + 1:3 * r + 2] * o_slc[:, cols]
                  + gate_t[3 * r + 2:3 * r + 3] * o_win[:, cols])
    o_ref[...] = jnp.concatenate(ys, axis=0).T.astype(o_ref.dtype)


def _overlap_t(ncp, n_cmp, n_slc):
    cs = np.arange(ncp)[None, :] * CMP_STRIDE
    ss = np.arange(LANES)[:, None] * SLC_BLOCK
    ov = np.clip(np.minimum(cs + CMP_BLOCK, ss + SLC_BLOCK) - np.maximum(cs, ss), 0, None) / CMP_BLOCK
    ov = ov * (np.arange(ncp)[None, :] < n_cmp) * (np.arange(LANES)[:, None] < n_slc)
    return jnp.asarray(ov, BF16)


def _nsa(qa, kc, vct, ks, vst, kw, vwt, gns, tq, tk):
    B, H, S, dh = qa.shape
    ncp = kc.shape[2]
    n_cmp = (S - CMP_BLOCK) // CMP_STRIDE + 1
    n_slc = S // SLC_BLOCK
    assert n_slc <= LANES and S % (2 * tk) == 0 and tk % SLC_BLOCK == 0 and tq == 2 * SLC_BLOCK
    ovt = _overlap_t(ncp, n_cmp, n_slc)
    k_c = pl.BlockSpec((None, None, ncp, dh), lambda b, g, t: (b, g, 0, 0))
    v_c = pl.BlockSpec((None, None, dh, ncp), lambda b, g, t: (b, g, 0, 0))
    k_s = pl.BlockSpec((None, None, S, dh), lambda b, g, t: (b, g, 0, 0))
    v_s = pl.BlockSpec((None, None, vst.shape[2], S), lambda b, g, t: (b, g, 0, 0))
    v_w = pl.BlockSpec((None, None, dh, S), lambda b, g, t: (b, g, 0, 0))
    const = lambda b, g, t: (0, 0)
    kern = functools.partial(_nsa_kernel, tq=tq, tk=tk, n_cmp=n_cmp, n_slc=n_slc, seq=S)
    return pl.pallas_call(
        kern,
        grid=(B, NSA_GROUPS, S // tq),
        in_specs=[pl.BlockSpec((None, NSA_HPG, tq, dh), lambda b, g, t: (b, g, t, 0)),
                  k_c, v_c, k_s, v_s, k_s, v_w,
                  pl.BlockSpec((None, tq, LANES), lambda b, g, t: (b, t, g)),
                  pl.BlockSpec((LANES, ncp), const)],
        out_specs=pl.BlockSpec((None, tq, NSA_HPG * dh), lambda b, g, t: (b, t, g)),
        out_shape=jax.ShapeDtypeStruct((B, S, H * dh), BF16),
        scratch_shapes=[pltpu.VMEM((LANES, tq), F32), pltpu.VMEM((LANES, tq), F32),
                        pltpu.VMEM((2, tk, NSA_HPG * tq), F32)],
        compiler_params=pltpu.CompilerParams(dimension_semantics=("parallel", "parallel", "arbitrary"),
                                             vmem_limit_bytes=VMEM_LIMIT),
        name="nsa",
    )(qa, kc, vct, ks, vst, kw, vwt, gns, ovt)


def _dilated_kernel(q_ref, kp_ref, kc_ref, vp_ref, vc_ref, o_ref, lse_ref, vt_ref, *, d, tq):
    span = 2 * tq
    row = lax.broadcasted_iota(jnp.int32, (span, tq), 0)
    lane = lax.broadcasted_iota(jnp.int32, (span, tq), 1)
    diff = (lane + tq) - row
    has_prev = pl.program_id(1) > 0
    mask = (diff >= 0) & (diff <= DIL_SPAN) & ((row >= tq) | has_prev)
    for r in range(d):
        rows = pl.ds(r, tq, stride=d)
        qb = q_ref[rows, :]
        kb = jnp.concatenate([kp_ref[rows, :], kc_ref[rows, :]], axis=0)
        vt_ref[...] = jnp.concatenate([vp_ref[rows, :], vc_ref[rows, :]], axis=0).T.astype(BF16)
        o_t, l_t = [], []
        for h in range(LANES // HEAD_DIM):
            cols = slice(h * HEAD_DIM, (h + 1) * HEAD_DIM)
            s = _dot_nt(kb[:, cols].astype(BF16), qb[:, cols].astype(BF16))
            s = jnp.where(mask, s, NEG)
            m = jnp.max(s, axis=0, keepdims=True)
            e = jnp.exp(s - m)
            den = jnp.sum(e, axis=0, keepdims=True)
            o_t.append(_dot(vt_ref[cols, :], (e / den).astype(BF16)))
            l_t.append(jnp.broadcast_to(m + jnp.log(den), (HEAD_DIM, tq)))
        o_ref[rows, :] = jnp.concatenate(o_t, axis=0).T
        lse_ref[rows, :] = jnp.concatenate(l_t, axis=0).T


def _dilated(qb, kb, vb, g):
    B, S, _ = qb.shape
    d = DIL_PAIRS[g][1]
    tq = min(LANES, S // d)
    assert DIL_PAIRS[g][0] // d == DIL_SPAN and DIL_SPAN <= tq and S % (d * tq) == 0
    gw = DIL_HPG * HEAD_DIM
    nb = gw // LANES
    cur = pl.BlockSpec((None, d * tq, LANES), lambda b, i, c: (b, i, g * nb + c))
    prev = pl.BlockSpec((None, d * tq, LANES), lambda b, i, c: (b, jnp.maximum(i - 1, 0), g * nb + c))
    out_blk = pl.BlockSpec((None, d * tq, LANES), lambda b, i, c: (b, i, c))
    return pl.pallas_call(
        functools.partial(_dilated_kernel, d=d, tq=tq),
        grid=(B, S // (d * tq), nb),
        in_specs=[cur, prev, cur, prev, cur],
        out_specs=[out_blk, out_blk],
        out_shape=[jax.ShapeDtypeStruct((B, S, gw), F32)] * 2,
        scratch_shapes=[pltpu.VMEM((LANES, 2 * tq), BF16)],
        compiler_params=pltpu.CompilerParams(dimension_semantics=("parallel", "arbitrary", "arbitrary"),
                                             vmem_limit_bytes=VMEM_LIMIT),
        name=f"dilated{g}",
    )(qb, kb, kb, vb, vb)


def _mixout_kernel(x_ref, ya_ref, o0_ref, o1_ref, o2_ref, l0_ref, l1_ref, l2_ref, gma_ref, gmb_ref,
                   woa_ref, wob_ref, wout_ref, out_ref):
    l0, l1, l2 = l0_ref[...], l1_ref[...], l2_ref[...]
    mx = jnp.maximum(jnp.maximum(l0, l1), l2)
    e0, e1, e2 = jnp.exp(l0 - mx), jnp.exp(l1 - mx), jnp.exp(l2 - mx)
    yb = (e0 * o0_ref[...] + e1 * o1_ref[...] + e2 * o2_ref[...]) / (e0 + e1 + e2)
    ta = _dot(ya_ref[...], woa_ref[...])
    tb = _dot(yb.astype(BF16), wob_ref[...])
    mixed = gma_ref[...].astype(F32) * ta + gmb_ref[...].astype(F32) * tb
    out_ref[...] = x_ref[...] + _dot(mixed.astype(BF16), wout_ref[...])


def _mixout(x, ya, dil, gma, gmb, w_o_a, w_o_b, w_out, tm):
    B, S, D = x.shape
    row = lambda w: pl.BlockSpec((None, tm, w), lambda b, m: (b, m, 0))
    full = lambda a: pl.BlockSpec(a.shape, lambda b, m: (0, 0))
    gw = DIL_HPG * HEAD_DIM
    wa, wb, wo = w_o_a.astype(BF16), w_o_b.astype(BF16), w_out.astype(BF16)
    return pl.pallas_call(
        _mixout_kernel,
        grid=(B, S // tm),
        in_specs=[row(D), row(ya.shape[-1])] + [row(gw)] * 6 + [row(D), row(D), full(wa), full(wb), full(wo)],
        out_specs=row(D),
        out_shape=jax.ShapeDtypeStruct((B, S, D), F32),
        compiler_params=pltpu.CompilerParams(dimension_semantics=("parallel", "parallel"),
                                             vmem_limit_bytes=VMEM_LIMIT),
        name="mixout",
    )(x, ya, dil[0][0], dil[1][0], dil[2][0], dil[0][1], dil[1][1], dil[2][1], gma, gmb, wa, wb, wo)


def _mlp_kernel(x_ref, g_ref, wup_ref, wdn_ref, out_ref, *, fc):
    x = x_ref[...]
    ms = jnp.mean(x * x, axis=-1, keepdims=True)
    h = (x * lax.rsqrt(ms + NORM_EPS) * g_ref[...]).astype(BF16)
    acc = x
    for off, w in _chunks(wup_ref.shape[1], fc):
        u = jnp.maximum(_dot(h, wup_ref[:, off:off + w]), 0.0)
        acc = acc + _dot((u * u).astype(BF16), wdn_ref[off:off + w, :])
    out_ref[...] = acc


def _mlp(x, norm2_g, w_up, w_down, tm, fc):
    B, S, D = x.shape
    F = w_up.shape[1]
    row = pl.BlockSpec((None, tm, D), lambda b, m: (b, m, 0))
    const = lambda b, m: (0, 0)
    return pl.pallas_call(
        functools.partial(_mlp_kernel, fc=fc),
        grid=(B, S // tm),
        in_specs=[row, pl.BlockSpec((1, D), const), pl.BlockSpec((D, F), const), pl.BlockSpec((F, D), const)],
        out_specs=row,
        out_shape=jax.ShapeDtypeStruct((B, S, D), F32),
        compiler_params=pltpu.CompilerParams(dimension_semantics=("parallel", "parallel"),
                                             vmem_limit_bytes=VMEM_LIMIT),
        name="mlp",
    )(x, norm2_g.reshape(1, D), w_up.astype(BF16), w_down.astype(BF16))


def kernel(x, norm1_g, w_in, q_norm_a, k_norm_cmp, k_norm_slc, k_norm_win, cmp_k_pos, cmp_k_w1, cmp_k_w2,
           cmp_v_pos, cmp_v_w1, cmp_v_w2, q_norm_b, k_norm_b, w_o_a, w_o_b, w_out, norm2_g, w_up, w_down):
    depth = w_in.shape[0]
    for i in range(depth):
        (qa, ks, kw, qb, kb, kvc_raw, vs, vw, vb, gma, gmb, gns) = _inproj(
            x, norm1_g[i], w_in[i], q_norm_a[i], k_norm_slc[i], k_norm_win[i], q_norm_b[i], k_norm_b[i], tm=256)
        kc, vct = _compress(kvc_raw, k_norm_cmp[i], cmp_k_pos[i], cmp_k_w1[i], cmp_k_w2[i],
                           cmp_v_pos[i], cmp_v_w1[i], cmp_v_w2[i])
        tr = lambda a: jnp.swapaxes(a, 2, 3)
        vst = jnp.concatenate([tr(vs), jnp.ones(vs.shape[:2] + (ONES_ROWS, vs.shape[2]), vs.dtype)], axis=2)
        ya = _nsa(qa, kc, vct, ks, vst, kw, tr(vw), gns, tq=128, tk=512)
        dil = [_dilated(qb, kb, vb, g) for g in range(DIL_GROUPS)]
        x = _mixout(x, ya, dil, gma, gmb, w_o_a[i], w_o_b[i], w_out[i], tm=256)
        x = _mlp(x, norm2_g[i], w_up[i], w_down[i], tm=512, fc=1024)
    return x
```

```python
import functools

import numpy as np
import jax
import jax.numpy as jnp
from jax import lax
from jax.experimental import pallas as pl
from jax.experimental.pallas import tpu as pltpu

HEAD_DIM = 64
HALF = HEAD_DIM // 2
ROPE_THETA = 10000.0
NORM_EPS = 1e-6
NEG = -1e30
NSA_HEADS = 12
NSA_GROUPS = 3
NSA_HPG = NSA_HEADS // NSA_GROUPS
CMP_BLOCK = 32
CMP_STRIDE = 16
SLC_BLOCK = 64
SLC_TOPK = 16
WIN_SIZE = 512
DIL_PAIRS = ((128, 1), (512, 4), (2048, 16))
DIL_GROUPS = len(DIL_PAIRS)
DIL_HPG = 4
DIL_HEADS = DIL_GROUPS * DIL_HPG
DIL_SPAN = 128
LOG2E = float(np.log2(np.e))
ONES_ROWS = 16

LANES = 128
MXU_N = 256
VMEM_LIMIT = 56 * 1024 * 1024

BF16 = jnp.bfloat16
F32 = jnp.float32

ROPE_SEGS = (("qa", 768), ("qb", 768), ("kb", 768), ("ks", 192), ("kw", 192))
PLAIN_SEGS = (("kvc", 384), ("vb", 768))
SIG_SEGS = (("gma", 1024), ("gmb", 1024), ("gns", 384))
ROPE_W = sum(w for _, w in ROPE_SEGS)
PLAIN_W = sum(w for _, w in PLAIN_SEGS)
SIG_W = sum(w for _, w in SIG_SEGS)
HEAD_MAJOR = ("qa", "ks", "kw")


def _dot(a, b):
    return jnp.dot(a, b, preferred_element_type=F32)


def _dot_nt(a, b):
    return lax.dot_general(a, b, (((1,), (1,)), ((), ())), preferred_element_type=F32)


def _chunks(total, width):
    out, off = [], 0
    while off < total:
        w = min(width, total - off)
        out.append((off, w))
        off += w
    return out


def _seg_lookup(segs, col):
    off = 0
    for name, w in segs:
        if col < off + w:
            return name, col - off
        off += w
    raise ValueError(col)


def _inproj_kernel(x_ref, g1_ref, w_ref, wvt_ref, gain_ref, bd_ref, cos_ref, sin_ref,
                   qa_ref, ks_ref, kw_ref, qb_ref, kb_ref,
                   kvc_ref, vb_ref, gma_ref, gmb_ref, gns_ref, vst_ref, vwt_ref):
    outs = dict(qa=qa_ref, ks=ks_ref, kw=kw_ref, qb=qb_ref, kb=kb_ref, kvc=kvc_ref,
                vb=vb_ref, gma=gma_ref, gmb=gmb_ref, gns=gns_ref)

    def emit(segs, col, val):
        name, rel = _seg_lookup(segs, col)
        ref = outs[name]
        if name in HEAD_MAJOR:
            for p in range(LANES // HEAD_DIM):
                nm, r = _seg_lookup(segs, col + p * HEAD_DIM)
                outs[nm][r // HEAD_DIM] = val[:, p * HEAD_DIM:(p + 1) * HEAD_DIM].astype(outs[nm].dtype)
        else:
            nm2, _ = _seg_lookup(segs, col + HEAD_DIM)
            if nm2 != name:
                raise ValueError("unaligned dense segment")
            ref[:, rel:rel + LANES] = val.astype(ref.dtype)

    x = x_ref[...]
    ms = jnp.mean(x * x, axis=-1, keepdims=True)
    h = (x * lax.rsqrt(ms + NORM_EPS) * g1_ref[...]).astype(BF16)

    lane = lax.broadcasted_iota(jnp.int32, (x.shape[0], LANES), 1)
    first_half = (lane % HEAD_DIM) < HALF
    cos = cos_ref[...]
    sin = sin_ref[...]
    bd = bd_ref[...]

    for off, w in _chunks(ROPE_W, MXU_N):
        y = _dot(h, w_ref[:, off:off + w])
        for u in range(w // LANES):
            c0 = off + u * LANES
            yy = y[:, u * LANES:(u + 1) * LANES]
            msq = _dot((yy * yy).astype(BF16), bd)
            yn = yy * lax.rsqrt(msq + NORM_EPS) * gain_ref[:, c0:c0 + LANES]
            rot = jnp.where(first_half, pltpu.roll(yn, LANES - HALF, 1), pltpu.roll(yn, HALF, 1))
            emit(ROPE_SEGS, c0, yn * cos + rot * sin)

    base = ROPE_W
    for off, w in _chunks(PLAIN_W, MXU_N):
        y = _dot(h, w_ref[:, base + off:base + off + w])
        for u in range(w // LANES):
            emit(PLAIN_SEGS, off + u * LANES, y[:, u * LANES:(u + 1) * LANES])

    base = ROPE_W + PLAIN_W
    for off, w in _chunks(SIG_W, MXU_N):
        y = jax.nn.sigmoid(_dot(h, w_ref[:, base + off:base + off + w]))
        for u in range(w // LANES):
            emit(SIG_SEGS, off + u * LANES, y[:, u * LANES:(u + 1) * LANES])

    v_t = _dot_nt(wvt_ref[...], h)
    ones = jnp.ones((ONES_ROWS, x.shape[0]), vst_ref.dtype)
    for i, ref in enumerate((vst_ref, vwt_ref)):
        for g in range(NSA_GROUPS):
            r0 = (i * NSA_GROUPS + g) * HEAD_DIM
            ref[g, 0:HEAD_DIM, :] = v_t[r0:r0 + HEAD_DIM].astype(ref.dtype)
            ref[g, HEAD_DIM:HEAD_DIM + ONES_ROWS, :] = ones


def _rope_tables(positions, width):
    inv_freq = np.power(ROPE_THETA, -np.arange(HALF, dtype=np.float64) / HALF)
    ang = np.asarray(positions, np.float64)[:, None] * inv_freq[None, :]
    reps = width // HEAD_DIM
    cos = np.tile(np.concatenate([np.cos(ang), np.cos(ang)], axis=1), (1, reps))
    sin = np.tile(np.concatenate([-np.sin(ang), np.sin(ang)], axis=1), (1, reps))
    return jnp.asarray(cos, F32), jnp.asarray(sin, F32)


def _inproj(x, norm1_g, w_in, q_norm_a, k_norm_slc, k_norm_win, q_norm_b, k_norm_b, tm):
    B, S, D = x.shape
    scale = HEAD_DIM ** -0.5
    sp = np.cumsum((0, 768, 192, 192, 192, 192, 192, 192, 36, 768, 768, 768, 1024, 1024))
    names = ("qa", "kc", "vc", "ks", "vs", "kw", "vw", "gns", "qb", "kb", "vb", "gma", "gmb")
    w_bf = w_in.astype(BF16)
    col = {n: w_bf[:, int(sp[i]):int(sp[i + 1])] for i, n in enumerate(names)}
    gpg = NSA_HPG * 3
    gns = jnp.concatenate(
        [jnp.pad(col["gns"][:, g * gpg:(g + 1) * gpg], ((0, 0), (0, LANES - gpg))) for g in range(NSA_GROUPS)],
        axis=1)
    col["gns"] = gns
    col["kvc"] = jnp.concatenate([col["kc"], col["vc"]], axis=1)
    w_p = jnp.concatenate([col[n] for n, _ in ROPE_SEGS + PLAIN_SEGS + SIG_SEGS], axis=1)
    wvt = jnp.concatenate([col["vs"], col["vw"]], axis=1).T
    ncol = ROPE_W + PLAIN_W + SIG_W
    assert w_p.shape == (D, ncol)

    gain = jnp.concatenate([
        jnp.tile(q_norm_a * (scale * LOG2E), NSA_HEADS), jnp.tile(q_norm_b * scale, DIL_HEADS),
        jnp.tile(k_norm_b, DIL_HEADS), jnp.tile(k_norm_slc, NSA_GROUPS), jnp.tile(k_norm_win, NSA_GROUPS)
    ]).reshape(1, ROPE_W).astype(F32)
    bd = jnp.asarray(np.kron(np.eye(LANES // HEAD_DIM), np.full((HEAD_DIM, HEAD_DIM), 1.0 / HEAD_DIM)), BF16)
    cos, sin = _rope_tables(np.arange(S), LANES)

    def hm(nh, dt):
        return (jax.ShapeDtypeStruct((B, nh, S, HEAD_DIM), dt),
                pl.BlockSpec((None, nh, tm, HEAD_DIM), lambda b, m: (b, 0, m, 0)))

    def vt():
        rows = HEAD_DIM + ONES_ROWS
        return (jax.ShapeDtypeStruct((B, NSA_GROUPS, rows, S), BF16),
                pl.BlockSpec((None, NSA_GROUPS, rows, tm), lambda b, m: (b, 0, 0, m)))

    def dense(w, dt):
        return (jax.ShapeDtypeStruct((B, S, w), dt), pl.BlockSpec((None, tm, w), lambda b, m: (b, m, 0)))

    outs = [hm(NSA_HEADS, BF16), hm(NSA_GROUPS, BF16), hm(NSA_GROUPS, BF16),
            dense(768, F32), dense(768, F32),
            dense(384, F32),
            dense(768, F32),
            dense(1024, BF16), dense(1024, BF16), dense(384, F32),
            vt(), vt()]
    const = lambda b, m: (0, 0)
    return pl.pallas_call(
        _inproj_kernel,
        grid=(B, S // tm),
        in_specs=[pl.BlockSpec((None, tm, D), lambda b, m: (b, m, 0)),
                  pl.BlockSpec((1, D), const),
                  pl.BlockSpec((D, ncol), const),
                  pl.BlockSpec(wvt.shape, const),
                  pl.BlockSpec((1, ROPE_W), const),
                  pl.BlockSpec((LANES, LANES), const),
                  pl.BlockSpec((tm, LANES), lambda b, m: (m, 0)),
                  pl.BlockSpec((tm, LANES), lambda b, m: (m, 0))],
        out_specs=[o[1] for o in outs],
        out_shape=[o[0] for o in outs],
        compiler_params=pltpu.CompilerParams(dimension_semantics=("parallel", "arbitrary"),
                                             vmem_limit_bytes=VMEM_LIMIT),
        name="inproj",
    )(x, norm1_g.reshape(1, D), w_p, wvt, gain, bd, cos, sin)


def _gelu_tanh(x):
    return 0.5 * x * (1.0 + jnp.tanh(np.sqrt(2.0 / np.pi) * (x + 0.044715 * (x * x * x))))


def _compress_kernel(x0_ref, x1_ref, x2_ref, pk_ref, pv_ref, w1k_ref, w2k_ref, w1v_ref, w2vt_ref,
                     gain_ref, cos_ref, sin_ref, kc_ref, vct_ref):
    nc = x0_ref.shape[0] // CMP_STRIDE
    half = CMP_STRIDE * HEAD_DIM
    per_blk = LANES // HEAD_DIM
    xs = [[x_ref[pl.ds(l, nc, stride=CMP_STRIDE), :] for l in range(CMP_STRIDE)] for x_ref in (x0_ref, x1_ref, x2_ref)]

    def hidden(head, pos_ref, w1_ref):
        blk, sub = divmod(head, per_blk)
        cols = slice(sub * HEAD_DIM, (sub + 1) * HEAD_DIM)
        x = [xs[blk][l][:, cols] for l in range(CMP_STRIDE)]
        xa = jnp.concatenate([x[l] + pos_ref[l:l + 1, :] for l in range(CMP_STRIDE)], axis=1)
        xb = jnp.concatenate([x[l] + pos_ref[CMP_STRIDE + l:CMP_STRIDE + l + 1, :] for l in range(CMP_STRIDE)], axis=1)
        a = _dot(xa.astype(BF16), w1_ref[0:half, :])
        b = _dot(xb.astype(BF16), w1_ref[half:2 * half, :])
        pre = a + pltpu.roll(b, nc - 1, 0)
        return _gelu_tanh(pre).astype(BF16)

    for g in range(NSA_GROUPS):
        kc = _dot(hidden(g, pk_ref, w1k_ref), w2k_ref[...])
        vct = _dot_nt(w2vt_ref[...], hidden(NSA_GROUPS + g, pv_ref, w1v_ref))
        ms = jnp.mean(kc * kc, axis=-1, keepdims=True)
        kn = kc * lax.rsqrt(ms + NORM_EPS) * gain_ref[...]
        rot = jnp.concatenate([kn[:, HALF:], kn[:, :HALF]], axis=-1)
        kc_ref[g] = (kn * cos_ref[...] + rot * sin_ref[...]).astype(kc_ref.dtype)
        vct_ref[g] = vct.astype(vct_ref.dtype)


def _compress(kvc_raw, k_norm_cmp, kpos, kw1, kw2, vpos, vw1, vw2):
    B, S, w = kvc_raw.shape
    G, dh = NSA_GROUPS, HEAD_DIM
    assert w == 2 * G * dh == 3 * LANES
    nc = S // CMP_STRIDE
    cos, sin = _rope_tables(np.arange(nc) * CMP_STRIDE + CMP_BLOCK - 1, dh)
    const = lambda b: (0, 0)
    xblk = lambda j: pl.BlockSpec((None, S, LANES), lambda b: (b, 0, j))
    hid = kw1.shape[1]
    return pl.pallas_call(
        _compress_kernel,
        grid=(B,),
        in_specs=[xblk(0), xblk(1), xblk(2),
                  pl.BlockSpec((CMP_BLOCK, dh), const), pl.BlockSpec((CMP_BLOCK, dh), const),
                  pl.BlockSpec((CMP_BLOCK * dh, hid), const), pl.BlockSpec((hid, dh), const),
                  pl.BlockSpec((CMP_BLOCK * dh, hid), const), pl.BlockSpec((dh, hid), const),
                  pl.BlockSpec((1, dh), const), pl.BlockSpec((nc, dh), const), pl.BlockSpec((nc, dh), const)],
        out_specs=[pl.BlockSpec((None, G, nc, dh), lambda b: (b, 0, 0, 0)),
                   pl.BlockSpec((None, G, dh, nc), lambda b: (b, 0, 0, 0))],
        out_shape=[jax.ShapeDtypeStruct((B, G, nc, dh), BF16), jax.ShapeDtypeStruct((B, G, dh, nc), BF16)],
        compiler_params=pltpu.CompilerParams(dimension_semantics=("parallel",), vmem_limit_bytes=VMEM_LIMIT),
        name="compress",
    )(kvc_raw, kvc_raw, kvc_raw, kpos, vpos, kw1.astype(BF16), kw2.astype(BF16), vw1.astype(BF16),
      vw2.T.astype(BF16), k_norm_cmp.reshape(1, dh).astype(F32), cos, sin)


def _nsa_kernel(q_ref, kc_ref, vct_ref, ks_ref, vst_ref, kw_ref, vwt_ref, gate_ref, ovt_ref,
                o_ref, score_ref, bias_ref, s_ref, *, tq, tk, n_cmp, n_slc, seq):
    R = NSA_HPG
    qt = pl.program_id(2)
    t0 = qt * tq
    t0a = pl.multiple_of(t0, tq)
    q2 = q_ref[...].reshape(R * tq, HEAD_DIM)
    ncp = kc_ref.shape[0]
    t_lane = t0 + lax.broadcasted_iota(jnp.int32, (1, tq), 1)
    bpt = tk // SLC_BLOCK
    last_tile = seq // tk - 1
    span = min(WIN_SIZE + tq, seq)
    start = pl.multiple_of(jnp.maximum(t0 - WIN_SIZE, 0), tq)

    def heads(a):
        return [a[:, r * tq:(r + 1) * tq] for r in range(R)]

    def qk(kt, slot):
        k0 = pl.multiple_of(jnp.minimum(kt, last_tile) * tk, tk)
        s_ref[slot] = _dot_nt(ks_ref[pl.ds(k0, tk), :], q2)

    s_c = _dot_nt(kc_ref[...], q2)
    s_d = _dot_nt(ks_ref[pl.ds(t0a, tq), :], q2)
    s_w = _dot_nt(kw_ref[pl.ds(start, span), :], q2)
    qk(0, 0)

    c_idx = lax.broadcasted_iota(jnp.int32, (ncp, tq), 0)
    cmask = ((c_idx * CMP_STRIDE + (CMP_BLOCK - 1)) <= t_lane) & (c_idx < n_cmp)
    ps = []
    for sr in heads(s_c):
        sr = jnp.where(cmask, sr, NEG)
        m = jnp.max(sr, axis=0, keepdims=True)
        e = jnp.where(cmask, jnp.exp2(sr - m), 0.0)
        den = jnp.sum(e, axis=0, keepdims=True)
        ps.append(e / jnp.where(den > 0.0, den, 1.0))
    o_cmp = _dot(vct_ref[...], jnp.concatenate(ps, axis=1).astype(BF16))

    psum = ps[0]
    for r in range(1, R):
        psum = psum + ps[r]
    p_hi = psum.astype(BF16)
    p_lo = (psum - p_hi.astype(F32)).astype(BF16)
    imp = _dot(ovt_ref[...], p_hi) + _dot(ovt_ref[...], p_lo)
    n_idx = lax.broadcasted_iota(jnp.int32, (LANES, tq), 0)
    cur = (t0 + lax.broadcasted_iota(jnp.int32, (LANES, tq), 1)) // SLC_BLOCK
    forced = (n_idx == 0) | (n_idx == cur) | (n_idx == cur - 1)
    score = jnp.where(forced, jnp.inf, jnp.where(n_idx <= cur, imp, -jnp.inf))
    score_ref[...] = score

    tri = lax.broadcasted_iota(jnp.int32, (tq, tq), 0) <= lax.broadcasted_iota(jnp.int32, (tq, tq), 1)
    m_d, p_d = [], []
    for sr in heads(s_d):
        sr = jnp.where(tri, sr, NEG)
        m = jnp.max(sr, axis=0, keepdims=True)
        m_d.append(m)
        p_d.append(jnp.exp2(sr - m).astype(BF16))
    carry = (jnp.concatenate(m_d, axis=1), _dot(vst_ref[:, pl.ds(t0a, tq)], jnp.concatenate(p_d, axis=1)))

    diff = t_lane - (start + lax.broadcasted_iota(jnp.int32, (span, tq), 0))
    wmask = (diff >= 0) & (diff < WIN_SIZE)
    pw = []
    for sr in heads(s_w):
        sr = jnp.where(wmask, sr, NEG)
        m = jnp.max(sr, axis=0, keepdims=True)
        pw.append(jnp.exp2(sr - m).astype(BF16))
    acc_w = _dot(vwt_ref[:, pl.ds(start, span)], jnp.concatenate(pw, axis=1))
    o_win = acc_w[:HEAD_DIM] / acc_w[HEAD_DIM:HEAD_DIM + 1]

    nb = -(-n_slc // 8) * 8
    unroll = 4
    score_b, n_b = score[:nb], n_idx[:nb]

    def rank_body(i, cnt):
        for u in range(unroll):
            mi = i * unroll + u
            row = score_ref[pl.ds(mi, 1), :]
            beats = (row > score_b) | ((row == score_b) & (mi < n_b))
            cnt = cnt + jnp.where(beats, 1.0, 0.0)
        return cnt

    n_live = jnp.minimum((t0 + tq - 1) // SLC_BLOCK + 1, n_slc)
    cnt = lax.fori_loop(0, (n_live + unroll - 1) // unroll, rank_body, jnp.zeros((nb, tq), F32))
    selected = (cnt < float(min(SLC_TOPK, n_slc))) & (score_b > -jnp.inf) & (n_b < t0 // SLC_BLOCK)
    bias_ref[:nb] = jnp.where(selected, 0.0, NEG)
    if nb < LANES:
        bias_ref[nb:] = jnp.full((LANES - nb, tq), NEG, F32)

    def update(kt, slot, carry):
        m_old, acc = carry
        k0 = pl.multiple_of(kt * tk, tk)
        brows = [bias_ref[pl.ds(kt * bpt + j, 1), :] for j in range(bpt)]
        m_new, p_all, alphas = [], [], []
        for r in range(R):
            cols = slice(r * tq, (r + 1) * tq)
            mo = m_old[:, cols]
            mn = mo
            for j in range(bpt):
                blk = s_ref[slot, j * SLC_BLOCK:(j + 1) * SLC_BLOCK, cols]
                mn = jnp.maximum(mn, jnp.max(blk + brows[j], axis=0, keepdims=True))
            p_all.append(jnp.concatenate(
                [jnp.exp2(s_ref[slot, j * SLC_BLOCK:(j + 1) * SLC_BLOCK, cols] + (brows[j] - mn)).astype(BF16)
                 for j in range(bpt)], axis=0))
            alphas.append(jnp.exp2(mo - mn))
            m_new.append(mn)
        pv = _dot(vst_ref[:, pl.ds(k0, tk)], jnp.concatenate(p_all, axis=1))
        return jnp.concatenate(m_new, axis=1), jnp.concatenate(alphas, axis=1) * acc + pv

    def pair_body(ii, carry):
        a = 2 * ii
        qk(a + 1, 1)
        carry = update(a, 0, carry)
        qk(a + 2, 0)
        return update(a + 1, 1, carry)

    n_main = (t0 + tk - 1) // tk
    _, acc = lax.fori_loop(0, (n_main + 1) // 2, pair_body, carry)
    o_slc = acc[:HEAD_DIM] / acc[HEAD_DIM:HEAD_DIM + 1]

    gate_t = gate_ref[...].T
    ys = []
    for r in range(R):
        cols = slice(r * tq, (r + 1) * tq)
        ys.append(gate_t[3 * r:3 * r + 1] * o_cmp[:, cols] + gate_t[3 * r + 1:3 * r + 2] * o_slc[:, cols]
                  + gate_t[3 * r + 2:3 * r + 3] * o_win[:, cols])
    o_ref[...] = jnp.concatenate(ys, axis=0).T.astype(o_ref.dtype)


def _overlap_t(ncp, n_cmp, n_slc):
    cs = np.arange(ncp)[None, :] * CMP_STRIDE
    ss = np.arange(LANES)[:, None] * SLC_BLOCK
    ov = np.clip(np.minimum(cs + CMP_BLOCK, ss + SLC_BLOCK) - np.maximum(cs, ss), 0, None) / CMP_BLOCK
    ov = ov * (np.arange(ncp)[None, :] < n_cmp) * (np.arange(LANES)[:, None] < n_slc)
    return jnp.asarray(ov, BF16)


def _nsa(qa, kc, vct, ks, vst, kw, vwt, gns, tq, tk):
    B, H, S, dh = qa.shape
    ncp = kc.shape[2]
    n_cmp = (S - CMP_BLOCK) // CMP_STRIDE + 1
    n_slc = S // SLC_BLOCK
    assert n_slc <= LANES and S % (2 * tk) == 0 and tk % SLC_BLOCK == 0 and tq == 2 * SLC_BLOCK
    ovt = _overlap_t(ncp, n_cmp, n_slc)
    k_c = pl.BlockSpec((None, None, ncp, dh), lambda b, g, t: (b, g, 0, 0))
    v_c = pl.BlockSpec((None, None, dh, ncp), lambda b, g, t: (b, g, 0, 0))
    k_s = pl.BlockSpec((None, None, S, dh), lambda b, g, t: (b, g, 0, 0))
    v_s = pl.BlockSpec((None, None, vst.shape[2], S), lambda b, g, t: (b, g, 0, 0))
    v_w = pl.BlockSpec((None, None, vwt.shape[2], S), lambda b, g, t: (b, g, 0, 0))
    const = lambda b, g, t: (0, 0)
    kern = functools.partial(_nsa_kernel, tq=tq, tk=tk, n_cmp=n_cmp, n_slc=n_slc, seq=S)
    return pl.pallas_call(
        kern,
        grid=(B, NSA_GROUPS, S // tq),
        in_specs=[pl.BlockSpec((None, NSA_HPG, tq, dh), lambda b, g, t: (b, g, t, 0)),
                  k_c, v_c, k_s, v_s, k_s, v_w,
                  pl.BlockSpec((None, tq, LANES), lambda b, g, t: (b, t, g)),
                  pl.BlockSpec((LANES, ncp), const)],
        out_specs=pl.BlockSpec((None, tq, NSA_HPG * dh), lambda b, g, t: (b, t, g)),
        out_shape=jax.ShapeDtypeStruct((B, S, H * dh), BF16),
        scratch_shapes=[pltpu.VMEM((LANES, tq), F32), pltpu.VMEM((LANES, tq), F32),
                        pltpu.VMEM((2, tk, NSA_HPG * tq), F32)],
        compiler_params=pltpu.CompilerParams(dimension_semantics=("parallel", "parallel", "arbitrary"),
                                             vmem_limit_bytes=VMEM_LIMIT),
        name="nsa",
    )(qa, kc, vct, ks, vst, kw, vwt, gns, ovt)


def _dilated_kernel(q_ref, kp_ref, kc_ref, vp_ref, vc_ref, o_ref, lse_ref, vt_ref, *, d, tq, nq):
    span = 2 * tq
    row = lax.broadcasted_iota(jnp.int32, (span, tq), 0)
    lane = lax.broadcasted_iota(jnp.int32, (span, tq), 1)
    diff = (lane + tq) - row
    band = (diff >= 0) & (diff <= DIL_SPAN)
    has_prev = pl.program_id(1) > 0
    first = band & ((row >= tq) | has_prev)
    for j in range(nq):
        for r in range(d):
            slot = j * d + r
            rows = pl.ds(r + j * tq * d, tq, stride=d)
            qb = q_ref[rows, :]
            if j == 0:
                kb = jnp.concatenate([kp_ref[pl.ds(r, tq, stride=d), :], kc_ref[rows, :]], axis=0)
                vb = jnp.concatenate([vp_ref[pl.ds(r, tq, stride=d), :], vc_ref[rows, :]], axis=0)
            else:
                kv_rows = pl.ds(r + (j - 1) * tq * d, span, stride=d)
                kb, vb = kc_ref[kv_rows, :], vc_ref[kv_rows, :]
            mask = first if j == 0 else band
            vt_ref[slot] = vb.T.astype(BF16)
            o_t, l_t = [], []
            for h in range(LANES // HEAD_DIM):
                cols = slice(h * HEAD_DIM, (h + 1) * HEAD_DIM)
                s = _dot_nt(kb[:, cols].astype(BF16), qb[:, cols].astype(BF16))
                s = jnp.where(mask, s, NEG)
                m = jnp.max(s, axis=0, keepdims=True)
                e = jnp.exp(s - m)
                den = jnp.sum(e, axis=0, keepdims=True)
                o_t.append(_dot(vt_ref[slot, cols, :], (e / den).astype(BF16)))
                l_t.append(jnp.broadcast_to(m + jnp.log(den), (HEAD_DIM, tq)))
            o_ref[rows, :] = jnp.concatenate(o_t, axis=0).T
            lse_ref[rows, :] = jnp.concatenate(l_t, axis=0).T


def _dilated(qb, kb, vb, g, block_tokens):
    B, S, _ = qb.shape
    d = DIL_PAIRS[g][1]
    tq = min(LANES, S // d)
    nq = max(block_tokens // (d * tq), 1)
    assert DIL_PAIRS[g][0] // d == DIL_SPAN and DIL_SPAN <= tq and S % (nq * d * tq) == 0
    gw = DIL_HPG * HEAD_DIM
    nb = gw // LANES
    cur = pl.BlockSpec((None, nq * d * tq, LANES), lambda b, i, c: (b, i, g * nb + c))
    prev = pl.BlockSpec((None, d * tq, LANES), lambda b, i, c: (b, jnp.maximum(i * nq - 1, 0), g * nb + c))
    out_blk = pl.BlockSpec((None, nq * d * tq, LANES), lambda b, i, c: (b, i, c))
    return pl.pallas_call(
        functools.partial(_dilated_kernel, d=d, tq=tq, nq=nq),
        grid=(B, S // (nq * d * tq), nb),
        in_specs=[cur, prev, cur, prev, cur],
        out_specs=[out_blk, out_blk],
        out_shape=[jax.ShapeDtypeStruct((B, S, gw), F32)] * 2,
        scratch_shapes=[pltpu.VMEM((nq * d, LANES, 2 * tq), BF16)],
        compiler_params=pltpu.CompilerParams(dimension_semantics=("parallel", "arbitrary", "arbitrary"),
                                             vmem_limit_bytes=VMEM_LIMIT),
        name=f"dilated{g}",
    )(qb, kb, kb, vb, vb)


def _mixout_kernel(x_ref, ya_ref, o0_ref, o1_ref, o2_ref, l0_ref, l1_ref, l2_ref, gma_ref, gmb_ref,
                   woa_ref, wob_ref, wout_ref, out_ref):
    l0, l1, l2 = l0_ref[...], l1_ref[...], l2_ref[...]
    mx = jnp.maximum(jnp.maximum(l0, l1), l2)
    e0, e1, e2 = jnp.exp(l0 - mx), jnp.exp(l1 - mx), jnp.exp(l2 - mx)
    yb = (e0 * o0_ref[...] + e1 * o1_ref[...] + e2 * o2_ref[...]) / (e0 + e1 + e2)
    ta = _dot(ya_ref[...], woa_ref[...])
    tb = _dot(yb.astype(BF16), wob_ref[...])
    mixed = gma_ref[...].astype(F32) * ta + gmb_ref[...].astype(F32) * tb
    out_ref[...] = x_ref[...] + _dot(mixed.astype(BF16), wout_ref[...])


def _mixout(x, ya, dil, gma, gmb, w_o_a, w_o_b, w_out, tm):
    B, S, D = x.shape
    row = lambda w: pl.BlockSpec((None, tm, w), lambda b, m: (b, m, 0))
    full = lambda a: pl.BlockSpec(a.shape, lambda b, m: (0, 0))
    gw = DIL_HPG * HEAD_DIM
    wa, wb, wo = w_o_a.astype(BF16), w_o_b.astype(BF16), w_out.astype(BF16)
    return pl.pallas_call(
        _mixout_kernel,
        grid=(B, S // tm),
        in_specs=[row(D), row(ya.shape[-1])] + [row(gw)] * 6 + [row(D), row(D), full(wa), full(wb), full(wo)],
        out_specs=row(D),
        out_shape=jax.ShapeDtypeStruct((B, S, D), F32),
        compiler_params=pltpu.CompilerParams(dimension_semantics=("parallel", "parallel"),
                                             vmem_limit_bytes=VMEM_LIMIT),
        name="mixout",
    )(x, ya, dil[0][0], dil[1][0], dil[2][0], dil[0][1], dil[1][1], dil[2][1], gma, gmb, wa, wb, wo)


def _mlp_kernel(x_ref, g_ref, wup_ref, wdn_ref, out_ref, *, fc):
    x = x_ref[...]
    ms = jnp.mean(x * x, axis=-1, keepdims=True)
    h = (x * lax.rsqrt(ms + NORM_EPS) * g_ref[...]).astype(BF16)
    acc = x
    for off, w in _chunks(wup_ref.shape[1], fc):
        u = jnp.maximum(_dot(h, wup_ref[:, off:off + w]), 0.0)
        acc = acc + _dot((u * u).astype(BF16), wdn_ref[off:off + w, :])
    out_ref[...] = acc


def _mlp(x, norm2_g, w_up, w_down, tm, fc):
    B, S, D = x.shape
    F = w_up.shape[1]
    row = pl.BlockSpec((None, tm, D), lambda b, m: (b, m, 0))
    const = lambda b, m: (0, 0)
    return pl.pallas_call(
        functools.partial(_mlp_kernel, fc=fc),
        grid=(B, S // tm),
        in_specs=[row, pl.BlockSpec((1, D), const), pl.BlockSpec((D, F), const), pl.BlockSpec((F, D), const)],
        out_specs=row,
        out_shape=jax.ShapeDtypeStruct((B, S, D), F32),
        compiler_params=pltpu.CompilerParams(dimension_semantics=("parallel", "parallel"),
                                             vmem_limit_bytes=VMEM_LIMIT),
        name="mlp",
    )(x, norm2_g.reshape(1, D), w_up.astype(BF16), w_down.astype(BF16))


def kernel(x, norm1_g, w_in, q_norm_a, k_norm_cmp, k_norm_slc, k_norm_win, cmp_k_pos, cmp_k_w1, cmp_k_w2,
           cmp_v_pos, cmp_v_w1, cmp_v_w2, q_norm_b, k_norm_b, w_o_a, w_o_b, w_out, norm2_g, w_up, w_down):
    depth = w_in.shape[0]
    for i in range(depth):
        (qa, ks, kw, qb, kb, kvc_raw, vb, gma, gmb, gns, vst, vwt) = _inproj(
            x, norm1_g[i], w_in[i], q_norm_a[i], k_norm_slc[i], k_norm_win[i], q_norm_b[i], k_norm_b[i], tm=256)
        kc, vct = _compress(kvc_raw, k_norm_cmp[i], cmp_k_pos[i], cmp_k_w1[i], cmp_k_w2[i],
                           cmp_v_pos[i], cmp_v_w1[i], cmp_v_w2[i])
        ya = _nsa(qa, kc, vct, ks, vst, kw, vwt, gns, tq=128, tk=512)
        dil = [_dilated(qb, kb, vb, g, block_tokens=1024) for g in range(DIL_GROUPS)]
        x = _mixout(x, ya, dil, gma, gmb, w_o_a[i], w_o_b[i], w_out[i], tm=256)
        x = _mlp(x, norm2_g[i], w_up[i], w_down[i], tm=512, fc=1024)
    return x
```

```python
import functools

import numpy as np
import jax
import jax.numpy as jnp
from jax import lax
from jax.experimental import pallas as pl
from jax.experimental.pallas import tpu as pltpu

HEAD_DIM = 64
HALF = HEAD_DIM // 2
ROPE_THETA = 10000.0
NORM_EPS = 1e-6
NEG = -1e30
NSA_HEADS = 12
NSA_GROUPS = 3
NSA_HPG = NSA_HEADS // NSA_GROUPS
CMP_BLOCK = 32
CMP_STRIDE = 16
SLC_BLOCK = 64
SLC_TOPK = 16
WIN_SIZE = 512
DIL_PAIRS = ((128, 1), (512, 4), (2048, 16))
DIL_GROUPS = len(DIL_PAIRS)
DIL_HPG = 4
DIL_HEADS = DIL_GROUPS * DIL_HPG
DIL_SPAN = 128
LOG2E = float(np.log2(np.e))
ONES_ROWS = 16

LANES = 128
MXU_N = 256
VMEM_LIMIT = 56 * 1024 * 1024

BF16 = jnp.bfloat16
F32 = jnp.float32

ROPE_SEGS = (("qa", 768), ("qb", 768), ("kb", 768), ("ks", 192), ("kw", 192))
PLAIN_SEGS = (("kvc", 384), ("vb", 768))
SIG_SEGS = (("gma", 1024), ("gmb", 1024), ("gns", 384))
ROPE_W = sum(w for _, w in ROPE_SEGS)
PLAIN_W = sum(w for _, w in PLAIN_SEGS)
SIG_W = sum(w for _, w in SIG_SEGS)
HEAD_MAJOR = ("qa", "ks", "kw")


def _dot(a, b):
    return jnp.dot(a, b, preferred_element_type=F32)


def _dot_nt(a, b):
    return lax.dot_general(a, b, (((1,), (1,)), ((), ())), preferred_element_type=F32)


def _chunks(total, width):
    out, off = [], 0
    while off < total:
        w = min(width, total - off)
        out.append((off, w))
        off += w
    return out


def _seg_lookup(segs, col):
    off = 0
    for name, w in segs:
        if col < off + w:
            return name, col - off
        off += w
    raise ValueError(col)


def _inproj_kernel(x_ref, g1_ref, w_ref, wvt_ref, gain_ref, bd_ref, cos_ref, sin_ref,
                   qa_ref, ks_ref, kw_ref, qb_ref, kb_ref,
                   kvc_ref, vb_ref, gma_ref, gmb_ref, gns_ref, vst_ref, vwt_ref):
    outs = dict(qa=qa_ref, ks=ks_ref, kw=kw_ref, qb=qb_ref, kb=kb_ref, kvc=kvc_ref,
                vb=vb_ref, gma=gma_ref, gmb=gmb_ref, gns=gns_ref)

    def emit(segs, col, val):
        name, rel = _seg_lookup(segs, col)
        ref = outs[name]
        if name in HEAD_MAJOR:
            for p in range(LANES // HEAD_DIM):
                nm, r = _seg_lookup(segs, col + p * HEAD_DIM)
                outs[nm][r // HEAD_DIM] = val[:, p * HEAD_DIM:(p + 1) * HEAD_DIM].astype(outs[nm].dtype)
        else:
            nm2, _ = _seg_lookup(segs, col + HEAD_DIM)
            if nm2 != name:
                raise ValueError("unaligned dense segment")
            ref[:, rel:rel + LANES] = val.astype(ref.dtype)

    x = x_ref[...]
    ms = jnp.mean(x * x, axis=-1, keepdims=True)
    h = (x * lax.rsqrt(ms + NORM_EPS) * g1_ref[...]).astype(BF16)

    lane = lax.broadcasted_iota(jnp.int32, (x.shape[0], LANES), 1)
    first_half = (lane % HEAD_DIM) < HALF
    cos = cos_ref[...]
    sin = sin_ref[...]
    bd = bd_ref[...]

    def rope_epilogue(off, w, y):
        msq = _dot((y * y).astype(BF16), bd[:w, :w])
        yn = y * lax.rsqrt(msq + NORM_EPS) * gain_ref[:, off:off + w]
        for u in range(w // LANES):
            v = yn[:, u * LANES:(u + 1) * LANES]
            rot = jnp.where(first_half, pltpu.roll(v, LANES - HALF, 1), pltpu.roll(v, HALF, 1))
            emit(ROPE_SEGS, off + u * LANES, v * cos + rot * sin)

    def plain_epilogue(off, w, y):
        for u in range(w // LANES):
            emit(PLAIN_SEGS, off + u * LANES, y[:, u * LANES:(u + 1) * LANES])

    def sig_epilogue(off, w, y):
        y = jax.nn.sigmoid(y)
        for u in range(w // LANES):
            emit(SIG_SEGS, off + u * LANES, y[:, u * LANES:(u + 1) * LANES])

    work = ([(0, off, w, rope_epilogue) for off, w in _chunks(ROPE_W, MXU_N)]
            + [(ROPE_W, off, w, plain_epilogue) for off, w in _chunks(PLAIN_W, MXU_N)]
            + [(ROPE_W + PLAIN_W, off, w, sig_epilogue) for off, w in _chunks(SIG_W, MXU_N)])
    pending = None
    for base, off, w, epilogue in work:
        y = _dot(h, w_ref[:, base + off:base + off + w])
        if pending is not None:
            pending[0](pending[1], pending[2], pending[3])
        pending = (epilogue, off, w, y)
    pending[0](pending[1], pending[2], pending[3])

    v_t = _dot_nt(wvt_ref[...], h)
    ones = jnp.ones((ONES_ROWS, x.shape[0]), vst_ref.dtype)
    for i, ref in enumerate((vst_ref, vwt_ref)):
        for g in range(NSA_GROUPS):
            r0 = (i * NSA_GROUPS + g) * HEAD_DIM
            ref[g, 0:HEAD_DIM, :] = v_t[r0:r0 + HEAD_DIM].astype(ref.dtype)
            ref[g, HEAD_DIM:HEAD_DIM + ONES_ROWS, :] = ones


def _rope_tables(positions, width):
    inv_freq = np.power(ROPE_THETA, -np.arange(HALF, dtype=np.float64) / HALF)
    ang = np.asarray(positions, np.float64)[:, None] * inv_freq[None, :]
    reps = width // HEAD_DIM
    cos = np.tile(np.concatenate([np.cos(ang), np.cos(ang)], axis=1), (1, reps))
    sin = np.tile(np.concatenate([-np.sin(ang), np.sin(ang)], axis=1), (1, reps))
    return jnp.asarray(cos, F32), jnp.asarray(sin, F32)


def _inproj(x, norm1_g, w_in, q_norm_a, k_norm_slc, k_norm_win, q_norm_b, k_norm_b, tm):
    B, S, D = x.shape
    scale = HEAD_DIM ** -0.5
    sp = np.cumsum((0, 768, 192, 192, 192, 192, 192, 192, 36, 768, 768, 768, 1024, 1024))
    names = ("qa", "kc", "vc", "ks", "vs", "kw", "vw", "gns", "qb", "kb", "vb", "gma", "gmb")
    w_bf = w_in.astype(BF16)
    col = {n: w_bf[:, int(sp[i]):int(sp[i + 1])] for i, n in enumerate(names)}
    gpg = NSA_HPG * 3
    gns = jnp.concatenate(
        [jnp.pad(col["gns"][:, g * gpg:(g + 1) * gpg], ((0, 0), (0, LANES - gpg))) for g in range(NSA_GROUPS)],
        axis=1)
    col["gns"] = gns
    col["kvc"] = jnp.concatenate([col["kc"], col["vc"]], axis=1)
    w_p = jnp.concatenate([col[n] for n, _ in ROPE_SEGS + PLAIN_SEGS + SIG_SEGS], axis=1)
    wvt = jnp.concatenate([col["vs"], col["vw"]], axis=1).T
    ncol = ROPE_W + PLAIN_W + SIG_W
    assert w_p.shape == (D, ncol)

    gain = jnp.concatenate([
        jnp.tile(q_norm_a * (scale * LOG2E), NSA_HEADS), jnp.tile(q_norm_b * scale, DIL_HEADS),
        jnp.tile(k_norm_b, DIL_HEADS), jnp.tile(k_norm_slc, NSA_GROUPS), jnp.tile(k_norm_win, NSA_GROUPS)
    ]).reshape(1, ROPE_W).astype(F32)
    bd = jnp.asarray(np.kron(np.eye(MXU_N // HEAD_DIM), np.full((HEAD_DIM, HEAD_DIM), 1.0 / HEAD_DIM)), BF16)
    cos, sin = _rope_tables(np.arange(S), LANES)

    def hm(nh, dt):
        return (jax.ShapeDtypeStruct((B, nh, S, HEAD_DIM), dt),
                pl.BlockSpec((None, nh, tm, HEAD_DIM), lambda b, m: (b, 0, m, 0)))

    def vt():
        rows = HEAD_DIM + ONES_ROWS
        return (jax.ShapeDtypeStruct((B, NSA_GROUPS, rows, S), BF16),
                pl.BlockSpec((None, NSA_GROUPS, rows, tm), lambda b, m: (b, 0, 0, m)))

    def dense(w, dt):
        return (jax.ShapeDtypeStruct((B, S, w), dt), pl.BlockSpec((None, tm, w), lambda b, m: (b, m, 0)))

    outs = [hm(NSA_HEADS, BF16), hm(NSA_GROUPS, BF16), hm(NSA_GROUPS, BF16),
            dense(768, F32), dense(768, F32),
            dense(384, F32),
            dense(768, F32),
            dense(1024, BF16), dense(1024, BF16), dense(384, F32),
            vt(), vt()]
    const = lambda b, m: (0, 0)
    return pl.pallas_call(
        _inproj_kernel,
        grid=(B, S // tm),
        in_specs=[pl.BlockSpec((None, tm, D), lambda b, m: (b, m, 0)),
                  pl.BlockSpec((1, D), const),
                  pl.BlockSpec((D, ncol), const),
                  pl.BlockSpec(wvt.shape, const),
                  pl.BlockSpec((1, ROPE_W), const),
                  pl.BlockSpec((MXU_N, MXU_N), const),
                  pl.BlockSpec((tm, LANES), lambda b, m: (m, 0)),
                  pl.BlockSpec((tm, LANES), lambda b, m: (m, 0))],
        out_specs=[o[1] for o in outs],
        out_shape=[o[0] for o in outs],
        compiler_params=pltpu.CompilerParams(dimension_semantics=("parallel", "arbitrary"),
                                             vmem_limit_bytes=VMEM_LIMIT),
        name="inproj",
    )(x, norm1_g.reshape(1, D), w_p, wvt, gain, bd, cos, sin)


def _gelu_tanh(x):
    return 0.5 * x * (1.0 + jnp.tanh(np.sqrt(2.0 / np.pi) * (x + 0.044715 * (x * x * x))))


def _compress_kernel(x0_ref, x1_ref, x2_ref, pk_ref, pv_ref, w1k_ref, w2k_ref, w1v_ref, w2vt_ref,
                     gain_ref, cos_ref, sin_ref, kc_ref, vct_ref):
    nc = x0_ref.shape[0] // CMP_STRIDE
    half = CMP_STRIDE * HEAD_DIM
    per_blk = LANES // HEAD_DIM
    xs = [[x_ref[pl.ds(l, nc, stride=CMP_STRIDE), :] for l in range(CMP_STRIDE)] for x_ref in (x0_ref, x1_ref, x2_ref)]

    def hidden(head, pos_ref, w1_ref):
        blk, sub = divmod(head, per_blk)
        cols = slice(sub * HEAD_DIM, (sub + 1) * HEAD_DIM)
        x = [xs[blk][l][:, cols] for l in range(CMP_STRIDE)]
        xa = jnp.concatenate([x[l] + pos_ref[l:l + 1, :] for l in range(CMP_STRIDE)], axis=1)
        xb = jnp.concatenate([x[l] + pos_ref[CMP_STRIDE + l:CMP_STRIDE + l + 1, :] for l in range(CMP_STRIDE)], axis=1)
        a = _dot(xa.astype(BF16), w1_ref[0:half, :])
        b = _dot(xb.astype(BF16), w1_ref[half:2 * half, :])
        pre = a + pltpu.roll(b, nc - 1, 0)
        return _gelu_tanh(pre).astype(BF16)

    for g in range(NSA_GROUPS):
        kc = _dot(hidden(g, pk_ref, w1k_ref), w2k_ref[...])
        vct = _dot_nt(w2vt_ref[...], hidden(NSA_GROUPS + g, pv_ref, w1v_ref))
        ms = jnp.mean(kc * kc, axis=-1, keepdims=True)
        kn = kc * lax.rsqrt(ms + NORM_EPS) * gain_ref[...]
        rot = jnp.concatenate([kn[:, HALF:], kn[:, :HALF]], axis=-1)
        kc_ref[g] = (kn * cos_ref[...] + rot * sin_ref[...]).astype(kc_ref.dtype)
        vct_ref[g] = vct.astype(vct_ref.dtype)


def _compress(kvc_raw, k_norm_cmp, kpos, kw1, kw2, vpos, vw1, vw2):
    B, S, w = kvc_raw.shape
    G, dh = NSA_GROUPS, HEAD_DIM
    assert w == 2 * G * dh == 3 * LANES
    nc = S // CMP_STRIDE
    cos, sin = _rope_tables(np.arange(nc) * CMP_STRIDE + CMP_BLOCK - 1, dh)
    const = lambda b: (0, 0)
    xblk = lambda j: pl.BlockSpec((None, S, LANES), lambda b: (b, 0, j))
    hid = kw1.shape[1]
    return pl.pallas_call(
        _compress_kernel,
        grid=(B,),
        in_specs=[xblk(0), xblk(1), xblk(2),
                  pl.BlockSpec((CMP_BLOCK, dh), const), pl.BlockSpec((CMP_BLOCK, dh), const),
                  pl.BlockSpec((CMP_BLOCK * dh, hid), const), pl.BlockSpec((hid, dh), const),
                  pl.BlockSpec((CMP_BLOCK * dh, hid), const), pl.BlockSpec((dh, hid), const),
                  pl.BlockSpec((1, dh), const), pl.BlockSpec((nc, dh), const), pl.BlockSpec((nc, dh), const)],
        out_specs=[pl.BlockSpec((None, G, nc, dh), lambda b: (b, 0, 0, 0)),
                   pl.BlockSpec((None, G, dh, nc), lambda b: (b, 0, 0, 0))],
        out_shape=[jax.ShapeDtypeStruct((B, G, nc, dh), BF16), jax.ShapeDtypeStruct((B, G, dh, nc), BF16)],
        compiler_params=pltpu.CompilerParams(dimension_semantics=("parallel",), vmem_limit_bytes=VMEM_LIMIT),
        name="compress",
    )(kvc_raw, kvc_raw, kvc_raw, kpos, vpos, kw1.astype(BF16), kw2.astype(BF16), vw1.astype(BF16),
      vw2.T.astype(BF16), k_norm_cmp.reshape(1, dh).astype(F32), cos, sin)


def _nsa_kernel(q_ref, kc_ref, vct_ref, ks_ref, vst_ref, kw_ref, vwt_ref, gate_ref, ovt_ref,
                o_ref, score_ref, bias_ref, s_ref, *, tq, tk, n_cmp, n_slc, seq):
    R = NSA_HPG
    qt = pl.program_id(2)
    t0 = qt * tq
    t0a = pl.multiple_of(t0, tq)
    q2 = q_ref[...].reshape(R * tq, HEAD_DIM)
    ncp = kc_ref.shape[0]
    t_lane = t0 + lax.broadcasted_iota(jnp.int32, (1, tq), 1)
    bpt = tk // SLC_BLOCK
    last_tile = seq // tk - 1
    span = min(WIN_SIZE + tq, seq)
    start = pl.multiple_of(jnp.maximum(t0 - WIN_SIZE, 0), tq)

    def heads(a):
        return [a[:, r * tq:(r + 1) * tq] for r in range(R)]

    def qk(kt, slot):
        k0 = pl.multiple_of(jnp.minimum(kt, last_tile) * tk, tk)
        s_ref[slot] = _dot_nt(ks_ref[pl.ds(k0, tk), :], q2)

    s_c = _dot_nt(kc_ref[...], q2)
    s_d = _dot_nt(ks_ref[pl.ds(t0a, tq), :], q2)
    s_w = _dot_nt(kw_ref[pl.ds(start, span), :], q2)
    qk(0, 0)

    c_idx = lax.broadcasted_iota(jnp.int32, (ncp, tq), 0)
    cmask = ((c_idx * CMP_STRIDE + (CMP_BLOCK - 1)) <= t_lane) & (c_idx < n_cmp)
    ps = []
    for sr in heads(s_c):
        sr = jnp.where(cmask, sr, NEG)
        m = jnp.max(sr, axis=0, keepdims=True)
        e = jnp.where(cmask, jnp.exp2(sr - m), 0.0)
        den = jnp.sum(e, axis=0, keepdims=True)
        ps.append(e / jnp.where(den > 0.0, den, 1.0))
    o_cmp = _dot(vct_ref[...], jnp.concatenate(ps, axis=1).astype(BF16))

    psum = ps[0]
    for r in range(1, R):
        psum = psum + ps[r]
    p_hi = psum.astype(BF16)
    p_lo = (psum - p_hi.astype(F32)).astype(BF16)
    imp = _dot(ovt_ref[...], p_hi) + _dot(ovt_ref[...], p_lo)
    n_idx = lax.broadcasted_iota(jnp.int32, (LANES, tq), 0)
    cur = (t0 + lax.broadcasted_iota(jnp.int32, (LANES, tq), 1)) // SLC_BLOCK
    forced = (n_idx == 0) | (n_idx == cur) | (n_idx == cur - 1)
    score = jnp.where(forced, jnp.inf, jnp.where(n_idx <= cur, imp, -jnp.inf))
    score_ref[...] = score

    tri = lax.broadcasted_iota(jnp.int32, (tq, tq), 0) <= lax.broadcasted_iota(jnp.int32, (tq, tq), 1)
    m_d, p_d = [], []
    for sr in heads(s_d):
        sr = jnp.where(tri, sr, NEG)
        m = jnp.max(sr, axis=0, keepdims=True)
        m_d.append(m)
        p_d.append(jnp.exp2(sr - m).astype(BF16))
    carry = (jnp.concatenate(m_d, axis=1), _dot(vst_ref[:, pl.ds(t0a, tq)], jnp.concatenate(p_d, axis=1)))

    diff = t_lane - (start + lax.broadcasted_iota(jnp.int32, (span, tq), 0))
    wmask = (diff >= 0) & (diff < WIN_SIZE)
    pw = []
    for sr in heads(s_w):
        sr = jnp.where(wmask, sr, NEG)
        m = jnp.max(sr, axis=0, keepdims=True)
        pw.append(jnp.exp2(sr - m).astype(BF16))
    acc_w = _dot(vwt_ref[:, pl.ds(start, span)], jnp.concatenate(pw, axis=1))
    o_win = acc_w[:HEAD_DIM] / acc_w[HEAD_DIM:HEAD_DIM + 1]

    nb = -(-n_slc // 8) * 8
    unroll = 4
    score_b, n_b = score[:nb], n_idx[:nb]

    def rank_body(i, cnt):
        for u in range(unroll):
            mi = i * unroll + u
            row = score_ref[pl.ds(mi, 1), :]
            beats = (row > score_b) | ((row == score_b) & (mi < n_b))
            cnt = cnt + jnp.where(beats, 1.0, 0.0)
        return cnt

    n_live = jnp.minimum((t0 + tq - 1) // SLC_BLOCK + 1, n_slc)
    cnt = lax.fori_loop(0, (n_live + unroll - 1) // unroll, rank_body, jnp.zeros((nb, tq), F32))
    selected = (cnt < float(min(SLC_TOPK, n_slc))) & (score_b > -jnp.inf) & (n_b < t0 // SLC_BLOCK)
    bias_ref[:nb] = jnp.where(selected, 0.0, NEG)
    if nb < LANES:
        bias_ref[nb:] = jnp.full((LANES - nb, tq), NEG, F32)

    def update(kt, slot, carry):
        m_old, acc = carry
        k0 = pl.multiple_of(kt * tk, tk)
        brows = [bias_ref[pl.ds(kt * bpt + j, 1), :] for j in range(bpt)]
        m_new, p_all, alphas = [], [], []
        for r in range(R):
            cols = slice(r * tq, (r + 1) * tq)
            mo = m_old[:, cols]
            m8 = None
            for j in range(bpt):
                blk = s_ref[slot, j * SLC_BLOCK:(j + 1) * SLC_BLOCK, cols]
                b8 = jnp.max(blk.reshape(SLC_BLOCK // 8, 8, tq), axis=0) + brows[j]
                m8 = b8 if m8 is None else jnp.maximum(m8, b8)
            mn = jnp.maximum(mo, jnp.max(m8, axis=0, keepdims=True))
            p_all.append(jnp.concatenate(
                [jnp.exp2(s_ref[slot, j * SLC_BLOCK:(j + 1) * SLC_BLOCK, cols] + (brows[j] - mn)).astype(BF16)
                 for j in range(bpt)], axis=0))
            alphas.append(jnp.exp2(mo - mn))
            m_new.append(mn)
        pv = _dot(vst_ref[:, pl.ds(k0, tk)], jnp.concatenate(p_all, axis=1))
        return jnp.concatenate(m_new, axis=1), jnp.concatenate(alphas, axis=1) * acc + pv

    def pair_body(ii, carry):
        a = 2 * ii
        qk(a + 1, 1)
        carry = update(a, 0, carry)
        qk(a + 2, 0)
        return update(a + 1, 1, carry)

    n_main = (t0 + tk - 1) // tk
    _, acc = lax.fori_loop(0, (n_main + 1) // 2, pair_body, carry)
    o_slc = acc[:HEAD_DIM] / acc[HEAD_DIM:HEAD_DIM + 1]

    gate_t = gate_ref[...].T
    ys = []
    for r in range(R):
        cols = slice(r * tq, (r + 1) * tq)
        ys.append(gate_t[3 * r:3 * r + 1] * o_cmp[:, cols] + gate_t[3 * r + 1:3 * r + 2] * o_slc[:, cols]
                  + gate_t[3 * r + 2:3 * r + 3] * o_win[:, cols])
    o_ref[...] = jnp.concatenate(ys, axis=0).T.astype(o_ref.dtype)


def _overlap_t(ncp, n_cmp, n_slc):
    cs = np.arange(ncp)[None, :] * CMP_STRIDE
    ss = np.arange(LANES)[:, None] * SLC_BLOCK
    ov = np.clip(np.minimum(cs + CMP_BLOCK, ss + SLC_BLOCK) - np.maximum(cs, ss), 0, None) / CMP_BLOCK
    ov = ov * (np.arange(ncp)[None, :] < n_cmp) * (np.arange(LANES)[:, None] < n_slc)
    return jnp.asarray(ov, BF16)


def _nsa(qa, kc, vct, ks, vst, kw, vwt, gns, tq, tk):
    B, H, S, dh = qa.shape
    ncp = kc.shape[2]
    n_cmp = (S - CMP_BLOCK) // CMP_STRIDE + 1
    n_slc = S // SLC_BLOCK
    assert n_slc <= LANES and S % (2 * tk) == 0 and tk % SLC_BLOCK == 0 and tq == 2 * SLC_BLOCK
    ovt = _overlap_t(ncp, n_cmp, n_slc)
    k_c = pl.BlockSpec((None, None, ncp, dh), lambda b, g, t: (b, g, 0, 0))
    v_c = pl.BlockSpec((None, None, dh, ncp), lambda b, g, t: (b, g, 0, 0))
    k_s = pl.BlockSpec((None, None, S, dh), lambda b, g, t: (b, g, 0, 0))
    v_s = pl.BlockSpec((None, None, vst.shape[2], S), lambda b, g, t: (b, g, 0, 0))
    v_w = pl.BlockSpec((None, None, vwt.shape[2], S), lambda b, g, t: (b, g, 0, 0))
    const = lambda b, g, t: (0, 0)
    kern = functools.partial(_nsa_kernel, tq=tq, tk=tk, n_cmp=n_cmp, n_slc=n_slc, seq=S)
    return pl.pallas_call(
        kern,
        grid=(B, NSA_GROUPS, S // tq),
        in_specs=[pl.BlockSpec((None, NSA_HPG, tq, dh), lambda b, g, t: (b, g, t, 0)),
                  k_c, v_c, k_s, v_s, k_s, v_w,
                  pl.BlockSpec((None, tq, LANES), lambda b, g, t: (b, t, g)),
                  pl.BlockSpec((LANES, ncp), const)],
        out_specs=pl.BlockSpec((None, tq, NSA_HPG * dh), lambda b, g, t: (b, t, g)),
        out_shape=jax.ShapeDtypeStruct((B, S, H * dh), BF16),
        scratch_shapes=[pltpu.VMEM((LANES, tq), F32), pltpu.VMEM((LANES, tq), F32),
                        pltpu.VMEM((2, tk, NSA_HPG * tq), F32)],
        compiler_params=pltpu.CompilerParams(dimension_semantics=("parallel", "parallel", "arbitrary"),
                                             vmem_limit_bytes=VMEM_LIMIT),
        name="nsa",
    )(qa, kc, vct, ks, vst, kw, vwt, gns, ovt)


def _dilated_kernel(q_ref, kp_ref, kc_ref, vp_ref, vc_ref, o_ref, lse_ref, vt_ref, *, d, tq, nq):
    span = 2 * tq
    row = lax.broadcasted_iota(jnp.int32, (span, tq), 0)
    lane = lax.broadcasted_iota(jnp.int32, (span, tq), 1)
    diff = (lane + tq) - row
    band = (diff >= 0) & (diff <= DIL_SPAN)
    has_prev = pl.program_id(1) > 0
    first = band & ((row >= tq) | has_prev)
    for j in range(nq):
        for r in range(d):
            slot = j * d + r
            rows = pl.ds(r + j * tq * d, tq, stride=d)
            qb = q_ref[rows, :]
            if j == 0:
                kb = jnp.concatenate([kp_ref[pl.ds(r, tq, stride=d), :], kc_ref[rows, :]], axis=0)
                vb = jnp.concatenate([vp_ref[pl.ds(r, tq, stride=d), :], vc_ref[rows, :]], axis=0)
            else:
                kv_rows = pl.ds(r + (j - 1) * tq * d, span, stride=d)
                kb, vb = kc_ref[kv_rows, :], vc_ref[kv_rows, :]
            mask = first if j == 0 else band
            vt_ref[slot] = vb.T.astype(BF16)
            o_t, l_t = [], []
            for h in range(LANES // HEAD_DIM):
                cols = slice(h * HEAD_DIM, (h + 1) * HEAD_DIM)
                s = _dot_nt(kb[:, cols].astype(BF16), qb[:, cols].astype(BF16))
                s = jnp.where(mask, s, NEG)
                m = jnp.max(s, axis=0, keepdims=True)
                e = jnp.exp(s - m)
                den = jnp.sum(e, axis=0, keepdims=True)
                o_t.append(_dot(vt_ref[slot, cols, :], (e / den).astype(BF16)))
                l_t.append(jnp.broadcast_to(m + jnp.log(den), (HEAD_DIM, tq)))
            o_ref[rows, :] = jnp.concatenate(o_t, axis=0).T
            lse_ref[rows, :] = jnp.concatenate(l_t, axis=0).T


def _dilated(qb, kb, vb, g, block_tokens):
    B, S, _ = qb.shape
    d = DIL_PAIRS[g][1]
    tq = min(LANES, S // d)
    nq = max(block_tokens // (d * tq), 1)
    assert DIL_PAIRS[g][0] // d == DIL_SPAN and DIL_SPAN <= tq and S % (nq * d * tq) == 0
    gw = DIL_HPG * HEAD_DIM
    nb = gw // LANES
    cur = pl.BlockSpec((None, nq * d * tq, LANES), lambda b, i, c: (b, i, g * nb + c))
    prev = pl.BlockSpec((None, d * tq, LANES), lambda b, i, c: (b, jnp.maximum(i * nq - 1, 0), g * nb + c))
    out_blk = pl.BlockSpec((None, nq * d * tq, LANES), lambda b, i, c: (b, i, c))
    return pl.pallas_call(
        functools.partial(_dilated_kernel, d=d, tq=tq, nq=nq),
        grid=(B, S // (nq * d * tq), nb),
        in_specs=[cur, prev, cur, prev, cur],
        out_specs=[out_blk, out_blk],
        out_shape=[jax.ShapeDtypeStruct((B, S, gw), F32)] * 2,
        scratch_shapes=[pltpu.VMEM((nq * d, LANES, 2 * tq), BF16)],
        compiler_params=pltpu.CompilerParams(dimension_semantics=("parallel", "arbitrary", "arbitrary"),
                                             vmem_limit_bytes=VMEM_LIMIT),
        name=f"dilated{g}",
    )(qb, kb, kb, vb, vb)


def _mixout_kernel(x_ref, ya_ref, o0_ref, o1_ref, o2_ref, l0_ref, l1_ref, l2_ref, gma_ref, gmb_ref,
                   woa_ref, wob_ref, wout_ref, out_ref):
    l0, l1, l2 = l0_ref[...], l1_ref[...], l2_ref[...]
    mx = jnp.maximum(jnp.maximum(l0, l1), l2)
    e0, e1, e2 = jnp.exp(l0 - mx), jnp.exp(l1 - mx), jnp.exp(l2 - mx)
    yb = (e0 * o0_ref[...] + e1 * o1_ref[...] + e2 * o2_ref[...]) / (e0 + e1 + e2)
    ta = _dot(ya_ref[...], woa_ref[...])
    tb = _dot(yb.astype(BF16), wob_ref[...])
    mixed = gma_ref[...].astype(F32) * ta + gmb_ref[...].astype(F32) * tb
    out_ref[...] = x_ref[...] + _dot(mixed.astype(BF16), wout_ref[...])


def _mixout(x, ya, dil, gma, gmb, w_o_a, w_o_b, w_out, tm):
    B, S, D = x.shape
    row = lambda w: pl.BlockSpec((None, tm, w), lambda b, m: (b, m, 0))
    full = lambda a: pl.BlockSpec(a.shape, lambda b, m: (0, 0))
    gw = DIL_HPG * HEAD_DIM
    wa, wb, wo = w_o_a.astype(BF16), w_o_b.astype(BF16), w_out.astype(BF16)
    return pl.pallas_call(
        _mixout_kernel,
        grid=(B, S // tm),
        in_specs=[row(D), row(ya.shape[-1])] + [row(gw)] * 6 + [row(D), row(D), full(wa), full(wb), full(wo)],
        out_specs=row(D),
        out_shape=jax.ShapeDtypeStruct((B, S, D), F32),
        compiler_params=pltpu.CompilerParams(dimension_semantics=("parallel", "parallel"),
                                             vmem_limit_bytes=VMEM_LIMIT),
        name="mixout",
    )(x, ya, dil[0][0], dil[1][0], dil[2][0], dil[0][1], dil[1][1], dil[2][1], gma, gmb, wa, wb, wo)


def _mlp_kernel(x_ref, g_ref, wup_ref, wdn_ref, out_ref, *, fc):
    x = x_ref[...]
    ms = jnp.mean(x * x, axis=-1, keepdims=True)
    h = (x * lax.rsqrt(ms + NORM_EPS) * g_ref[...]).astype(BF16)
    acc = x
    for off, w in _chunks(wup_ref.shape[1], fc):
        u = jnp.maximum(_dot(h, wup_ref[:, off:off + w]), 0.0)
        acc = acc + _dot((u * u).astype(BF16), wdn_ref[off:off + w, :])
    out_ref[...] = acc


def _mlp(x, norm2_g, w_up, w_down, tm, fc):
    B, S, D = x.shape
    F = w_up.shape[1]
    row = pl.BlockSpec((None, tm, D), lambda b, m: (b, m, 0))
    const = lambda b, m: (0, 0)
    return pl.pallas_call(
        functools.partial(_mlp_kernel, fc=fc),
        grid=(B, S // tm),
        in_specs=[row, pl.BlockSpec((1, D), const), pl.BlockSpec((D, F), const), pl.BlockSpec((F, D), const)],
        out_specs=row,
        out_shape=jax.ShapeDtypeStruct((B, S, D), F32),
        compiler_params=pltpu.CompilerParams(dimension_semantics=("parallel", "parallel"),
                                             vmem_limit_bytes=VMEM_LIMIT),
        name="mlp",
    )(x, norm2_g.reshape(1, D), w_up.astype(BF16), w_down.astype(BF16))


def kernel(x, norm1_g, w_in, q_norm_a, k_norm_cmp, k_norm_slc, k_norm_win, cmp_k_pos, cmp_k_w1, cmp_k_w2,
           cmp_v_pos, cmp_v_w1, cmp_v_w2, q_norm_b, k_norm_b, w_o_a, w_o_b, w_out, norm2_g, w_up, w_down):
    depth = w_in.shape[0]
    for i in range(depth):
        (qa, ks, kw, qb, kb, kvc_raw, vb, gma, gmb, gns, vst, vwt) = _inproj(
            x, norm1_g[i], w_in[i], q_norm_a[i], k_norm_slc[i], k_norm_win[i], q_norm_b[i], k_norm_b[i], tm=512)
        kc, vct = _compress(kvc_raw, k_norm_cmp[i], cmp_k_pos[i], cmp_k_w1[i], cmp_k_w2[i],
                           cmp_v_pos[i], cmp_v_w1[i], cmp_v_w2[i])
        ya = _nsa(qa, kc, vct, ks, vst, kw, vwt, gns, tq=128, tk=512)
        dil = [_dilated(qb, kb, vb, g, block_tokens=1024) for g in range(DIL_GROUPS)]
        x = _mixout(x, ya, dil, gma, gmb, w_o_a[i], w_o_b[i], w_out[i], tm=256)
        x = _mlp(x, norm2_g[i], w_up[i], w_down[i], tm=512, fc=1024)
    return x
```

```python
import functools

import numpy as np
import jax
import jax.numpy as jnp
from jax import lax
from jax.experimental import pallas as pl
from jax.experimental.pallas import tpu as pltpu

HEAD_DIM = 64
HALF = HEAD_DIM // 2
ROPE_THETA = 10000.0
NORM_EPS = 1e-6
NEG = -1e30
NSA_HEADS = 12
NSA_GROUPS = 3
NSA_HPG = NSA_HEADS // NSA_GROUPS
CMP_BLOCK = 32
CMP_STRIDE = 16
SLC_BLOCK = 64
SLC_TOPK = 16
WIN_SIZE = 512
DIL_PAIRS = ((128, 1), (512, 4), (2048, 16))
DIL_GROUPS = len(DIL_PAIRS)
DIL_HPG = 4
DIL_HEADS = DIL_GROUPS * DIL_HPG
DIL_SPAN = 128
LOG2E = float(np.log2(np.e))
ONES_ROWS = 16

LANES = 128
MXU_N = 256
VMEM_LIMIT = 56 * 1024 * 1024

BF16 = jnp.bfloat16
F32 = jnp.float32

ROPE_SEGS = (("qa", 768), ("qb", 768), ("kb", 768), ("ks", 192), ("kw", 192))
PLAIN_SEGS = (("kvc", 384), ("vb", 768))
SIG_SEGS = (("gma", 1024), ("gmb", 1024), ("gns", 384))
ROPE_W = sum(w for _, w in ROPE_SEGS)
PLAIN_W = sum(w for _, w in PLAIN_SEGS)
SIG_W = sum(w for _, w in SIG_SEGS)
HEAD_MAJOR = ("qa", "ks", "kw")


def _dot(a, b):
    return jnp.dot(a, b, preferred_element_type=F32)


def _dot_nt(a, b):
    return lax.dot_general(a, b, (((1,), (1,)), ((), ())), preferred_element_type=F32)


def _chunks(total, width):
    out, off = [], 0
    while off < total:
        w = min(width, total - off)
        out.append((off, w))
        off += w
    return out


def _seg_lookup(segs, col):
    off = 0
    for name, w in segs:
        if col < off + w:
            return name, col - off
        off += w
    raise ValueError(col)


def _inproj_kernel(x_ref, g1_ref, w_ref, wvt_ref, gain_ref, bd_ref, cos_ref, sin_ref,
                   qa_ref, ks_ref, kw_ref, qb_ref, kb_ref,
                   kvc_ref, vb_ref, gma_ref, gmb_ref, gns_ref, vst_ref, vwt_ref):
    outs = dict(qa=qa_ref, ks=ks_ref, kw=kw_ref, qb=qb_ref, kb=kb_ref, kvc=kvc_ref,
                vb=vb_ref, gma=gma_ref, gmb=gmb_ref, gns=gns_ref)

    def emit(segs, col, val):
        name, rel = _seg_lookup(segs, col)
        ref = outs[name]
        if name in HEAD_MAJOR:
            for p in range(LANES // HEAD_DIM):
                nm, r = _seg_lookup(segs, col + p * HEAD_DIM)
                outs[nm][r // HEAD_DIM] = val[:, p * HEAD_DIM:(p + 1) * HEAD_DIM].astype(outs[nm].dtype)
        else:
            nm2, _ = _seg_lookup(segs, col + HEAD_DIM)
            if nm2 != name:
                raise ValueError("unaligned dense segment")
            ref[:, rel:rel + LANES] = val.astype(ref.dtype)

    x = x_ref[...]
    ms = jnp.mean(x * x, axis=-1, keepdims=True)
    h = (x * lax.rsqrt(ms + NORM_EPS) * g1_ref[...]).astype(BF16)

    lane = lax.broadcasted_iota(jnp.int32, (x.shape[0], LANES), 1)
    first_half = (lane % HEAD_DIM) < HALF
    cos = cos_ref[...]
    sin = sin_ref[...]
    bd = bd_ref[...]

    def rope_epilogue(off, w, y):
        msq = _dot((y * y).astype(BF16), bd[:w, :w])
        yn = y * lax.rsqrt(msq + NORM_EPS) * gain_ref[:, off:off + w]
        for u in range(w // LANES):
            v = yn[:, u * LANES:(u + 1) * LANES]
            rot = jnp.where(first_half, pltpu.roll(v, LANES - HALF, 1), pltpu.roll(v, HALF, 1))
            emit(ROPE_SEGS, off + u * LANES, v * cos + rot * sin)

    def plain_epilogue(off, w, y):
        for u in range(w // LANES):
            emit(PLAIN_SEGS, off + u * LANES, y[:, u * LANES:(u + 1) * LANES])

    def sig_epilogue(off, w, y):
        y = jax.nn.sigmoid(y)
        for u in range(w // LANES):
            emit(SIG_SEGS, off + u * LANES, y[:, u * LANES:(u + 1) * LANES])

    work = ([(0, off, w, rope_epilogue) for off, w in _chunks(ROPE_W, MXU_N)]
            + [(ROPE_W, off, w, plain_epilogue) for off, w in _chunks(PLAIN_W, MXU_N)]
            + [(ROPE_W + PLAIN_W, off, w, sig_epilogue) for off, w in _chunks(SIG_W, MXU_N)])
    pending = None
    for base, off, w, epilogue in work:
        y = _dot(h, w_ref[:, base + off:base + off + w])
        if pending is not None:
            pending[0](pending[1], pending[2], pending[3])
        pending = (epilogue, off, w, y)
    pending[0](pending[1], pending[2], pending[3])

    v_t = _dot_nt(wvt_ref[...], h)
    ones = jnp.ones((ONES_ROWS, x.shape[0]), vst_ref.dtype)
    for i, ref in enumerate((vst_ref, vwt_ref)):
        for g in range(NSA_GROUPS):
            r0 = (i * NSA_GROUPS + g) * HEAD_DIM
            ref[g, 0:HEAD_DIM, :] = v_t[r0:r0 + HEAD_DIM].astype(ref.dtype)
            ref[g, HEAD_DIM:HEAD_DIM + ONES_ROWS, :] = ones


def _rope_tables(positions, width):
    inv_freq = np.power(ROPE_THETA, -np.arange(HALF, dtype=np.float64) / HALF)
    ang = np.asarray(positions, np.float64)[:, None] * inv_freq[None, :]
    reps = width // HEAD_DIM
    cos = np.tile(np.concatenate([np.cos(ang), np.cos(ang)], axis=1), (1, reps))
    sin = np.tile(np.concatenate([-np.sin(ang), np.sin(ang)], axis=1), (1, reps))
    return jnp.asarray(cos, F32), jnp.asarray(sin, F32)


def _inproj(x, norm1_g, w_in, q_norm_a, k_norm_slc, k_norm_win, q_norm_b, k_norm_b, tm):
    B, S, D = x.shape
    scale = HEAD_DIM ** -0.5
    sp = np.cumsum((0, 768, 192, 192, 192, 192, 192, 192, 36, 768, 768, 768, 1024, 1024))
    names = ("qa", "kc", "vc", "ks", "vs", "kw", "vw", "gns", "qb", "kb", "vb", "gma", "gmb")
    w_bf = w_in.astype(BF16)
    col = {n: w_bf[:, int(sp[i]):int(sp[i + 1])] for i, n in enumerate(names)}
    gpg = NSA_HPG * 3
    gns = jnp.concatenate(
        [jnp.pad(col["gns"][:, g * gpg:(g + 1) * gpg], ((0, 0), (0, LANES - gpg))) for g in range(NSA_GROUPS)],
        axis=1)
    col["gns"] = gns
    col["kvc"] = jnp.concatenate([col["kc"], col["vc"]], axis=1)
    w_p = jnp.concatenate([col[n] for n, _ in ROPE_SEGS + PLAIN_SEGS + SIG_SEGS], axis=1)
    wvt = jnp.concatenate([col["vs"], col["vw"]], axis=1).T
    ncol = ROPE_W + PLAIN_W + SIG_W
    assert w_p.shape == (D, ncol)

    gain = jnp.concatenate([
        jnp.tile(q_norm_a * (scale * LOG2E), NSA_HEADS), jnp.tile(q_norm_b * (scale * LOG2E), DIL_HEADS),
        jnp.tile(k_norm_b, DIL_HEADS), jnp.tile(k_norm_slc, NSA_GROUPS), jnp.tile(k_norm_win, NSA_GROUPS)
    ]).reshape(1, ROPE_W).astype(F32)
    bd = jnp.asarray(np.kron(np.eye(MXU_N // HEAD_DIM), np.full((HEAD_DIM, HEAD_DIM), 1.0 / HEAD_DIM)), BF16)
    cos, sin = _rope_tables(np.arange(S), LANES)

    def hm(nh, dt):
        return (jax.ShapeDtypeStruct((B, nh, S, HEAD_DIM), dt),
                pl.BlockSpec((None, nh, tm, HEAD_DIM), lambda b, m: (b, 0, m, 0)))

    def vt():
        rows = HEAD_DIM + ONES_ROWS
        return (jax.ShapeDtypeStruct((B, NSA_GROUPS, rows, S), BF16),
                pl.BlockSpec((None, NSA_GROUPS, rows, tm), lambda b, m: (b, 0, 0, m)))

    def dense(w, dt):
        return (jax.ShapeDtypeStruct((B, S, w), dt), pl.BlockSpec((None, tm, w), lambda b, m: (b, m, 0)))

    outs = [hm(NSA_HEADS, BF16), hm(NSA_GROUPS, BF16), hm(NSA_GROUPS, BF16),
            dense(768, F32), dense(768, F32),
            dense(384, F32),
            dense(768, F32),
            dense(1024, BF16), dense(1024, BF16), dense(384, F32),
            vt(), vt()]
    const = lambda b, m: (0, 0)
    return pl.pallas_call(
        _inproj_kernel,
        grid=(B, S // tm),
        in_specs=[pl.BlockSpec((None, tm, D), lambda b, m: (b, m, 0)),
                  pl.BlockSpec((1, D), const),
                  pl.BlockSpec((D, ncol), const),
                  pl.BlockSpec(wvt.shape, const),
                  pl.BlockSpec((1, ROPE_W), const),
                  pl.BlockSpec((MXU_N, MXU_N), const),
                  pl.BlockSpec((tm, LANES), lambda b, m: (m, 0)),
                  pl.BlockSpec((tm, LANES), lambda b, m: (m, 0))],
        out_specs=[o[1] for o in outs],
        out_shape=[o[0] for o in outs],
        compiler_params=pltpu.CompilerParams(dimension_semantics=("parallel", "arbitrary"),
                                             vmem_limit_bytes=VMEM_LIMIT),
        name="inproj",
    )(x, norm1_g.reshape(1, D), w_p, wvt, gain, bd, cos, sin)


def _gelu_tanh(x):
    return 0.5 * x * (1.0 + jnp.tanh(np.sqrt(2.0 / np.pi) * (x + 0.044715 * (x * x * x))))


def _compress_kernel(x0_ref, x1_ref, x2_ref, pk_ref, pv_ref, w1k_ref, w2k_ref, w1v_ref, w2vt_ref,
                     gain_ref, cos_ref, sin_ref, kc_ref, vct_ref):
    nc = x0_ref.shape[0] // CMP_STRIDE
    half = CMP_STRIDE * HEAD_DIM
    per_blk = LANES // HEAD_DIM
    xs = [[x_ref[pl.ds(l, nc, stride=CMP_STRIDE), :] for l in range(CMP_STRIDE)] for x_ref in (x0_ref, x1_ref, x2_ref)]

    def hidden(head, pos_ref, w1_ref):
        blk, sub = divmod(head, per_blk)
        cols = slice(sub * HEAD_DIM, (sub + 1) * HEAD_DIM)
        x = [xs[blk][l][:, cols] for l in range(CMP_STRIDE)]
        xa = jnp.concatenate([x[l] + pos_ref[l:l + 1, :] for l in range(CMP_STRIDE)], axis=1)
        xb = jnp.concatenate([x[l] + pos_ref[CMP_STRIDE + l:CMP_STRIDE + l + 1, :] for l in range(CMP_STRIDE)], axis=1)
        a = _dot(xa.astype(BF16), w1_ref[0:half, :])
        b = _dot(xb.astype(BF16), w1_ref[half:2 * half, :])
        pre = a + pltpu.roll(b, nc - 1, 0)
        return _gelu_tanh(pre).astype(BF16)

    for g in range(NSA_GROUPS):
        kc = _dot(hidden(g, pk_ref, w1k_ref), w2k_ref[...])
        vct = _dot_nt(w2vt_ref[...], hidden(NSA_GROUPS + g, pv_ref, w1v_ref))
        ms = jnp.mean(kc * kc, axis=-1, keepdims=True)
        kn = kc * lax.rsqrt(ms + NORM_EPS) * gain_ref[...]
        rot = jnp.concatenate([kn[:, HALF:], kn[:, :HALF]], axis=-1)
        kc_ref[g] = (kn * cos_ref[...] + rot * sin_ref[...]).astype(kc_ref.dtype)
        vct_ref[g] = vct.astype(vct_ref.dtype)


def _compress(kvc_raw, k_norm_cmp, kpos, kw1, kw2, vpos, vw1, vw2):
    B, S, w = kvc_raw.shape
    G, dh = NSA_GROUPS, HEAD_DIM
    assert w == 2 * G * dh == 3 * LANES
    nc = S // CMP_STRIDE
    cos, sin = _rope_tables(np.arange(nc) * CMP_STRIDE + CMP_BLOCK - 1, dh)
    const = lambda b: (0, 0)
    xblk = lambda j: pl.BlockSpec((None, S, LANES), lambda b: (b, 0, j))
    hid = kw1.shape[1]
    return pl.pallas_call(
        _compress_kernel,
        grid=(B,),
        in_specs=[xblk(0), xblk(1), xblk(2),
                  pl.BlockSpec((CMP_BLOCK, dh), const), pl.BlockSpec((CMP_BLOCK, dh), const),
                  pl.BlockSpec((CMP_BLOCK * dh, hid), const), pl.BlockSpec((hid, dh), const),
                  pl.BlockSpec((CMP_BLOCK * dh, hid), const), pl.BlockSpec((dh, hid), const),
                  pl.BlockSpec((1, dh), const), pl.BlockSpec((nc, dh), const), pl.BlockSpec((nc, dh), const)],
        out_specs=[pl.BlockSpec((None, G, nc, dh), lambda b: (b, 0, 0, 0)),
                   pl.BlockSpec((None, G, dh, nc), lambda b: (b, 0, 0, 0))],
        out_shape=[jax.ShapeDtypeStruct((B, G, nc, dh), BF16), jax.ShapeDtypeStruct((B, G, dh, nc), BF16)],
        compiler_params=pltpu.CompilerParams(dimension_semantics=("parallel",), vmem_limit_bytes=VMEM_LIMIT),
        name="compress",
    )(kvc_raw, kvc_raw, kvc_raw, kpos, vpos, kw1.astype(BF16), kw2.astype(BF16), vw1.astype(BF16),
      vw2.T.astype(BF16), k_norm_cmp.reshape(1, dh).astype(F32), cos, sin)


def _nsa_kernel(q_ref, kc_ref, vct_ref, ks_ref, vst_ref, kw_ref, vwt_ref, gate_ref, ovt_ref,
                o_ref, score_ref, bias_ref, s_ref, *, tq, tk, n_cmp, n_slc, seq):
    R = NSA_HPG
    qt = pl.program_id(2)
    t0 = qt * tq
    t0a = pl.multiple_of(t0, tq)
    q2 = q_ref[...].reshape(R * tq, HEAD_DIM)
    ncp = kc_ref.shape[0]
    t_lane = t0 + lax.broadcasted_iota(jnp.int32, (1, tq), 1)
    bpt = tk // SLC_BLOCK
    last_tile = seq // tk - 1
    span = min(WIN_SIZE + tq, seq)
    start = pl.multiple_of(jnp.maximum(t0 - WIN_SIZE, 0), tq)

    def heads(a):
        return [a[:, r * tq:(r + 1) * tq] for r in range(R)]

    def qk(kt, slot):
        k0 = pl.multiple_of(jnp.minimum(kt, last_tile) * tk, tk)
        s_ref[slot] = _dot_nt(ks_ref[pl.ds(k0, tk), :], q2)

    s_c = _dot_nt(kc_ref[...], q2)
    s_d = _dot_nt(ks_ref[pl.ds(t0a, tq), :], q2)
    s_w = _dot_nt(kw_ref[pl.ds(start, span), :], q2)
    qk(0, 0)

    c_idx = lax.broadcasted_iota(jnp.int32, (ncp, tq), 0)
    cmask = ((c_idx * CMP_STRIDE + (CMP_BLOCK - 1)) <= t_lane) & (c_idx < n_cmp)
    ps = []
    for sr in heads(s_c):
        sr = jnp.where(cmask, sr, NEG)
        m = jnp.max(sr, axis=0, keepdims=True)
        e = jnp.where(cmask, jnp.exp2(sr - m), 0.0)
        den = jnp.sum(e, axis=0, keepdims=True)
        ps.append(e / jnp.where(den > 0.0, den, 1.0))
    o_cmp = _dot(vct_ref[...], jnp.concatenate(ps, axis=1).astype(BF16))

    psum = ps[0]
    for r in range(1, R):
        psum = psum + ps[r]
    p_hi = psum.astype(BF16)
    p_lo = (psum - p_hi.astype(F32)).astype(BF16)
    imp = _dot(ovt_ref[...], p_hi) + _dot(ovt_ref[...], p_lo)
    n_idx = lax.broadcasted_iota(jnp.int32, (LANES, tq), 0)
    cur = (t0 + lax.broadcasted_iota(jnp.int32, (LANES, tq), 1)) // SLC_BLOCK
    forced = (n_idx == 0) | (n_idx == cur) | (n_idx == cur - 1)
    score = jnp.where(forced, jnp.inf, jnp.where(n_idx <= cur, imp, -jnp.inf))
    score_ref[...] = score

    tri = lax.broadcasted_iota(jnp.int32, (tq, tq), 0) <= lax.broadcasted_iota(jnp.int32, (tq, tq), 1)
    m_d, p_d = [], []
    for sr in heads(s_d):
        sr = jnp.where(tri, sr, NEG)
        m = jnp.max(sr, axis=0, keepdims=True)
        m_d.append(m)
        p_d.append(jnp.exp2(sr - m).astype(BF16))
    carry = (jnp.concatenate(m_d, axis=1), _dot(vst_ref[:, pl.ds(t0a, tq)], jnp.concatenate(p_d, axis=1)))

    diff = t_lane - (start + lax.broadcasted_iota(jnp.int32, (span, tq), 0))
    wmask = (diff >= 0) & (diff < WIN_SIZE)
    pw = []
    for sr in heads(s_w):
        sr = jnp.where(wmask, sr, NEG)
        m = jnp.max(sr, axis=0, keepdims=True)
        pw.append(jnp.exp2(sr - m).astype(BF16))
    acc_w = _dot(vwt_ref[:, pl.ds(start, span)], jnp.concatenate(pw, axis=1))
    o_win = acc_w[:HEAD_DIM] / acc_w[HEAD_DIM:HEAD_DIM + 1]

    nb = -(-n_slc // 8) * 8
    unroll = 4
    score_b, n_b = score[:nb], n_idx[:nb]

    def rank_body(i, cnt):
        for u in range(unroll):
            mi = i * unroll + u
            row = score_ref[pl.ds(mi, 1), :]
            beats = (row > score_b) | ((row == score_b) & (mi < n_b))
            cnt = cnt + jnp.where(beats, 1.0, 0.0)
        return cnt

    n_live = jnp.minimum((t0 + tq - 1) // SLC_BLOCK + 1, n_slc)
    cnt = lax.fori_loop(0, (n_live + unroll - 1) // unroll, rank_body, jnp.zeros((nb, tq), F32))
    selected = (cnt < float(min(SLC_TOPK, n_slc))) & (score_b > -jnp.inf) & (n_b < t0 // SLC_BLOCK)
    bias_ref[:nb] = jnp.where(selected, 0.0, NEG)
    if nb < LANES:
        bias_ref[nb:] = jnp.full((LANES - nb, tq), NEG, F32)

    def update(kt, slot, carry):
        m_old, acc = carry
        k0 = pl.multiple_of(kt * tk, tk)
        brows = [bias_ref[pl.ds(kt * bpt + j, 1), :] for j in range(bpt)]
        m_new, p_all, alphas = [], [], []
        for r in range(R):
            cols = slice(r * tq, (r + 1) * tq)
            mo = m_old[:, cols]
            m8 = None
            for j in range(bpt):
                blk = s_ref[slot, j * SLC_BLOCK:(j + 1) * SLC_BLOCK, cols]
                b8 = jnp.max(blk.reshape(SLC_BLOCK // 8, 8, tq), axis=0) + brows[j]
                m8 = b8 if m8 is None else jnp.maximum(m8, b8)
            mn = jnp.maximum(mo, jnp.max(m8, axis=0, keepdims=True))
            p_all.append(jnp.concatenate(
                [jnp.exp2(s_ref[slot, j * SLC_BLOCK:(j + 1) * SLC_BLOCK, cols] + (brows[j] - mn)).astype(BF16)
                 for j in range(bpt)], axis=0))
            alphas.append(jnp.exp2(mo - mn))
            m_new.append(mn)
        pv = _dot(vst_ref[:, pl.ds(k0, tk)], jnp.concatenate(p_all, axis=1))
        return jnp.concatenate(m_new, axis=1), jnp.concatenate(alphas, axis=1) * acc + pv

    def pair_body(ii, carry):
        a = 2 * ii
        qk(a + 1, 1)
        carry = update(a, 0, carry)
        qk(a + 2, 0)
        return update(a + 1, 1, carry)

    n_main = (t0 + tk - 1) // tk
    _, acc = lax.fori_loop(0, (n_main + 1) // 2, pair_body, carry)
    o_slc = acc[:HEAD_DIM] / acc[HEAD_DIM:HEAD_DIM + 1]

    gate_t = gate_ref[...].T
    ys = []
    for r in range(R):
        cols = slice(r * tq, (r + 1) * tq)
        ys.append(gate_t[3 * r:3 * r + 1] * o_cmp[:, cols] + gate_t[3 * r + 1:3 * r + 2] * o_slc[:, cols]
                  + gate_t[3 * r + 2:3 * r + 3] * o_win[:, cols])
    o_ref[...] = jnp.concatenate(ys, axis=0).T.astype(o_ref.dtype)


def _overlap_t(ncp, n_cmp, n_slc):
    cs = np.arange(ncp)[None, :] * CMP_STRIDE
    ss = np.arange(LANES)[:, None] * SLC_BLOCK
    ov = np.clip(np.minimum(cs + CMP_BLOCK, ss + SLC_BLOCK) - np.maximum(cs, ss), 0, None) / CMP_BLOCK
    ov = ov * (np.arange(ncp)[None, :] < n_cmp) * (np.arange(LANES)[:, None] < n_slc)
    return jnp.asarray(ov, BF16)


def _nsa(qa, kc, vct, ks, vst, kw, vwt, gns, tq, tk):
    B, H, S, dh = qa.shape
    ncp = kc.shape[2]
    n_cmp = (S - CMP_BLOCK) // CMP_STRIDE + 1
    n_slc = S // SLC_BLOCK
    assert n_slc <= LANES and S % (2 * tk) == 0 and tk % SLC_BLOCK == 0 and tq == 2 * SLC_BLOCK
    ovt = _overlap_t(ncp, n_cmp, n_slc)
    k_c = pl.BlockSpec((None, None, ncp, dh), lambda b, g, t: (b, g, 0, 0))
    v_c = pl.BlockSpec((None, None, dh, ncp), lambda b, g, t: (b, g, 0, 0))
    k_s = pl.BlockSpec((None, None, S, dh), lambda b, g, t: (b, g, 0, 0))
    v_s = pl.BlockSpec((None, None, vst.shape[2], S), lambda b, g, t: (b, g, 0, 0))
    v_w = pl.BlockSpec((None, None, vwt.shape[2], S), lambda b, g, t: (b, g, 0, 0))
    const = lambda b, g, t: (0, 0)
    kern = functools.partial(_nsa_kernel, tq=tq, tk=tk, n_cmp=n_cmp, n_slc=n_slc, seq=S)
    return pl.pallas_call(
        kern,
        grid=(B, NSA_GROUPS, S // tq),
        in_specs=[pl.BlockSpec((None, NSA_HPG, tq, dh), lambda b, g, t: (b, g, t, 0)),
                  k_c, v_c, k_s, v_s, k_s, v_w,
                  pl.BlockSpec((None, tq, LANES), lambda b, g, t: (b, t, g)),
                  pl.BlockSpec((LANES, ncp), const)],
        out_specs=pl.BlockSpec((None, tq, NSA_HPG * dh), lambda b, g, t: (b, t, g)),
        out_shape=jax.ShapeDtypeStruct((B, S, H * dh), BF16),
        scratch_shapes=[pltpu.VMEM((LANES, tq), F32), pltpu.VMEM((LANES, tq), F32),
                        pltpu.VMEM((2, tk, NSA_HPG * tq), F32)],
        compiler_params=pltpu.CompilerParams(dimension_semantics=("parallel", "parallel", "arbitrary"),
                                             vmem_limit_bytes=VMEM_LIMIT),
        name="nsa",
    )(qa, kc, vct, ks, vst, kw, vwt, gns, ovt)


def _dilated_kernel(q_ref, kp_ref, kc_ref, vp_ref, vc_ref, o_ref, lse_ref, vt_ref, *, d, tq, nq):
    span = 2 * tq
    nh = LANES // HEAD_DIM
    row = lax.broadcasted_iota(jnp.int32, (span, tq), 0)
    lane = lax.broadcasted_iota(jnp.int32, (span, tq), 1)
    diff = (lane + tq) - row
    band = (diff >= 0) & (diff <= DIL_SPAN)
    has_prev = pl.program_id(1) > 0
    first = band & ((row >= tq) | has_prev)
    head_of_lane = lax.broadcasted_iota(jnp.int32, (tq, LANES), 1) // HEAD_DIM
    vt_ref[:, LANES:, :] = jnp.ones((nq * d, ONES_ROWS, span), vt_ref.dtype)

    def scores(j, r):
        rows = pl.ds(r + j * tq * d, tq, stride=d)
        qb = q_ref[rows, :]
        if j == 0:
            kb = jnp.concatenate([kp_ref[pl.ds(r, tq, stride=d), :], kc_ref[rows, :]], axis=0)
            vb = jnp.concatenate([vp_ref[pl.ds(r, tq, stride=d), :], vc_ref[rows, :]], axis=0)
        else:
            kv_rows = pl.ds(r + (j - 1) * tq * d, span, stride=d)
            kb, vb = kc_ref[kv_rows, :], vc_ref[kv_rows, :]
        vt_ref[j * d + r, 0:LANES, :] = vb.T.astype(BF16)
        q_bd = jnp.concatenate([jnp.where(head_of_lane == h, qb, 0.0) for h in range(nh)], axis=0).astype(BF16)
        return _dot_nt(kb.astype(BF16), q_bd)

    def finish(j, r, s):
        mask = first if j == 0 else band
        ms, ps = [], []
        for h in range(nh):
            sh = jnp.where(mask, s[:, h * tq:(h + 1) * tq], NEG)
            m = jnp.max(sh, axis=0, keepdims=True)
            ms.append(m)
            ps.append(jnp.exp2(sh - m).astype(BF16))
        pv = _dot(vt_ref[j * d + r], jnp.concatenate(ps, axis=1))
        o_t, l_t = [], []
        for h in range(nh):
            den = pv[LANES:LANES + 1, h * tq:(h + 1) * tq]
            o_t.append(pv[h * HEAD_DIM:(h + 1) * HEAD_DIM, h * tq:(h + 1) * tq] / den)
            l_t.append(jnp.broadcast_to(ms[h] + jnp.log2(den), (HEAD_DIM, tq)))
        rows = pl.ds(r + j * tq * d, tq, stride=d)
        o_ref[rows, :] = jnp.concatenate(o_t, axis=0).T
        lse_ref[rows, :] = jnp.concatenate(l_t, axis=0).T

    pending = None
    for j in range(nq):
        for r in range(d):
            s = scores(j, r)
            if pending is not None:
                finish(*pending)
            pending = (j, r, s)
    finish(*pending)


def _dilated(qb, kb, vb, g, block_tokens):
    B, S, _ = qb.shape
    d = DIL_PAIRS[g][1]
    tq = min(LANES, S // d)
    nq = max(block_tokens // (d * tq), 1)
    assert DIL_PAIRS[g][0] // d == DIL_SPAN and DIL_SPAN <= tq and S % (nq * d * tq) == 0
    gw = DIL_HPG * HEAD_DIM
    nb = gw // LANES
    cur = pl.BlockSpec((None, nq * d * tq, LANES), lambda b, i, c: (b, i, g * nb + c))
    prev = pl.BlockSpec((None, d * tq, LANES), lambda b, i, c: (b, jnp.maximum(i * nq - 1, 0), g * nb + c))
    out_blk = pl.BlockSpec((None, nq * d * tq, LANES), lambda b, i, c: (b, i, c))
    return pl.pallas_call(
        functools.partial(_dilated_kernel, d=d, tq=tq, nq=nq),
        grid=(B, S // (nq * d * tq), nb),
        in_specs=[cur, prev, cur, prev, cur],
        out_specs=[out_blk, out_blk],
        out_shape=[jax.ShapeDtypeStruct((B, S, gw), F32)] * 2,
        scratch_shapes=[pltpu.VMEM((nq * d, LANES + ONES_ROWS, 2 * tq), BF16)],
        compiler_params=pltpu.CompilerParams(dimension_semantics=("parallel", "arbitrary", "arbitrary"),
                                             vmem_limit_bytes=VMEM_LIMIT),
        name=f"dilated{g}",
    )(qb, kb, kb, vb, vb)


def _mixout_kernel(x_ref, ya_ref, o0_ref, o1_ref, o2_ref, l0_ref, l1_ref, l2_ref, gma_ref, gmb_ref,
                   woa_ref, wob_ref, wout_ref, out_ref):
    l0, l1, l2 = l0_ref[...], l1_ref[...], l2_ref[...]
    mx = jnp.maximum(jnp.maximum(l0, l1), l2)
    e0, e1, e2 = jnp.exp2(l0 - mx), jnp.exp2(l1 - mx), jnp.exp2(l2 - mx)
    yb = (e0 * o0_ref[...] + e1 * o1_ref[...] + e2 * o2_ref[...]) / (e0 + e1 + e2)
    ta = _dot(ya_ref[...], woa_ref[...])
    tb = _dot(yb.astype(BF16), wob_ref[...])
    mixed = gma_ref[...].astype(F32) * ta + gmb_ref[...].astype(F32) * tb
    out_ref[...] = x_ref[...] + _dot(mixed.astype(BF16), wout_ref[...])


def _mixout(x, ya, dil, gma, gmb, w_o_a, w_o_b, w_out, tm):
    B, S, D = x.shape
    row = lambda w: pl.BlockSpec((None, tm, w), lambda b, m: (b, m, 0))
    full = lambda a: pl.BlockSpec(a.shape, lambda b, m: (0, 0))
    gw = DIL_HPG * HEAD_DIM
    wa, wb, wo = w_o_a.astype(BF16), w_o_b.astype(BF16), w_out.astype(BF16)
    return pl.pallas_call(
        _mixout_kernel,
        grid=(B, S // tm),
        in_specs=[row(D), row(ya.shape[-1])] + [row(gw)] * 6 + [row(D), row(D), full(wa), full(wb), full(wo)],
        out_specs=row(D),
        out_shape=jax.ShapeDtypeStruct((B, S, D), F32),
        compiler_params=pltpu.CompilerParams(dimension_semantics=("parallel", "parallel"),
                                             vmem_limit_bytes=VMEM_LIMIT),
        name="mixout",
    )(x, ya, dil[0][0], dil[1][0], dil[2][0], dil[0][1], dil[1][1], dil[2][1], gma, gmb, wa, wb, wo)


def _mlp_kernel(x_ref, g_ref, wup_ref, wdn_ref, out_ref, *, fc):
    x = x_ref[...]
    ms = jnp.mean(x * x, axis=-1, keepdims=True)
    h = (x * lax.rsqrt(ms + NORM_EPS) * g_ref[...]).astype(BF16)
    acc = x
    for off, w in _chunks(wup_ref.shape[1], fc):
        u = jnp.maximum(_dot(h, wup_ref[:, off:off + w]), 0.0)
        acc = acc + _dot((u * u).astype(BF16), wdn_ref[off:off + w, :])
    out_ref[...] = acc


def _mlp(x, norm2_g, w_up, w_down, tm, fc):
    B, S, D = x.shape
    F = w_up.shape[1]
    row = pl.BlockSpec((None, tm, D), lambda b, m: (b, m, 0))
    const = lambda b, m: (0, 0)
    return pl.pallas_call(
        functools.partial(_mlp_kernel, fc=fc),
        grid=(B, S // tm),
        in_specs=[row, pl.BlockSpec((1, D), const), pl.BlockSpec((D, F), const), pl.BlockSpec((F, D), const)],
        out_specs=row,
        out_shape=jax.ShapeDtypeStruct((B, S, D), F32),
        compiler_params=pltpu.CompilerParams(dimension_semantics=("parallel", "parallel"),
                                             vmem_limit_bytes=VMEM_LIMIT),
        name="mlp",
    )(x, norm2_g.reshape(1, D), w_up.astype(BF16), w_down.astype(BF16))


def kernel(x, norm1_g, w_in, q_norm_a, k_norm_cmp, k_norm_slc, k_norm_win, cmp_k_pos, cmp_k_w1, cmp_k_w2,
           cmp_v_pos, cmp_v_w1, cmp_v_w2, q_norm_b, k_norm_b, w_o_a, w_o_b, w_out, norm2_g, w_up, w_down):
    depth = w_in.shape[0]
    for i in range(depth):
        (qa, ks, kw, qb, kb, kvc_raw, vb, gma, gmb, gns, vst, vwt) = _inproj(
            x, norm1_g[i], w_in[i], q_norm_a[i], k_norm_slc[i], k_norm_win[i], q_norm_b[i], k_norm_b[i], tm=512)
        kc, vct = _compress(kvc_raw, k_norm_cmp[i], cmp_k_pos[i], cmp_k_w1[i], cmp_k_w2[i],
                           cmp_v_pos[i], cmp_v_w1[i], cmp_v_w2[i])
        ya = _nsa(qa, kc, vct, ks, vst, kw, vwt, gns, tq=128, tk=512)
        dil = [_dilated(qb, kb, vb, g, block_tokens=1024) for g in range(DIL_GROUPS)]
        x = _mixout(x, ya, dil, gma, gmb, w_o_a[i], w_o_b[i], w_out[i], tm=256)
        x = _mlp(x, norm2_g[i], w_up[i], w_down[i], tm=512, fc=1024)
    return x
```

```python
import functools

import numpy as np
import jax
import jax.numpy as jnp
from jax import lax
from jax.experimental import pallas as pl
from jax.experimental.pallas import tpu as pltpu

HEAD_DIM = 64
HALF = HEAD_DIM // 2
ROPE_THETA = 10000.0
NORM_EPS = 1e-6
NEG = -1e30
NSA_HEADS = 12
NSA_GROUPS = 3
NSA_HPG = NSA_HEADS // NSA_GROUPS
CMP_BLOCK = 32
CMP_STRIDE = 16
SLC_BLOCK = 64
SLC_TOPK = 16
WIN_SIZE = 512
DIL_PAIRS = ((128, 1), (512, 4), (2048, 16))
DIL_GROUPS = len(DIL_PAIRS)
DIL_HPG = 4
DIL_HEADS = DIL_GROUPS * DIL_HPG
DIL_SPAN = 128
LOG2E = float(np.log2(np.e))
ONES_ROWS = 16

LANES = 128
MXU_N = 256
VMEM_LIMIT = 56 * 1024 * 1024

BF16 = jnp.bfloat16
F32 = jnp.float32

ROPE_SEGS = (("qa", 768), ("qb", 768), ("kb", 768), ("ks", 192), ("kw", 192))
PLAIN_SEGS = (("kvc", 384), ("vb", 768))
SIG_SEGS = (("gma", 1024), ("gmb", 1024), ("gns", 384))
ROPE_W = sum(w for _, w in ROPE_SEGS)
PLAIN_W = sum(w for _, w in PLAIN_SEGS)
SIG_W = sum(w for _, w in SIG_SEGS)
HEAD_MAJOR = ("qa", "ks", "kw")


def _dot(a, b):
    return jnp.dot(a, b, preferred_element_type=F32)


def _dot_nt(a, b):
    return lax.dot_general(a, b, (((1,), (1,)), ((), ())), preferred_element_type=F32)


def _chunks(total, width):
    out, off = [], 0
    while off < total:
        w = min(width, total - off)
        out.append((off, w))
        off += w
    return out


def _seg_lookup(segs, col):
    off = 0
    for name, w in segs:
        if col < off + w:
            return name, col - off
        off += w
    raise ValueError(col)


def _inproj_kernel(x_ref, g1_ref, w_ref, wvt_ref, gain_ref, bd_ref, cos_ref, sin_ref,
                   qa_ref, ks_ref, kw_ref, qb_ref, kb_ref,
                   kvc_ref, vb_ref, gma_ref, gmb_ref, gns_ref, vst_ref, vwt_ref):
    outs = dict(qa=qa_ref, ks=ks_ref, kw=kw_ref, qb=qb_ref, kb=kb_ref, kvc=kvc_ref,
                vb=vb_ref, gma=gma_ref, gmb=gmb_ref, gns=gns_ref)

    def emit(segs, col, val):
        name, rel = _seg_lookup(segs, col)
        ref = outs[name]
        if name in HEAD_MAJOR:
            for p in range(LANES // HEAD_DIM):
                nm, r = _seg_lookup(segs, col + p * HEAD_DIM)
                outs[nm][r // HEAD_DIM] = val[:, p * HEAD_DIM:(p + 1) * HEAD_DIM].astype(outs[nm].dtype)
        else:
            nm2, _ = _seg_lookup(segs, col + HEAD_DIM)
            if nm2 != name:
                raise ValueError("unaligned dense segment")
            ref[:, rel:rel + LANES] = val.astype(ref.dtype)

    x = x_ref[...]
    ms = jnp.mean(x * x, axis=-1, keepdims=True)
    h = (x * lax.rsqrt(ms + NORM_EPS) * g1_ref[...]).astype(BF16)

    lane = lax.broadcasted_iota(jnp.int32, (x.shape[0], LANES), 1)
    first_half = (lane % HEAD_DIM) < HALF
    cos = cos_ref[...]
    sin = sin_ref[...]
    bd = bd_ref[...]

    def rope_epilogue(off, w, y):
        msq = _dot((y * y).astype(BF16), bd[:w, :w])
        yn = y * lax.rsqrt(msq + NORM_EPS) * gain_ref[:, off:off + w]
        for u in range(w // LANES):
            v = yn[:, u * LANES:(u + 1) * LANES]
            rot = jnp.where(first_half, pltpu.roll(v, LANES - HALF, 1), pltpu.roll(v, HALF, 1))
            emit(ROPE_SEGS, off + u * LANES, v * cos + rot * sin)

    def plain_epilogue(off, w, y):
        for u in range(w // LANES):
            emit(PLAIN_SEGS, off + u * LANES, y[:, u * LANES:(u + 1) * LANES])

    def sig_epilogue(off, w, y):
        y = jax.nn.sigmoid(y)
        for u in range(w // LANES):
            emit(SIG_SEGS, off + u * LANES, y[:, u * LANES:(u + 1) * LANES])

    work = ([(0, off, w, rope_epilogue) for off, w in _chunks(ROPE_W, MXU_N)]
            + [(ROPE_W, off, w, plain_epilogue) for off, w in _chunks(PLAIN_W, MXU_N)]
            + [(ROPE_W + PLAIN_W, off, w, sig_epilogue) for off, w in _chunks(SIG_W, MXU_N)])
    pending = None
    for base, off, w, epilogue in work:
        y = _dot(h, w_ref[:, base + off:base + off + w])
        if pending is not None:
            pending[0](pending[1], pending[2], pending[3])
        pending = (epilogue, off, w, y)
    pending[0](pending[1], pending[2], pending[3])

    v_t = _dot_nt(wvt_ref[...], h)
    ones = jnp.ones((ONES_ROWS, x.shape[0]), vst_ref.dtype)
    for i, ref in enumerate((vst_ref, vwt_ref)):
        for g in range(NSA_GROUPS):
            r0 = (i * NSA_GROUPS + g) * HEAD_DIM
            ref[g, 0:HEAD_DIM, :] = v_t[r0:r0 + HEAD_DIM].astype(ref.dtype)
            ref[g, HEAD_DIM:HEAD_DIM + ONES_ROWS, :] = ones


def _rope_tables(positions, width):
    inv_freq = np.power(ROPE_THETA, -np.arange(HALF, dtype=np.float64) / HALF)
    ang = np.asarray(positions, np.float64)[:, None] * inv_freq[None, :]
    reps = width // HEAD_DIM
    cos = np.tile(np.concatenate([np.cos(ang), np.cos(ang)], axis=1), (1, reps))
    sin = np.tile(np.concatenate([-np.sin(ang), np.sin(ang)], axis=1), (1, reps))
    return jnp.asarray(cos, F32), jnp.asarray(sin, F32)


def _inproj(x, norm1_g, w_in, q_norm_a, k_norm_slc, k_norm_win, q_norm_b, k_norm_b, tm):
    B, S, D = x.shape
    scale = HEAD_DIM ** -0.5
    sp = np.cumsum((0, 768, 192, 192, 192, 192, 192, 192, 36, 768, 768, 768, 1024, 1024))
    names = ("qa", "kc", "vc", "ks", "vs", "kw", "vw", "gns", "qb", "kb", "vb", "gma", "gmb")
    w_bf = w_in.astype(BF16)
    col = {n: w_bf[:, int(sp[i]):int(sp[i + 1])] for i, n in enumerate(names)}
    gpg = NSA_HPG * 3
    gns = jnp.concatenate(
        [jnp.pad(col["gns"][:, g * gpg:(g + 1) * gpg], ((0, 0), (0, LANES - gpg))) for g in range(NSA_GROUPS)],
        axis=1)
    col["gns"] = gns
    col["kvc"] = jnp.concatenate([col["kc"], col["vc"]], axis=1)
    w_p = jnp.concatenate([col[n] for n, _ in ROPE_SEGS + PLAIN_SEGS + SIG_SEGS], axis=1)
    wvt = jnp.concatenate([col["vs"], col["vw"]], axis=1).T
    ncol = ROPE_W + PLAIN_W + SIG_W
    assert w_p.shape == (D, ncol)

    gain = jnp.concatenate([
        jnp.tile(q_norm_a * (scale * LOG2E), NSA_HEADS), jnp.tile(q_norm_b * (scale * LOG2E), DIL_HEADS),
        jnp.tile(k_norm_b, DIL_HEADS), jnp.tile(k_norm_slc, NSA_GROUPS), jnp.tile(k_norm_win, NSA_GROUPS)
    ]).reshape(1, ROPE_W).astype(F32)
    bd = jnp.asarray(np.kron(np.eye(MXU_N // HEAD_DIM), np.full((HEAD_DIM, HEAD_DIM), 1.0 / HEAD_DIM)), BF16)
    cos, sin = _rope_tables(np.arange(S), LANES)

    def hm(nh, dt):
        return (jax.ShapeDtypeStruct((B, nh, S, HEAD_DIM), dt),
                pl.BlockSpec((None, nh, tm, HEAD_DIM), lambda b, m: (b, 0, m, 0)))

    def vt():
        rows = HEAD_DIM + ONES_ROWS
        return (jax.ShapeDtypeStruct((B, NSA_GROUPS, rows, S), BF16),
                pl.BlockSpec((None, NSA_GROUPS, rows, tm), lambda b, m: (b, 0, 0, m)))

    def dense(w, dt):
        return (jax.ShapeDtypeStruct((B, S, w), dt), pl.BlockSpec((None, tm, w), lambda b, m: (b, m, 0)))

    outs = [hm(NSA_HEADS, BF16), hm(NSA_GROUPS, BF16), hm(NSA_GROUPS, BF16),
            dense(768, F32), dense(768, F32),
            dense(384, F32),
            dense(768, F32),
            dense(1024, BF16), dense(1024, BF16), dense(384, F32),
            vt(), vt()]
    const = lambda b, m: (0, 0)
    return pl.pallas_call(
        _inproj_kernel,
        grid=(B, S // tm),
        in_specs=[pl.BlockSpec((None, tm, D), lambda b, m: (b, m, 0)),
                  pl.BlockSpec((1, D), const),
                  pl.BlockSpec((D, ncol), const),
                  pl.BlockSpec(wvt.shape, const),
                  pl.BlockSpec((1, ROPE_W), const),
                  pl.BlockSpec((MXU_N, MXU_N), const),
                  pl.BlockSpec((tm, LANES), lambda b, m: (m, 0)),
                  pl.BlockSpec((tm, LANES), lambda b, m: (m, 0))],
        out_specs=[o[1] for o in outs],
        out_shape=[o[0] for o in outs],
        compiler_params=pltpu.CompilerParams(dimension_semantics=("parallel", "arbitrary"),
                                             vmem_limit_bytes=VMEM_LIMIT),
        name="inproj",
    )(x, norm1_g.reshape(1, D), w_p, wvt, gain, bd, cos, sin)


def _gelu_tanh(x):
    return 0.5 * x * (1.0 + jnp.tanh(np.sqrt(2.0 / np.pi) * (x + 0.044715 * (x * x * x))))


def _compress_kernel(x0_ref, x1_ref, x2_ref, pk_ref, pv_ref, w1k_ref, w2k_ref, w1v_ref, w2vt_ref,
                     gain_ref, cos_ref, sin_ref, kc_ref, vct_ref):
    nc = x0_ref.shape[0] // CMP_STRIDE
    half = CMP_STRIDE * HEAD_DIM
    per_blk = LANES // HEAD_DIM
    xs = [[x_ref[pl.ds(l, nc, stride=CMP_STRIDE), :] for l in range(CMP_STRIDE)] for x_ref in (x0_ref, x1_ref, x2_ref)]

    def hidden(head, pos_ref, w1_ref):
        blk, sub = divmod(head, per_blk)
        cols = slice(sub * HEAD_DIM, (sub + 1) * HEAD_DIM)
        x = [xs[blk][l][:, cols] for l in range(CMP_STRIDE)]
        xa = jnp.concatenate([x[l] + pos_ref[l:l + 1, :] for l in range(CMP_STRIDE)], axis=1)
        xb = jnp.concatenate([x[l] + pos_ref[CMP_STRIDE + l:CMP_STRIDE + l + 1, :] for l in range(CMP_STRIDE)], axis=1)
        a = _dot(xa.astype(BF16), w1_ref[0:half, :])
        b = _dot(xb.astype(BF16), w1_ref[half:2 * half, :])
        pre = a + pltpu.roll(b, nc - 1, 0)
        return _gelu_tanh(pre).astype(BF16)

    for g in range(NSA_GROUPS):
        kc = _dot(hidden(g, pk_ref, w1k_ref), w2k_ref[...])
        vct = _dot_nt(w2vt_ref[...], hidden(NSA_GROUPS + g, pv_ref, w1v_ref))
        ms = jnp.mean(kc * kc, axis=-1, keepdims=True)
        kn = kc * lax.rsqrt(ms + NORM_EPS) * gain_ref[...]
        rot = jnp.concatenate([kn[:, HALF:], kn[:, :HALF]], axis=-1)
        kc_ref[g] = (kn * cos_ref[...] + rot * sin_ref[...]).astype(kc_ref.dtype)
        vct_ref[g] = vct.astype(vct_ref.dtype)


def _compress(kvc_raw, k_norm_cmp, kpos, kw1, kw2, vpos, vw1, vw2):
    B, S, w = kvc_raw.shape
    G, dh = NSA_GROUPS, HEAD_DIM
    assert w == 2 * G * dh == 3 * LANES
    nc = S // CMP_STRIDE
    cos, sin = _rope_tables(np.arange(nc) * CMP_STRIDE + CMP_BLOCK - 1, dh)
    const = lambda b: (0, 0)
    xblk = lambda j: pl.BlockSpec((None, S, LANES), lambda b: (b, 0, j))
    hid = kw1.shape[1]
    return pl.pallas_call(
        _compress_kernel,
        grid=(B,),
        in_specs=[xblk(0), xblk(1), xblk(2),
                  pl.BlockSpec((CMP_BLOCK, dh), const), pl.BlockSpec((CMP_BLOCK, dh), const),
                  pl.BlockSpec((CMP_BLOCK * dh, hid), const), pl.BlockSpec((hid, dh), const),
                  pl.BlockSpec((CMP_BLOCK * dh, hid), const), pl.BlockSpec((dh, hid), const),
                  pl.BlockSpec((1, dh), const), pl.BlockSpec((nc, dh), const), pl.BlockSpec((nc, dh), const)],
        out_specs=[pl.BlockSpec((None, G, nc, dh), lambda b: (b, 0, 0, 0)),
                   pl.BlockSpec((None, G, dh, nc), lambda b: (b, 0, 0, 0))],
        out_shape=[jax.ShapeDtypeStruct((B, G, nc, dh), BF16), jax.ShapeDtypeStruct((B, G, dh, nc), BF16)],
        compiler_params=pltpu.CompilerParams(dimension_semantics=("parallel",), vmem_limit_bytes=VMEM_LIMIT),
        name="compress",
    )(kvc_raw, kvc_raw, kvc_raw, kpos, vpos, kw1.astype(BF16), kw2.astype(BF16), vw1.astype(BF16),
      vw2.T.astype(BF16), k_norm_cmp.reshape(1, dh).astype(F32), cos, sin)


def _nsa_kernel(q_ref, kc_ref, vct_ref, ks_ref, vst_ref, kw_ref, vwt_ref, gate_ref, ovt_ref,
                o_ref, bias_ref, s_ref, *, tq, tk, n_cmp, n_slc, seq):
    R = NSA_HPG
    qt = pl.program_id(2)
    t0 = qt * tq
    t0a = pl.multiple_of(t0, tq)
    q2 = q_ref[...].reshape(R * tq, HEAD_DIM)
    ncp = kc_ref.shape[0]
    t_lane = t0 + lax.broadcasted_iota(jnp.int32, (1, tq), 1)
    bpt = tk // SLC_BLOCK
    last_tile = seq // tk - 1
    span = min(WIN_SIZE + tq, seq)
    start = pl.multiple_of(jnp.maximum(t0 - WIN_SIZE, 0), tq)

    def heads(a):
        return [a[:, r * tq:(r + 1) * tq] for r in range(R)]

    def qk(kt, slot):
        k0 = pl.multiple_of(jnp.minimum(kt, last_tile) * tk, tk)
        s_ref[slot] = _dot_nt(ks_ref[pl.ds(k0, tk), :], q2)

    s_c = _dot_nt(kc_ref[...], q2)
    s_d = _dot_nt(ks_ref[pl.ds(t0a, tq), :], q2)
    s_w = _dot_nt(kw_ref[pl.ds(start, span), :], q2)
    qk(0, 0)

    c_idx = lax.broadcasted_iota(jnp.int32, (ncp, tq), 0)
    cmask = ((c_idx * CMP_STRIDE + (CMP_BLOCK - 1)) <= t_lane) & (c_idx < n_cmp)
    ps = []
    for sr in heads(s_c):
        sr = jnp.where(cmask, sr, NEG)
        m = jnp.max(sr, axis=0, keepdims=True)
        e = jnp.exp2(sr - m)
        den = jnp.sum(e, axis=0, keepdims=True)
        ps.append(e * jnp.where(m > 0.5 * NEG, 1.0 / den, 0.0))
    o_cmp = _dot(vct_ref[...], jnp.concatenate(ps, axis=1).astype(BF16))

    psum = ps[0]
    for r in range(1, R):
        psum = psum + ps[r]
    p_hi = psum.astype(BF16)
    p_lo = (psum - p_hi.astype(F32)).astype(BF16)
    nb = -(-n_slc // 8) * 8
    imp = (_dot(ovt_ref[...], p_hi) + _dot(ovt_ref[...], p_lo))[:nb]
    n_b = lax.broadcasted_iota(jnp.int32, (nb, tq), 0)
    cur = (t0 + lax.broadcasted_iota(jnp.int32, (nb, tq), 1)) // SLC_BLOCK
    forced = (n_b == 0) | (n_b == cur) | (n_b == cur - 1)
    visible = n_b <= cur

    tri = lax.broadcasted_iota(jnp.int32, (tq, tq), 0) <= lax.broadcasted_iota(jnp.int32, (tq, tq), 1)
    m_d, p_d = [], []
    for sr in heads(s_d):
        sr = jnp.where(tri, sr, NEG)
        m = jnp.max(sr, axis=0, keepdims=True)
        m_d.append(m)
        p_d.append(jnp.exp2(sr - m).astype(BF16))
    carry = (jnp.concatenate(m_d, axis=1), _dot(vst_ref[:, pl.ds(t0a, tq)], jnp.concatenate(p_d, axis=1)))

    diff = t_lane - (start + lax.broadcasted_iota(jnp.int32, (span, tq), 0))
    wmask = (diff >= 0) & (diff < WIN_SIZE)
    pw = []
    for sr in heads(s_w):
        sr = jnp.where(wmask, sr, NEG)
        m = jnp.max(sr, axis=0, keepdims=True)
        pw.append(jnp.exp2(sr - m).astype(BF16))
    acc_w = _dot(vwt_ref[:, pl.ds(start, span)], jnp.concatenate(pw, axis=1))
    o_win = acc_w[:HEAD_DIM] / acc_w[HEAD_DIM:HEAD_DIM + 1]

    top_n = min(SLC_TOPK, n_slc)
    n_f = n_b.astype(F32)
    work = jnp.where(forced, -jnp.inf, jnp.where(visible, imp, -jnp.inf))
    picked = jnp.zeros((nb, tq), F32)
    for _ in range(max(top_n - 3, 0)):
        mx = jnp.max(work, axis=0, keepdims=True)
        first = jnp.min(jnp.where(work == mx, n_f, float(nb)), axis=0, keepdims=True)
        hit = n_f == jnp.where(mx > -jnp.inf, first, -1.0)
        picked = jnp.where(hit, 1.0, picked)
        work = jnp.where(hit, -jnp.inf, work)
    chosen = jnp.where(forced, 1.0, jnp.where(cur < top_n, 1.0, picked))
    live = jnp.where(visible, jnp.where(n_b < t0 // SLC_BLOCK, chosen, 0.0), 0.0)
    bias_ref[:nb] = jnp.where(live > 0.5, 0.0, NEG)
    if nb < LANES:
        bias_ref[nb:] = jnp.full((LANES - nb, tq), NEG, F32)

    def update(kt, slot, carry):
        m_old, acc = carry
        k0 = pl.multiple_of(kt * tk, tk)
        brows = [bias_ref[pl.ds(kt * bpt + j, 1), :] for j in range(bpt)]
        m_new, p_all, alphas = [], [], []
        for r in range(R):
            cols = slice(r * tq, (r + 1) * tq)
            mo = m_old[:, cols]
            m8 = None
            for j in range(bpt):
                blk = s_ref[slot, j * SLC_BLOCK:(j + 1) * SLC_BLOCK, cols]
                b8 = jnp.max(blk.reshape(SLC_BLOCK // 8, 8, tq), axis=0) + brows[j]
                m8 = b8 if m8 is None else jnp.maximum(m8, b8)
            mn = jnp.maximum(mo, jnp.max(m8, axis=0, keepdims=True))
            p_all.append(jnp.concatenate(
                [jnp.exp2(s_ref[slot, j * SLC_BLOCK:(j + 1) * SLC_BLOCK, cols] + (brows[j] - mn)).astype(BF16)
                 for j in range(bpt)], axis=0))
            alphas.append(jnp.exp2(mo - mn))
            m_new.append(mn)
        pv = _dot(vst_ref[:, pl.ds(k0, tk)], jnp.concatenate(p_all, axis=1))
        return jnp.concatenate(m_new, axis=1), jnp.concatenate(alphas, axis=1) * acc + pv

    def pair_body(ii, carry):
        a = 2 * ii
        qk(a + 1, 1)
        carry = update(a, 0, carry)
        qk(a + 2, 0)
        return update(a + 1, 1, carry)

    n_main = (t0 + tk - 1) // tk
    _, acc = lax.fori_loop(0, (n_main + 1) // 2, pair_body, carry)
    o_slc = acc[:HEAD_DIM] / acc[HEAD_DIM:HEAD_DIM + 1]

    gate_t = gate_ref[...].T
    ys = []
    for r in range(R):
        cols = slice(r * tq, (r + 1) * tq)
        ys.append(gate_t[3 * r:3 * r + 1] * o_cmp[:, cols] + gate_t[3 * r + 1:3 * r + 2] * o_slc[:, cols]
                  + gate_t[3 * r + 2:3 * r + 3] * o_win[:, cols])
    o_ref[...] = jnp.concatenate(ys, axis=0).T.astype(o_ref.dtype)


def _overlap_t(ncp, n_cmp, n_slc):
    cs = np.arange(ncp)[None, :] * CMP_STRIDE
    ss = np.arange(LANES)[:, None] * SLC_BLOCK
    ov = np.clip(np.minimum(cs + CMP_BLOCK, ss + SLC_BLOCK) - np.maximum(cs, ss), 0, None) / CMP_BLOCK
    ov = ov * (np.arange(ncp)[None, :] < n_cmp) * (np.arange(LANES)[:, None] < n_slc)
    return jnp.asarray(ov, BF16)


def _nsa(qa, kc, vct, ks, vst, kw, vwt, gns, tq, tk):
    B, H, S, dh = qa.shape
    ncp = kc.shape[2]
    n_cmp = (S - CMP_BLOCK) // CMP_STRIDE + 1
    n_slc = S // SLC_BLOCK
    assert n_slc <= LANES and S % (2 * tk) == 0 and tk % SLC_BLOCK == 0 and tq == 2 * SLC_BLOCK
    ovt = _overlap_t(ncp, n_cmp, n_slc)
    k_c = pl.BlockSpec((None, None, ncp, dh), lambda b, g, t: (b, g, 0, 0))
    v_c = pl.BlockSpec((None, None, dh, ncp), lambda b, g, t: (b, g, 0, 0))
    k_s = pl.BlockSpec((None, None, S, dh), lambda b, g, t: (b, g, 0, 0))
    v_s = pl.BlockSpec((None, None, vst.shape[2], S), lambda b, g, t: (b, g, 0, 0))
    v_w = pl.BlockSpec((None, None, vwt.shape[2], S), lambda b, g, t: (b, g, 0, 0))
    const = lambda b, g, t: (0, 0)
    kern = functools.partial(_nsa_kernel, tq=tq, tk=tk, n_cmp=n_cmp, n_slc=n_slc, seq=S)
    return pl.pallas_call(
        kern,
        grid=(B, NSA_GROUPS, S // tq),
        in_specs=[pl.BlockSpec((None, NSA_HPG, tq, dh), lambda b, g, t: (b, g, t, 0)),
                  k_c, v_c, k_s, v_s, k_s, v_w,
                  pl.BlockSpec((None, tq, LANES), lambda b, g, t: (b, t, g)),
                  pl.BlockSpec((LANES, ncp), const)],
        out_specs=pl.BlockSpec((None, tq, NSA_HPG * dh), lambda b, g, t: (b, t, g)),
        out_shape=jax.ShapeDtypeStruct((B, S, H * dh), BF16),
        scratch_shapes=[pltpu.VMEM((LANES, tq), F32),
                        pltpu.VMEM((2, tk, NSA_HPG * tq), F32)],
        compiler_params=pltpu.CompilerParams(dimension_semantics=("parallel", "parallel", "arbitrary"),
                                             vmem_limit_bytes=VMEM_LIMIT),
        name="nsa",
    )(qa, kc, vct, ks, vst, kw, vwt, gns, ovt)


def _dilated_kernel(q_ref, kp_ref, kc_ref, vp_ref, vc_ref, o_ref, lse_ref, vt_ref, *, d, tq, nq):
    span = 2 * tq
    nh = LANES // HEAD_DIM
    row = lax.broadcasted_iota(jnp.int32, (span, tq), 0)
    lane = lax.broadcasted_iota(jnp.int32, (span, tq), 1)
    diff = (lane + tq) - row
    band = (diff >= 0) & (diff <= DIL_SPAN)
    has_prev = pl.program_id(1) > 0
    first = band & ((row >= tq) | has_prev)
    head_of_lane = lax.broadcasted_iota(jnp.int32, (tq, LANES), 1) // HEAD_DIM
    vt_ref[:, LANES:, :] = jnp.ones((nq * d, ONES_ROWS, span), vt_ref.dtype)

    def scores(j, r):
        rows = pl.ds(r + j * tq * d, tq, stride=d)
        qb = q_ref[rows, :]
        if j == 0:
            kb = jnp.concatenate([kp_ref[pl.ds(r, tq, stride=d), :], kc_ref[rows, :]], axis=0)
            vb = jnp.concatenate([vp_ref[pl.ds(r, tq, stride=d), :], vc_ref[rows, :]], axis=0)
        else:
            kv_rows = pl.ds(r + (j - 1) * tq * d, span, stride=d)
            kb, vb = kc_ref[kv_rows, :], vc_ref[kv_rows, :]
        vt_ref[j * d + r, 0:LANES, :] = vb.T.astype(BF16)
        q_bd = jnp.concatenate([jnp.where(head_of_lane == h, qb, 0.0) for h in range(nh)], axis=0).astype(BF16)
        return _dot_nt(kb.astype(BF16), q_bd)

    def finish(j, r, s):
        mask = first if j == 0 else band
        ms, ps = [], []
        for h in range(nh):
            sh = jnp.where(mask, s[:, h * tq:(h + 1) * tq], NEG)
            m = jnp.max(sh, axis=0, keepdims=True)
            ms.append(m)
            ps.append(jnp.exp2(sh - m).astype(BF16))
        pv = _dot(vt_ref[j * d + r], jnp.concatenate(ps, axis=1))
        o_t, l_t = [], []
        for h in range(nh):
            den = pv[LANES:LANES + 1, h * tq:(h + 1) * tq]
            o_t.append(pv[h * HEAD_DIM:(h + 1) * HEAD_DIM, h * tq:(h + 1) * tq] / den)
            l_t.append(jnp.broadcast_to(ms[h] + jnp.log2(den), (HEAD_DIM, tq)))
        rows = pl.ds(r + j * tq * d, tq, stride=d)
        o_ref[rows, :] = jnp.concatenate(o_t, axis=0).T
        lse_ref[rows, :] = jnp.concatenate(l_t, axis=0).T

    pending = None
    for j in range(nq):
        for r in range(d):
            s = scores(j, r)
            if pending is not None:
                finish(*pending)
            pending = (j, r, s)
    finish(*pending)


def _dilated(qb, kb, vb, g, block_tokens):
    B, S, _ = qb.shape
    d = DIL_PAIRS[g][1]
    tq = min(LANES, S // d)
    nq = max(block_tokens // (d * tq), 1)
    assert DIL_PAIRS[g][0] // d == DIL_SPAN and DIL_SPAN <= tq and S % (nq * d * tq) == 0
    gw = DIL_HPG * HEAD_DIM
    nb = gw // LANES
    cur = pl.BlockSpec((None, nq * d * tq, LANES), lambda b, i, c: (b, i, g * nb + c))
    prev = pl.BlockSpec((None, d * tq, LANES), lambda b, i, c: (b, jnp.maximum(i * nq - 1, 0), g * nb + c))
    out_blk = pl.BlockSpec((None, nq * d * tq, LANES), lambda b, i, c: (b, i, c))
    return pl.pallas_call(
        functools.partial(_dilated_kernel, d=d, tq=tq, nq=nq),
        grid=(B, S // (nq * d * tq), nb),
        in_specs=[cur, prev, cur, prev, cur],
        out_specs=[out_blk, out_blk],
        out_shape=[jax.ShapeDtypeStruct((B, S, gw), F32)] * 2,
        scratch_shapes=[pltpu.VMEM((nq * d, LANES + ONES_ROWS, 2 * tq), BF16)],
        compiler_params=pltpu.CompilerParams(dimension_semantics=("parallel", "arbitrary", "arbitrary"),
                                             vmem_limit_bytes=VMEM_LIMIT),
        name=f"dilated{g}",
    )(qb, kb, kb, vb, vb)


def _out_mlp_kernel(x_ref, ya_ref, o0_ref, o1_ref, o2_ref, l0_ref, l1_ref, l2_ref, gma_ref, gmb_ref,
                    woa_ref, wob_ref, wout_ref, g2_ref, wup_ref, wdn_ref, out_ref, *, fc):
    l0, l1, l2 = l0_ref[...], l1_ref[...], l2_ref[...]
    mx = jnp.maximum(jnp.maximum(l0, l1), l2)
    e0, e1, e2 = jnp.exp2(l0 - mx), jnp.exp2(l1 - mx), jnp.exp2(l2 - mx)
    yb = (e0 * o0_ref[...] + e1 * o1_ref[...] + e2 * o2_ref[...]) / (e0 + e1 + e2)
    ta = _dot(ya_ref[...], woa_ref[...])
    tb = _dot(yb.astype(BF16), wob_ref[...])
    mixed = gma_ref[...].astype(F32) * ta + gmb_ref[...].astype(F32) * tb
    x1 = x_ref[...] + _dot(mixed.astype(BF16), wout_ref[...])
    ms = jnp.mean(x1 * x1, axis=-1, keepdims=True)
    h = (x1 * lax.rsqrt(ms + NORM_EPS) * g2_ref[...]).astype(BF16)
    acc = x1
    for off, w in _chunks(wup_ref.shape[1], fc):
        u = jnp.maximum(_dot(h, wup_ref[:, off:off + w]), 0.0)
        acc = acc + _dot((u * u).astype(BF16), wdn_ref[off:off + w, :])
    out_ref[...] = acc


def _out_mlp(x, ya, dil, gma, gmb, w_o_a, w_o_b, w_out, norm2_g, w_up, w_down, tm, fc):
    B, S, D = x.shape
    row = lambda w: pl.BlockSpec((None, tm, w), lambda b, m: (b, m, 0))
    full = lambda a: pl.BlockSpec(a.shape, lambda b, m: (0, 0))
    gw = DIL_HPG * HEAD_DIM
    ws = [w.astype(BF16) for w in (w_o_a, w_o_b, w_out)]
    g2 = norm2_g.reshape(1, D)
    wu, wd = w_up.astype(BF16), w_down.astype(BF16)
    return pl.pallas_call(
        functools.partial(_out_mlp_kernel, fc=fc),
        grid=(B, S // tm),
        in_specs=([row(D), row(ya.shape[-1])] + [row(gw)] * 6 + [row(D), row(D)]
                  + [full(w) for w in ws] + [full(g2), full(wu), full(wd)]),
        out_specs=row(D),
        out_shape=jax.ShapeDtypeStruct((B, S, D), F32),
        compiler_params=pltpu.CompilerParams(dimension_semantics=("parallel", "parallel"),
                                             vmem_limit_bytes=VMEM_LIMIT),
        name="out_mlp",
    )(x, ya, dil[0][0], dil[1][0], dil[2][0], dil[0][1], dil[1][1], dil[2][1], gma, gmb, *ws, g2, wu, wd)


def kernel(x, norm1_g, w_in, q_norm_a, k_norm_cmp, k_norm_slc, k_norm_win, cmp_k_pos, cmp_k_w1, cmp_k_w2,
           cmp_v_pos, cmp_v_w1, cmp_v_w2, q_norm_b, k_norm_b, w_o_a, w_o_b, w_out, norm2_g, w_up, w_down):
    depth = w_in.shape[0]
    for i in range(depth):
        (qa, ks, kw, qb, kb, kvc_raw, vb, gma, gmb, gns, vst, vwt) = _inproj(
            x, norm1_g[i], w_in[i], q_norm_a[i], k_norm_slc[i], k_norm_win[i], q_norm_b[i], k_norm_b[i], tm=512)
        kc, vct = _compress(kvc_raw, k_norm_cmp[i], cmp_k_pos[i], cmp_k_w1[i], cmp_k_w2[i],
                           cmp_v_pos[i], cmp_v_w1[i], cmp_v_w2[i])
        ya = _nsa(qa, kc, vct, ks, vst, kw, vwt, gns, tq=128, tk=512)
        dil = [_dilated(qb, kb, vb, g, block_tokens=1024) for g in range(DIL_GROUPS)]
        x = _out_mlp(x, ya, dil, gma, gmb, w_o_a[i], w_o_b[i], w_out[i], norm2_g[i], w_up[i], w_down[i],
                     tm=512, fc=1024)
    return x
```

```python
import functools

import numpy as np
import jax
import jax.numpy as jnp
from jax import lax
from jax.experimental import pallas as pl
from jax.experimental.pallas import tpu as pltpu

HEAD_DIM = 64
HALF = HEAD_DIM // 2
ROPE_THETA = 10000.0
NORM_EPS = 1e-6
NEG = -1e30
NSA_HEADS = 12
NSA_GROUPS = 3
NSA_HPG = NSA_HEADS // NSA_GROUPS
CMP_BLOCK = 32
CMP_STRIDE = 16
SLC_BLOCK = 64
SLC_TOPK = 16
WIN_SIZE = 512
DIL_PAIRS = ((128, 1), (512, 4), (2048, 16))
DIL_GROUPS = len(DIL_PAIRS)
DIL_HPG = 4
DIL_HEADS = DIL_GROUPS * DIL_HPG
DIL_SPAN = 128
LOG2E = float(np.log2(np.e))
ONES_ROWS = 16

LANES = 128
MXU_N = 256
VMEM_LIMIT = 56 * 1024 * 1024

BF16 = jnp.bfloat16
F32 = jnp.float32

ROPE_SEGS = (("qa", 768), ("qb", 768), ("kb", 768), ("ks", 192), ("kw", 192))
PLAIN_SEGS = (("kvc", 384), ("vb", 768))
SIG_SEGS = (("gma", 1024), ("gmb", 1024), ("gns", 384))
ROPE_W = sum(w for _, w in ROPE_SEGS)
PLAIN_W = sum(w for _, w in PLAIN_SEGS)
SIG_W = sum(w for _, w in SIG_SEGS)
HEAD_MAJOR = ("qa", "ks", "kw")


def _dot(a, b):
    return jnp.dot(a, b, preferred_element_type=F32)


def _dot_nt(a, b):
    return lax.dot_general(a, b, (((1,), (1,)), ((), ())), preferred_element_type=F32)


def _chunks(total, width):
    out, off = [], 0
    while off < total:
        w = min(width, total - off)
        out.append((off, w))
        off += w
    return out


def _seg_lookup(segs, col):
    off = 0
    for name, w in segs:
        if col < off + w:
            return name, col - off
        off += w
    raise ValueError(col)


def _inproj_kernel(x_ref, g1_ref, w_ref, wvt_ref, gain_ref, bd_ref, cos_ref, sin_ref,
                   qa_ref, ks_ref, kw_ref, qb_ref, kb_ref,
                   kvc_ref, vb_ref, gma_ref, gmb_ref, gns_ref, vst_ref, vwt_ref):
    outs = dict(qa=qa_ref, ks=ks_ref, kw=kw_ref, qb=qb_ref, kb=kb_ref, kvc=kvc_ref,
                vb=vb_ref, gma=gma_ref, gmb=gmb_ref, gns=gns_ref)

    def emit(segs, col, val):
        name, rel = _seg_lookup(segs, col)
        ref = outs[name]
        if name in HEAD_MAJOR:
            for p in range(LANES // HEAD_DIM):
                nm, r = _seg_lookup(segs, col + p * HEAD_DIM)
                outs[nm][r // HEAD_DIM] = val[:, p * HEAD_DIM:(p + 1) * HEAD_DIM].astype(outs[nm].dtype)
        else:
            nm2, _ = _seg_lookup(segs, col + HEAD_DIM)
            if nm2 != name:
                raise ValueError("unaligned dense segment")
            ref[:, rel:rel + LANES] = val.astype(ref.dtype)

    x = x_ref[...]
    ms = jnp.mean(x * x, axis=-1, keepdims=True)
    h = (x * lax.rsqrt(ms + NORM_EPS) * g1_ref[...]).astype(BF16)

    lane = lax.broadcasted_iota(jnp.int32, (x.shape[0], LANES), 1)
    first_half = (lane % HEAD_DIM) < HALF
    cos = cos_ref[...]
    sin = sin_ref[...]
    bd = bd_ref[...]

    def rope_epilogue(off, w, y):
        msq = _dot((y * y).astype(BF16), bd[:w, :w])
        yn = y * lax.rsqrt(msq + NORM_EPS) * gain_ref[:, off:off + w]
        for u in range(w // LANES):
            v = yn[:, u * LANES:(u + 1) * LANES]
            rot = jnp.where(first_half, pltpu.roll(v, LANES - HALF, 1), pltpu.roll(v, HALF, 1))
            emit(ROPE_SEGS, off + u * LANES, v * cos + rot * sin)

    def plain_epilogue(off, w, y):
        for u in range(w // LANES):
            emit(PLAIN_SEGS, off + u * LANES, y[:, u * LANES:(u + 1) * LANES])

    def sig_epilogue(off, w, y):
        y = jax.nn.sigmoid(y)
        for u in range(w // LANES):
            emit(SIG_SEGS, off + u * LANES, y[:, u * LANES:(u + 1) * LANES])

    work = ([(0, off, w, rope_epilogue) for off, w in _chunks(ROPE_W, MXU_N)]
            + [(ROPE_W, off, w, plain_epilogue) for off, w in _chunks(PLAIN_W, MXU_N)]
            + [(ROPE_W + PLAIN_W, off, w, sig_epilogue) for off, w in _chunks(SIG_W, MXU_N)])
    pending = None
    for base, off, w, epilogue in work:
        y = _dot(h, w_ref[:, base + off:base + off + w])
        if pending is not None:
            pending[0](pending[1], pending[2], pending[3])
        pending = (epilogue, off, w, y)
    pending[0](pending[1], pending[2], pending[3])

    v_t = _dot_nt(wvt_ref[...], h)
    ones = jnp.ones((ONES_ROWS, x.shape[0]), vst_ref.dtype)
    for i, ref in enumerate((vst_ref, vwt_ref)):
        for g in range(NSA_GROUPS):
            r0 = (i * NSA_GROUPS + g) * HEAD_DIM
            ref[g, 0:HEAD_DIM, :] = v_t[r0:r0 + HEAD_DIM].astype(ref.dtype)
            ref[g, HEAD_DIM:HEAD_DIM + ONES_ROWS, :] = ones


def _rope_tables(positions, width):
    inv_freq = np.power(ROPE_THETA, -np.arange(HALF, dtype=np.float64) / HALF)
    ang = np.asarray(positions, np.float64)[:, None] * inv_freq[None, :]
    reps = width // HEAD_DIM
    cos = np.tile(np.concatenate([np.cos(ang), np.cos(ang)], axis=1), (1, reps))
    sin = np.tile(np.concatenate([-np.sin(ang), np.sin(ang)], axis=1), (1, reps))
    return jnp.asarray(cos, F32), jnp.asarray(sin, F32)


def _inproj(x, norm1_g, w_in, q_norm_a, k_norm_slc, k_norm_win, q_norm_b, k_norm_b, tm):
    B, S, D = x.shape
    scale = HEAD_DIM ** -0.5
    sp = np.cumsum((0, 768, 192, 192, 192, 192, 192, 192, 36, 768, 768, 768, 1024, 1024))
    names = ("qa", "kc", "vc", "ks", "vs", "kw", "vw", "gns", "qb", "kb", "vb", "gma", "gmb")
    w_bf = w_in.astype(BF16)
    col = {n: w_bf[:, int(sp[i]):int(sp[i + 1])] for i, n in enumerate(names)}
    gpg = NSA_HPG * 3
    gns = jnp.concatenate(
        [jnp.pad(col["gns"][:, g * gpg:(g + 1) * gpg], ((0, 0), (0, LANES - gpg))) for g in range(NSA_GROUPS)],
        axis=1)
    col["gns"] = gns
    col["kvc"] = jnp.concatenate([col["kc"], col["vc"]], axis=1)
    w_p = jnp.concatenate([col[n] for n, _ in ROPE_SEGS + PLAIN_SEGS + SIG_SEGS], axis=1)
    wvt = jnp.concatenate([col["vs"], col["vw"]], axis=1).T
    ncol = ROPE_W + PLAIN_W + SIG_W
    assert w_p.shape == (D, ncol)

    gain = jnp.concatenate([
        jnp.tile(q_norm_a * (scale * LOG2E), NSA_HEADS), jnp.tile(q_norm_b * (scale * LOG2E), DIL_HEADS),
        jnp.tile(k_norm_b, DIL_HEADS), jnp.tile(k_norm_slc, NSA_GROUPS), jnp.tile(k_norm_win, NSA_GROUPS)
    ]).reshape(1, ROPE_W).astype(F32)
    bd = jnp.asarray(np.kron(np.eye(MXU_N // HEAD_DIM), np.full((HEAD_DIM, HEAD_DIM), 1.0 / HEAD_DIM)), BF16)
    cos, sin = _rope_tables(np.arange(S), LANES)

    def hm(nh, dt):
        return (jax.ShapeDtypeStruct((B, nh, S, HEAD_DIM), dt),
                pl.BlockSpec((None, nh, tm, HEAD_DIM), lambda b, m: (b, 0, m, 0)))

    def vt():
        rows = HEAD_DIM + ONES_ROWS
        return (jax.ShapeDtypeStruct((B, NSA_GROUPS, rows, S), BF16),
                pl.BlockSpec((None, NSA_GROUPS, rows, tm), lambda b, m: (b, 0, 0, m)))

    def dense(w, dt):
        return (jax.ShapeDtypeStruct((B, S, w), dt), pl.BlockSpec((None, tm, w), lambda b, m: (b, m, 0)))

    outs = [hm(NSA_HEADS, BF16), hm(NSA_GROUPS, BF16), hm(NSA_GROUPS, BF16),
            dense(768, F32), dense(768, F32),
            dense(384, F32),
            dense(768, F32),
            dense(1024, BF16), dense(1024, BF16), dense(384, F32),
            vt(), vt()]
    const = lambda b, m: (0, 0)
    return pl.pallas_call(
        _inproj_kernel,
        grid=(B, S // tm),
        in_specs=[pl.BlockSpec((None, tm, D), lambda b, m: (b, m, 0)),
                  pl.BlockSpec((1, D), const),
                  pl.BlockSpec((D, ncol), const),
                  pl.BlockSpec(wvt.shape, const),
                  pl.BlockSpec((1, ROPE_W), const),
                  pl.BlockSpec((MXU_N, MXU_N), const),
                  pl.BlockSpec((tm, LANES), lambda b, m: (m, 0)),
                  pl.BlockSpec((tm, LANES), lambda b, m: (m, 0))],
        out_specs=[o[1] for o in outs],
        out_shape=[o[0] for o in outs],
        compiler_params=pltpu.CompilerParams(dimension_semantics=("parallel", "arbitrary"),
                                             vmem_limit_bytes=VMEM_LIMIT),
        name="inproj",
    )(x, norm1_g.reshape(1, D), w_p, wvt, gain, bd, cos, sin)


def _gelu_tanh(x):
    return 0.5 * x * (1.0 + jnp.tanh(np.sqrt(2.0 / np.pi) * (x + 0.044715 * (x * x * x))))


def _compress_kernel(x0_ref, x1_ref, x2_ref, pk_ref, pv_ref, w1k_ref, w2k_ref, w1v_ref, w2vt_ref,
                     gain_ref, cos_ref, sin_ref, kc_ref, vct_ref):
    nc = x0_ref.shape[0] // CMP_STRIDE
    half = CMP_STRIDE * HEAD_DIM
    per_blk = LANES // HEAD_DIM
    xs = [[x_ref[pl.ds(l, nc, stride=CMP_STRIDE), :] for l in range(CMP_STRIDE)] for x_ref in (x0_ref, x1_ref, x2_ref)]

    def hidden(head, pos_ref, w1_ref):
        blk, sub = divmod(head, per_blk)
        cols = slice(sub * HEAD_DIM, (sub + 1) * HEAD_DIM)
        x = [xs[blk][l][:, cols] for l in range(CMP_STRIDE)]
        xa = jnp.concatenate([x[l] + pos_ref[l:l + 1, :] for l in range(CMP_STRIDE)], axis=1)
        xb = jnp.concatenate([x[l] + pos_ref[CMP_STRIDE + l:CMP_STRIDE + l + 1, :] for l in range(CMP_STRIDE)], axis=1)
        a = _dot(xa.astype(BF16), w1_ref[0:half, :])
        b = _dot(xb.astype(BF16), w1_ref[half:2 * half, :])
        pre = a + pltpu.roll(b, nc - 1, 0)
        return _gelu_tanh(pre).astype(BF16)

    for g in range(NSA_GROUPS):
        kc = _dot(hidden(g, pk_ref, w1k_ref), w2k_ref[...])
        vct = _dot_nt(w2vt_ref[...], hidden(NSA_GROUPS + g, pv_ref, w1v_ref))
        ms = jnp.mean(kc * kc, axis=-1, keepdims=True)
        kn = kc * lax.rsqrt(ms + NORM_EPS) * gain_ref[...]
        rot = jnp.concatenate([kn[:, HALF:], kn[:, :HALF]], axis=-1)
        kc_ref[g] = (kn * cos_ref[...] + rot * sin_ref[...]).astype(kc_ref.dtype)
        vct_ref[g] = vct.astype(vct_ref.dtype)


def _compress(kvc_raw, k_norm_cmp, kpos, kw1, kw2, vpos, vw1, vw2):
    B, S, w = kvc_raw.shape
    G, dh = NSA_GROUPS, HEAD_DIM
    assert w == 2 * G * dh == 3 * LANES
    nc = S // CMP_STRIDE
    cos, sin = _rope_tables(np.arange(nc) * CMP_STRIDE + CMP_BLOCK - 1, dh)
    const = lambda b: (0, 0)
    xblk = lambda j: pl.BlockSpec((None, S, LANES), lambda b: (b, 0, j))
    hid = kw1.shape[1]
    return pl.pallas_call(
        _compress_kernel,
        grid=(B,),
        in_specs=[xblk(0), xblk(1), xblk(2),
                  pl.BlockSpec((CMP_BLOCK, dh), const), pl.BlockSpec((CMP_BLOCK, dh), const),
                  pl.BlockSpec((CMP_BLOCK * dh, hid), const), pl.BlockSpec((hid, dh), const),
                  pl.BlockSpec((CMP_BLOCK * dh, hid), const), pl.BlockSpec((dh, hid), const),
                  pl.BlockSpec((1, dh), const), pl.BlockSpec((nc, dh), const), pl.BlockSpec((nc, dh), const)],
        out_specs=[pl.BlockSpec((None, G, nc, dh), lambda b: (b, 0, 0, 0)),
                   pl.BlockSpec((None, G, dh, nc), lambda b: (b, 0, 0, 0))],
        out_shape=[jax.ShapeDtypeStruct((B, G, nc, dh), BF16), jax.ShapeDtypeStruct((B, G, dh, nc), BF16)],
        compiler_params=pltpu.CompilerParams(dimension_semantics=("parallel",), vmem_limit_bytes=VMEM_LIMIT),
        name="compress",
    )(kvc_raw, kvc_raw, kvc_raw, kpos, vpos, kw1.astype(BF16), kw2.astype(BF16), vw1.astype(BF16),
      vw2.T.astype(BF16), k_norm_cmp.reshape(1, dh).astype(F32), cos, sin)


def _nsa_kernel(*refs, nsub, **static):
    for sub in range(nsub):
        _nsa_tile(*refs, sub=sub, nsub=nsub, **static)


def _nsa_tile(q_ref, kc_ref, vct_ref, ks_ref, vst_ref, kw_ref, vwt_ref, gate_ref, ovt_ref,
              o_ref, bias_ref, s_ref, *, sub, nsub, tq, tk, n_cmp, n_slc, seq):
    R = NSA_HPG
    qt = pl.program_id(2) * nsub + sub
    rows = slice(sub * tq, (sub + 1) * tq)
    t0 = qt * tq
    t0a = pl.multiple_of(t0, tq)
    q2 = q_ref[:, rows, :].reshape(R * tq, HEAD_DIM)
    ncp = kc_ref.shape[0]
    t_lane = t0 + lax.broadcasted_iota(jnp.int32, (1, tq), 1)
    bpt = tk // SLC_BLOCK
    last_tile = seq // tk - 1
    span = min(WIN_SIZE + tq, seq)
    start = pl.multiple_of(jnp.maximum(t0 - WIN_SIZE, 0), tq)

    def heads(a):
        return [a[:, r * tq:(r + 1) * tq] for r in range(R)]

    def qk(kt, slot):
        k0 = pl.multiple_of(jnp.minimum(kt, last_tile) * tk, tk)
        s_ref[slot] = _dot_nt(ks_ref[pl.ds(k0, tk), :], q2)

    s_c = _dot_nt(kc_ref[...], q2)
    s_d = _dot_nt(ks_ref[pl.ds(t0a, tq), :], q2)
    s_w = _dot_nt(kw_ref[pl.ds(start, span), :], q2)
    qk(0, 0)

    c_idx = lax.broadcasted_iota(jnp.int32, (ncp, tq), 0)
    cmask = ((c_idx * CMP_STRIDE + (CMP_BLOCK - 1)) <= t_lane) & (c_idx < n_cmp)
    ps = []
    for sr in heads(s_c):
        sr = jnp.where(cmask, sr, NEG)
        m = jnp.max(sr, axis=0, keepdims=True)
        e = jnp.exp2(sr - m)
        den = jnp.sum(e, axis=0, keepdims=True)
        ps.append(e * jnp.where(m > 0.5 * NEG, 1.0 / den, 0.0))
    o_cmp = _dot(vct_ref[...], jnp.concatenate(ps, axis=1).astype(BF16))

    psum = ps[0]
    for r in range(1, R):
        psum = psum + ps[r]
    p_hi = psum.astype(BF16)
    p_lo = (psum - p_hi.astype(F32)).astype(BF16)
    nb = -(-n_slc // 8) * 8
    imp = (_dot(ovt_ref[...], p_hi) + _dot(ovt_ref[...], p_lo))[:nb]
    n_b = lax.broadcasted_iota(jnp.int32, (nb, tq), 0)
    cur = (t0 + lax.broadcasted_iota(jnp.int32, (nb, tq), 1)) // SLC_BLOCK
    forced = (n_b == 0) | (n_b == cur) | (n_b == cur - 1)
    visible = n_b <= cur

    top_n = min(SLC_TOPK, n_slc)
    n_f = n_b.astype(F32)
    work = jnp.where(forced, -jnp.inf, jnp.where(visible, imp, -jnp.inf))
    picked = jnp.zeros((nb, tq), F32)
    for _ in range(max(top_n - 3, 0)):
        mx = jnp.max(work, axis=0, keepdims=True)
        first = jnp.min(jnp.where(work == mx, n_f, float(nb)), axis=0, keepdims=True)
        hit = n_f == jnp.where(mx > -jnp.inf, first, -1.0)
        picked = jnp.where(hit, 1.0, picked)
        work = jnp.where(hit, -jnp.inf, work)
    chosen = jnp.where(forced, 1.0, jnp.where(cur < top_n, 1.0, picked))
    live = jnp.where(visible, jnp.where(n_b < t0 // SLC_BLOCK, chosen, 0.0), 0.0)
    bias_ref[:nb] = jnp.where(live > 0.5, 0.0, NEG)
    if nb < LANES:
        bias_ref[nb:] = jnp.full((LANES - nb, tq), NEG, F32)

    tri = lax.broadcasted_iota(jnp.int32, (tq, tq), 0) <= lax.broadcasted_iota(jnp.int32, (tq, tq), 1)
    m_d, p_d = [], []
    for sr in heads(s_d):
        sr = jnp.where(tri, sr, NEG)
        m = jnp.max(sr, axis=0, keepdims=True)
        m_d.append(m)
        p_d.append(jnp.exp2(sr - m).astype(BF16))
    carry = (jnp.concatenate(m_d, axis=1), _dot(vst_ref[:, pl.ds(t0a, tq)], jnp.concatenate(p_d, axis=1)))

    diff = t_lane - (start + lax.broadcasted_iota(jnp.int32, (span, tq), 0))
    wmask = (diff >= 0) & (diff < WIN_SIZE)
    pw = []
    for sr in heads(s_w):
        sr = jnp.where(wmask, sr, NEG)
        m = jnp.max(sr, axis=0, keepdims=True)
        pw.append(jnp.exp2(sr - m).astype(BF16))
    acc_w = _dot(vwt_ref[:, pl.ds(start, span)], jnp.concatenate(pw, axis=1))
    o_win = acc_w[:HEAD_DIM] / acc_w[HEAD_DIM:HEAD_DIM + 1]

    def update(kt, slot, carry):
        m_old, acc = carry
        k0 = pl.multiple_of(kt * tk, tk)
        brows = [bias_ref[pl.ds(kt * bpt + j, 1), :] for j in range(bpt)]
        m_new, p_all, alphas = [], [], []
        for r in range(R):
            cols = slice(r * tq, (r + 1) * tq)
            mo = m_old[:, cols]
            m8 = None
            for j in range(bpt):
                blk = s_ref[slot, j * SLC_BLOCK:(j + 1) * SLC_BLOCK, cols]
                b8 = jnp.max(blk.reshape(SLC_BLOCK // 8, 8, tq), axis=0) + brows[j]
                m8 = b8 if m8 is None else jnp.maximum(m8, b8)
            mn = jnp.maximum(mo, jnp.max(m8, axis=0, keepdims=True))
            p_all.append(jnp.concatenate(
                [jnp.exp2(s_ref[slot, j * SLC_BLOCK:(j + 1) * SLC_BLOCK, cols] + (brows[j] - mn)).astype(BF16)
                 for j in range(bpt)], axis=0))
            alphas.append(jnp.exp2(mo - mn))
            m_new.append(mn)
        pv = _dot(vst_ref[:, pl.ds(k0, tk)], jnp.concatenate(p_all, axis=1))
        return jnp.concatenate(m_new, axis=1), jnp.concatenate(alphas, axis=1) * acc + pv

    def pair_body(ii, carry):
        a = 2 * ii
        qk(a + 1, 1)
        carry = update(a, 0, carry)
        qk(a + 2, 0)
        return update(a + 1, 1, carry)

    n_main = (t0 + tk - 1) // tk
    _, acc = lax.fori_loop(0, (n_main + 1) // 2, pair_body, carry)
    o_slc = acc[:HEAD_DIM] / acc[HEAD_DIM:HEAD_DIM + 1]

    gate_t = gate_ref[rows, :].T
    ys = []
    for r in range(R):
        cols = slice(r * tq, (r + 1) * tq)
        ys.append(gate_t[3 * r:3 * r + 1] * o_cmp[:, cols] + gate_t[3 * r + 1:3 * r + 2] * o_slc[:, cols]
                  + gate_t[3 * r + 2:3 * r + 3] * o_win[:, cols])
    o_ref[rows, :] = jnp.concatenate(ys, axis=0).T.astype(o_ref.dtype)


def _overlap_t(ncp, n_cmp, n_slc):
    cs = np.arange(ncp)[None, :] * CMP_STRIDE
    ss = np.arange(LANES)[:, None] * SLC_BLOCK
    ov = np.clip(np.minimum(cs + CMP_BLOCK, ss + SLC_BLOCK) - np.maximum(cs, ss), 0, None) / CMP_BLOCK
    ov = ov * (np.arange(ncp)[None, :] < n_cmp) * (np.arange(LANES)[:, None] < n_slc)
    return jnp.asarray(ov, BF16)


def _nsa(qa, kc, vct, ks, vst, kw, vwt, gns, tq, tk, nsub):
    B, H, S, dh = qa.shape
    ncp = kc.shape[2]
    n_cmp = (S - CMP_BLOCK) // CMP_STRIDE + 1
    n_slc = S // SLC_BLOCK
    assert n_slc <= LANES and S % (2 * tk) == 0 and tk % SLC_BLOCK == 0 and tq == 2 * SLC_BLOCK and S % (nsub * tq) == 0
    ovt = _overlap_t(ncp, n_cmp, n_slc)
    k_c = pl.BlockSpec((None, None, ncp, dh), lambda b, g, t: (b, g, 0, 0))
    v_c = pl.BlockSpec((None, None, dh, ncp), lambda b, g, t: (b, g, 0, 0))
    k_s = pl.BlockSpec((None, None, S, dh), lambda b, g, t: (b, g, 0, 0))
    v_s = pl.BlockSpec((None, None, vst.shape[2], S), lambda b, g, t: (b, g, 0, 0))
    v_w = pl.BlockSpec((None, None, vwt.shape[2], S), lambda b, g, t: (b, g, 0, 0))
    const = lambda b, g, t: (0, 0)
    kern = functools.partial(_nsa_kernel, tq=tq, tk=tk, n_cmp=n_cmp, n_slc=n_slc, seq=S, nsub=nsub)
    return pl.pallas_call(
        kern,
        grid=(B, NSA_GROUPS, S // (nsub * tq)),
        in_specs=[pl.BlockSpec((None, NSA_HPG, nsub * tq, dh), lambda b, g, t: (b, g, t, 0)),
                  k_c, v_c, k_s, v_s, k_s, v_w,
                  pl.BlockSpec((None, nsub * tq, LANES), lambda b, g, t: (b, t, g)),
                  pl.BlockSpec((LANES, ncp), const)],
        out_specs=pl.BlockSpec((None, nsub * tq, NSA_HPG * dh), lambda b, g, t: (b, t, g)),
        out_shape=jax.ShapeDtypeStruct((B, S, H * dh), BF16),
        scratch_shapes=[pltpu.VMEM((LANES, tq), F32),
                        pltpu.VMEM((2, tk, NSA_HPG * tq), F32)],
        compiler_params=pltpu.CompilerParams(dimension_semantics=("parallel", "parallel", "arbitrary"),
                                             vmem_limit_bytes=VMEM_LIMIT),
        name="nsa",
    )(qa, kc, vct, ks, vst, kw, vwt, gns, ovt)


def _dilated_kernel(q_ref, kp_ref, kc_ref, vp_ref, vc_ref, o_ref, lse_ref, vt_ref, *, d, tq, nq):
    span = 2 * tq
    nh = LANES // HEAD_DIM
    row = lax.broadcasted_iota(jnp.int32, (span, tq), 0)
    lane = lax.broadcasted_iota(jnp.int32, (span, tq), 1)
    diff = (lane + tq) - row
    band = (diff >= 0) & (diff <= DIL_SPAN)
    has_prev = pl.program_id(1) > 0
    first = band & ((row >= tq) | has_prev)
    head_of_lane = lax.broadcasted_iota(jnp.int32, (tq, LANES), 1) // HEAD_DIM
    vt_ref[:, LANES:, :] = jnp.ones((nq * d, ONES_ROWS, span), vt_ref.dtype)

    def scores(j, r):
        rows = pl.ds(r + j * tq * d, tq, stride=d)
        qb = q_ref[rows, :]
        if j == 0:
            kb = jnp.concatenate([kp_ref[pl.ds(r, tq, stride=d), :], kc_ref[rows, :]], axis=0)
            vb = jnp.concatenate([vp_ref[pl.ds(r, tq, stride=d), :], vc_ref[rows, :]], axis=0)
        else:
            kv_rows = pl.ds(r + (j - 1) * tq * d, span, stride=d)
            kb, vb = kc_ref[kv_rows, :], vc_ref[kv_rows, :]
        vt_ref[j * d + r, 0:LANES, :] = vb.T.astype(BF16)
        q_bd = jnp.concatenate([jnp.where(head_of_lane == h, qb, 0.0) for h in range(nh)], axis=0).astype(BF16)
        return _dot_nt(kb.astype(BF16), q_bd)

    def finish(j, r, s):
        mask = first if j == 0 else band
        ms, ps = [], []
        for h in range(nh):
            sh = jnp.where(mask, s[:, h * tq:(h + 1) * tq], NEG)
            m = jnp.max(sh, axis=0, keepdims=True)
            ms.append(m)
            ps.append(jnp.exp2(sh - m).astype(BF16))
        pv = _dot(vt_ref[j * d + r], jnp.concatenate(ps, axis=1))
        o_t, l_t = [], []
        for h in range(nh):
            den = pv[LANES:LANES + 1, h * tq:(h + 1) * tq]
            o_t.append(pv[h * HEAD_DIM:(h + 1) * HEAD_DIM, h * tq:(h + 1) * tq] / den)
            l_t.append(jnp.broadcast_to(ms[h] + jnp.log2(den), (HEAD_DIM, tq)))
        rows = pl.ds(r + j * tq * d, tq, stride=d)
        o_ref[rows, :] = jnp.concatenate(o_t, axis=0).T
        lse_ref[rows, :] = jnp.concatenate(l_t, axis=0).T

    pending = None
    for j in range(nq):
        for r in range(d):
            s = scores(j, r)
            if pending is not None:
                finish(*pending)
            pending = (j, r, s)
    finish(*pending)


def _dilated(qb, kb, vb, g, block_tokens):
    B, S, _ = qb.shape
    d = DIL_PAIRS[g][1]
    tq = min(LANES, S // d)
    nq = max(block_tokens // (d * tq), 1)
    assert DIL_PAIRS[g][0] // d == DIL_SPAN and DIL_SPAN <= tq and S % (nq * d * tq) == 0
    gw = DIL_HPG * HEAD_DIM
    nb = gw // LANES
    cur = pl.BlockSpec((None, nq * d * tq, LANES), lambda b, i, c: (b, i, g * nb + c))
    prev = pl.BlockSpec((None, d * tq, LANES), lambda b, i, c: (b, jnp.maximum(i * nq - 1, 0), g * nb + c))
    out_blk = pl.BlockSpec((None, nq * d * tq, LANES), lambda b, i, c: (b, i, c))
    return pl.pallas_call(
        functools.partial(_dilated_kernel, d=d, tq=tq, nq=nq),
        grid=(B, S // (nq * d * tq), nb),
        in_specs=[cur, prev, cur, prev, cur],
        out_specs=[out_blk, out_blk],
        out_shape=[jax.ShapeDtypeStruct((B, S, gw), F32)] * 2,
        scratch_shapes=[pltpu.VMEM((nq * d, LANES + ONES_ROWS, 2 * tq), BF16)],
        compiler_params=pltpu.CompilerParams(dimension_semantics=("parallel", "arbitrary", "arbitrary"),
                                             vmem_limit_bytes=VMEM_LIMIT),
        name=f"dilated{g}",
    )(qb, kb, kb, vb, vb)


def _out_mlp_kernel(x_ref, ya_ref, o0_ref, o1_ref, o2_ref, l0_ref, l1_ref, l2_ref, gma_ref, gmb_ref,
                    woa_ref, wob_ref, wout_ref, g2_ref, wup_ref, wdn_ref, out_ref, *, fc):
    l0, l1, l2 = l0_ref[...], l1_ref[...], l2_ref[...]
    mx = jnp.maximum(jnp.maximum(l0, l1), l2)
    e0, e1, e2 = jnp.exp2(l0 - mx), jnp.exp2(l1 - mx), jnp.exp2(l2 - mx)
    yb = (e0 * o0_ref[...] + e1 * o1_ref[...] + e2 * o2_ref[...]) / (e0 + e1 + e2)
    ta = _dot(ya_ref[...], woa_ref[...])
    tb = _dot(yb.astype(BF16), wob_ref[...])
    mixed = gma_ref[...].astype(F32) * ta + gmb_ref[...].astype(F32) * tb
    x1 = x_ref[...] + _dot(mixed.astype(BF16), wout_ref[...])
    ms = jnp.mean(x1 * x1, axis=-1, keepdims=True)
    h = (x1 * lax.rsqrt(ms + NORM_EPS) * g2_ref[...]).astype(BF16)
    acc = x1
    for off, w in _chunks(wup_ref.shape[1], fc):
        u = jnp.maximum(_dot(h, wup_ref[:, off:off + w]), 0.0)
        acc = acc + _dot((u * u).astype(BF16), wdn_ref[off:off + w, :])
    out_ref[...] = acc


def _out_mlp(x, ya, dil, gma, gmb, w_o_a, w_o_b, w_out, norm2_g, w_up, w_down, tm, fc):
    B, S, D = x.shape
    row = lambda w: pl.BlockSpec((None, tm, w), lambda b, m: (b, m, 0))
    full = lambda a: pl.BlockSpec(a.shape, lambda b, m: (0, 0))
    gw = DIL_HPG * HEAD_DIM
    ws = [w.astype(BF16) for w in (w_o_a, w_o_b, w_out)]
    g2 = norm2_g.reshape(1, D)
    wu, wd = w_up.astype(BF16), w_down.astype(BF16)
    return pl.pallas_call(
        functools.partial(_out_mlp_kernel, fc=fc),
        grid=(B, S // tm),
        in_specs=([row(D), row(ya.shape[-1])] + [row(gw)] * 6 + [row(D), row(D)]
                  + [full(w) for w in ws] + [full(g2), full(wu), full(wd)]),
        out_specs=row(D),
        out_shape=jax.ShapeDtypeStruct((B, S, D), F32),
        compiler_params=pltpu.CompilerParams(dimension_semantics=("parallel", "parallel"),
                                             vmem_limit_bytes=VMEM_LIMIT),
        name="out_mlp",
    )(x, ya, dil[0][0], dil[1][0], dil[2][0], dil[0][1], dil[1][1], dil[2][1], gma, gmb, *ws, g2, wu, wd)


def kernel(x, norm1_g, w_in, q_norm_a, k_norm_cmp, k_norm_slc, k_norm_win, cmp_k_pos, cmp_k_w1, cmp_k_w2,
           cmp_v_pos, cmp_v_w1, cmp_v_w2, q_norm_b, k_norm_b, w_o_a, w_o_b, w_out, norm2_g, w_up, w_down):
    depth = w_in.shape[0]
    for i in range(depth):
        (qa, ks, kw, qb, kb, kvc_raw, vb, gma, gmb, gns, vst, vwt) = _inproj(
            x, norm1_g[i], w_in[i], q_norm_a[i], k_norm_slc[i], k_norm_win[i], q_norm_b[i], k_norm_b[i], tm=512)
        kc, vct = _compress(kvc_raw, k_norm_cmp[i], cmp_k_pos[i], cmp_k_w1[i], cmp_k_w2[i],
                           cmp_v_pos[i], cmp_v_w1[i], cmp_v_w2[i])
        ya = _nsa(qa, kc, vct, ks, vst, kw, vwt, gns, tq=128, tk=512, nsub=4)
        dil = [_dilated(qb, kb, vb, g, block_tokens=1024) for g in range(DIL_GROUPS)]
        x = _out_mlp(x, ya, dil, gma, gmb, w_o_a[i], w_o_b[i], w_out[i], norm2_g[i], w_up[i], w_down[i],
                     tm=512, fc=1024)
    return x
```

```python
import functools

import numpy as np
import jax
import jax.numpy as jnp
from jax import lax
from jax.experimental import pallas as pl
from jax.experimental.pallas import tpu as pltpu

HEAD_DIM = 64
HALF = HEAD_DIM // 2
ROPE_THETA = 10000.0
NORM_EPS = 1e-6
NEG = -1e30
NSA_HEADS = 12
NSA_GROUPS = 3
NSA_HPG = NSA_HEADS // NSA_GROUPS
CMP_BLOCK = 32
CMP_STRIDE = 16
SLC_BLOCK = 64
SLC_TOPK = 16
WIN_SIZE = 512
DIL_PAIRS = ((128, 1), (512, 4), (2048, 16))
DIL_GROUPS = len(DIL_PAIRS)
DIL_HPG = 4
DIL_HEADS = DIL_GROUPS * DIL_HPG
DIL_SPAN = 128
LOG2E = float(np.log2(np.e))
ONES_ROWS = 16

LANES = 128
MXU_N = 256
VMEM_LIMIT = 56 * 1024 * 1024

BF16 = jnp.bfloat16
F32 = jnp.float32

ROPE_SEGS = (("qa", 768), ("qb", 768), ("kb", 768), ("ks", 192), ("kw", 192))
PLAIN_SEGS = (("kvc", 384), ("vb", 768))
SIG_SEGS = (("gma", 1024), ("gmb", 1024), ("gns", 384))
ROPE_W = sum(w for _, w in ROPE_SEGS)
PLAIN_W = sum(w for _, w in PLAIN_SEGS)
SIG_W = sum(w for _, w in SIG_SEGS)
HEAD_MAJOR = ("qa", "ks", "kw")


def _dot(a, b):
    return jnp.dot(a, b, preferred_element_type=F32)


def _dot_nt(a, b):
    return lax.dot_general(a, b, (((1,), (1,)), ((), ())), preferred_element_type=F32)


def _chunks(total, width):
    out, off = [], 0
    while off < total:
        w = min(width, total - off)
        out.append((off, w))
        off += w
    return out


def _seg_lookup(segs, col):
    off = 0
    for name, w in segs:
        if col < off + w:
            return name, col - off
        off += w
    raise ValueError(col)


def _inproj_kernel(x_ref, g1_ref, w_ref, wvt_ref, gain_ref, bd_ref, cos_ref, sin_ref,
                   qa_ref, ks_ref, kw_ref, qb_ref, kb_ref,
                   kvc_ref, vb_ref, gma_ref, gmb_ref, gns_ref, vst_ref, vwt_ref):
    outs = dict(qa=qa_ref, ks=ks_ref, kw=kw_ref, qb=qb_ref, kb=kb_ref, kvc=kvc_ref,
                vb=vb_ref, gma=gma_ref, gmb=gmb_ref, gns=gns_ref)

    def emit(segs, col, val):
        name, rel = _seg_lookup(segs, col)
        ref = outs[name]
        if name in HEAD_MAJOR:
            for p in range(LANES // HEAD_DIM):
                nm, r = _seg_lookup(segs, col + p * HEAD_DIM)
                outs[nm][r // HEAD_DIM] = val[:, p * HEAD_DIM:(p + 1) * HEAD_DIM].astype(outs[nm].dtype)
        else:
            nm2, _ = _seg_lookup(segs, col + HEAD_DIM)
            if nm2 != name:
                raise ValueError("unaligned dense segment")
            ref[:, rel:rel + LANES] = val.astype(ref.dtype)

    x = x_ref[...]
    ms = jnp.mean(x * x, axis=-1, keepdims=True)
    h = (x * lax.rsqrt(ms + NORM_EPS) * g1_ref[...]).astype(BF16)

    lane = lax.broadcasted_iota(jnp.int32, (x.shape[0], LANES), 1)
    first_half = (lane % HEAD_DIM) < HALF
    cos = cos_ref[...]
    sin = sin_ref[...]
    bd = bd_ref[...]

    def rope_epilogue(off, w, y):
        msq = _dot((y * y).astype(BF16), bd[:w, :w])
        yn = y * lax.rsqrt(msq + NORM_EPS) * gain_ref[:, off:off + w]
        for u in range(w // LANES):
            v = yn[:, u * LANES:(u + 1) * LANES]
            rot = jnp.where(first_half, pltpu.roll(v, LANES - HALF, 1), pltpu.roll(v, HALF, 1))
            emit(ROPE_SEGS, off + u * LANES, v * cos + rot * sin)

    def plain_epilogue(off, w, y):
        for u in range(w // LANES):
            emit(PLAIN_SEGS, off + u * LANES, y[:, u * LANES:(u + 1) * LANES])

    def sig_epilogue(off, w, y):
        y = jax.nn.sigmoid(y)
        for u in range(w // LANES):
            emit(SIG_SEGS, off + u * LANES, y[:, u * LANES:(u + 1) * LANES])

    work = ([(0, off, w, rope_epilogue) for off, w in _chunks(ROPE_W, MXU_N)]
            + [(ROPE_W, off, w, plain_epilogue) for off, w in _chunks(PLAIN_W, MXU_N)]
            + [(ROPE_W + PLAIN_W, off, w, sig_epilogue) for off, w in _chunks(SIG_W, MXU_N)])
    pending = None
    for base, off, w, epilogue in work:
        y = _dot(h, w_ref[:, base + off:base + off + w])
        if pending is not None:
            pending[0](pending[1], pending[2], pending[3])
        pending = (epilogue, off, w, y)
    pending[0](pending[1], pending[2], pending[3])

    v_t = _dot_nt(wvt_ref[...], h)
    ones = jnp.ones((ONES_ROWS, x.shape[0]), vst_ref.dtype)
    for i, ref in enumerate((vst_ref, vwt_ref)):
        for g in range(NSA_GROUPS):
            r0 = (i * NSA_GROUPS + g) * HEAD_DIM
            ref[g, 0:HEAD_DIM, :] = v_t[r0:r0 + HEAD_DIM].astype(ref.dtype)
            ref[g, HEAD_DIM:HEAD_DIM + ONES_ROWS, :] = ones


def _rope_tables(positions, width):
    inv_freq = np.power(ROPE_THETA, -np.arange(HALF, dtype=np.float64) / HALF)
    ang = np.asarray(positions, np.float64)[:, None] * inv_freq[None, :]
    reps = width // HEAD_DIM
    cos = np.tile(np.concatenate([np.cos(ang), np.cos(ang)], axis=1), (1, reps))
    sin = np.tile(np.concatenate([-np.sin(ang), np.sin(ang)], axis=1), (1, reps))
    return jnp.asarray(cos, F32), jnp.asarray(sin, F32)


def _inproj(x, norm1_g, w_in, q_norm_a, k_norm_slc, k_norm_win, q_norm_b, k_norm_b, tm):
    B, S, D = x.shape
    scale = HEAD_DIM ** -0.5
    sp = np.cumsum((0, 768, 192, 192, 192, 192, 192, 192, 36, 768, 768, 768, 1024, 1024))
    names = ("qa", "kc", "vc", "ks", "vs", "kw", "vw", "gns", "qb", "kb", "vb", "gma", "gmb")
    w_bf = w_in.astype(BF16)
    col = {n: w_bf[:, int(sp[i]):int(sp[i + 1])] for i, n in enumerate(names)}
    gpg = NSA_HPG * 3
    gns = jnp.concatenate(
        [jnp.pad(col["gns"][:, g * gpg:(g + 1) * gpg], ((0, 0), (0, LANES - gpg))) for g in range(NSA_GROUPS)],
        axis=1)
    col["gns"] = gns
    col["kvc"] = jnp.concatenate([col["kc"], col["vc"]], axis=1)
    w_p = jnp.concatenate([col[n] for n, _ in ROPE_SEGS + PLAIN_SEGS + SIG_SEGS], axis=1)
    wvt = jnp.concatenate([col["vs"], col["vw"]], axis=1).T
    ncol = ROPE_W + PLAIN_W + SIG_W
    assert w_p.shape == (D, ncol)

    gain = jnp.concatenate([
        jnp.tile(q_norm_a * (scale * LOG2E), NSA_HEADS), jnp.tile(q_norm_b * (scale * LOG2E), DIL_HEADS),
        jnp.tile(k_norm_b, DIL_HEADS), jnp.tile(k_norm_slc, NSA_GROUPS), jnp.tile(k_norm_win, NSA_GROUPS)
    ]).reshape(1, ROPE_W).astype(F32)
    bd = jnp.asarray(np.kron(np.eye(MXU_N // HEAD_DIM), np.full((HEAD_DIM, HEAD_DIM), 1.0 / HEAD_DIM)), BF16)
    cos, sin = _rope_tables(np.arange(S), LANES)

    def hm(nh, dt):
        return (jax.ShapeDtypeStruct((B, nh, S, HEAD_DIM), dt),
                pl.BlockSpec((None, nh, tm, HEAD_DIM), lambda b, m: (b, 0, m, 0)))

    def vt():
        rows = HEAD_DIM + ONES_ROWS
        return (jax.ShapeDtypeStruct((B, NSA_GROUPS, rows, S), BF16),
                pl.BlockSpec((None, NSA_GROUPS, rows, tm), lambda b, m: (b, 0, 0, m)))

    def dense(w, dt):
        return (jax.ShapeDtypeStruct((B, S, w), dt), pl.BlockSpec((None, tm, w), lambda b, m: (b, m, 0)))

    outs = [hm(NSA_HEADS, BF16), hm(NSA_GROUPS, BF16), hm(NSA_GROUPS, BF16),
            dense(768, F32), dense(768, F32),
            dense(384, F32),
            dense(768, F32),
            dense(1024, BF16), dense(1024, BF16), dense(384, F32),
            vt(), vt()]
    const = lambda b, m: (0, 0)
    return pl.pallas_call(
        _inproj_kernel,
        grid=(B, S // tm),
        in_specs=[pl.BlockSpec((None, tm, D), lambda b, m: (b, m, 0)),
                  pl.BlockSpec((1, D), const),
                  pl.BlockSpec((D, ncol), const),
                  pl.BlockSpec(wvt.shape, const),
                  pl.BlockSpec((1, ROPE_W), const),
                  pl.BlockSpec((MXU_N, MXU_N), const),
                  pl.BlockSpec((tm, LANES), lambda b, m: (m, 0)),
                  pl.BlockSpec((tm, LANES), lambda b, m: (m, 0))],
        out_specs=[o[1] for o in outs],
        out_shape=[o[0] for o in outs],
        compiler_params=pltpu.CompilerParams(dimension_semantics=("parallel", "arbitrary"),
                                             vmem_limit_bytes=VMEM_LIMIT),
        name="inproj",
    )(x, norm1_g.reshape(1, D), w_p, wvt, gain, bd, cos, sin)


def _gelu_tanh(x):
    return 0.5 * x * (1.0 + jnp.tanh(np.sqrt(2.0 / np.pi) * (x + 0.044715 * (x * x * x))))


def _compress_kernel(x0_ref, x1_ref, x2_ref, pk_ref, pv_ref, w1k_ref, w2k_ref, w1v_ref, w2vt_ref,
                     gain_ref, cos_ref, sin_ref, kc_ref, vct_ref):
    nc = x0_ref.shape[0] // CMP_STRIDE
    half = CMP_STRIDE * HEAD_DIM
    per_blk = LANES // HEAD_DIM
    xs = [[x_ref[pl.ds(l, nc, stride=CMP_STRIDE), :] for l in range(CMP_STRIDE)] for x_ref in (x0_ref, x1_ref, x2_ref)]

    def hidden(head, pos_ref, w1_ref):
        blk, sub = divmod(head, per_blk)
        cols = slice(sub * HEAD_DIM, (sub + 1) * HEAD_DIM)
        x = [xs[blk][l][:, cols] for l in range(CMP_STRIDE)]
        xa = jnp.concatenate([x[l] + pos_ref[l:l + 1, :] for l in range(CMP_STRIDE)], axis=1)
        xb = jnp.concatenate([x[l] + pos_ref[CMP_STRIDE + l:CMP_STRIDE + l + 1, :] for l in range(CMP_STRIDE)], axis=1)
        a = _dot(xa.astype(BF16), w1_ref[0:half, :])
        b = _dot(xb.astype(BF16), w1_ref[half:2 * half, :])
        pre = a + pltpu.roll(b, nc - 1, 0)
        return _gelu_tanh(pre).astype(BF16)

    for g in range(NSA_GROUPS):
        kc = _dot(hidden(g, pk_ref, w1k_ref), w2k_ref[...])
        vct = _dot_nt(w2vt_ref[...], hidden(NSA_GROUPS + g, pv_ref, w1v_ref))
        ms = jnp.mean(kc * kc, axis=-1, keepdims=True)
        kn = kc * lax.rsqrt(ms + NORM_EPS) * gain_ref[...]
        rot = jnp.concatenate([kn[:, HALF:], kn[:, :HALF]], axis=-1)
        kc_ref[g] = (kn * cos_ref[...] + rot * sin_ref[...]).astype(kc_ref.dtype)
        vct_ref[g] = vct.astype(vct_ref.dtype)


def _compress(kvc_raw, k_norm_cmp, kpos, kw1, kw2, vpos, vw1, vw2):
    B, S, w = kvc_raw.shape
    G, dh = NSA_GROUPS, HEAD_DIM
    assert w == 2 * G * dh == 3 * LANES
    nc = S // CMP_STRIDE
    cos, sin = _rope_tables(np.arange(nc) * CMP_STRIDE + CMP_BLOCK - 1, dh)
    const = lambda b: (0, 0)
    xblk = lambda j: pl.BlockSpec((None, S, LANES), lambda b: (b, 0, j))
    hid = kw1.shape[1]
    return pl.pallas_call(
        _compress_kernel,
        grid=(B,),
        in_specs=[xblk(0), xblk(1), xblk(2),
                  pl.BlockSpec((CMP_BLOCK, dh), const), pl.BlockSpec((CMP_BLOCK, dh), const),
                  pl.BlockSpec((CMP_BLOCK * dh, hid), const), pl.BlockSpec((hid, dh), const),
                  pl.BlockSpec((CMP_BLOCK * dh, hid), const), pl.BlockSpec((dh, hid), const),
                  pl.BlockSpec((1, dh), const), pl.BlockSpec((nc, dh), const), pl.BlockSpec((nc, dh), const)],
        out_specs=[pl.BlockSpec((None, G, nc, dh), lambda b: (b, 0, 0, 0)),
                   pl.BlockSpec((None, G, dh, nc), lambda b: (b, 0, 0, 0))],
        out_shape=[jax.ShapeDtypeStruct((B, G, nc, dh), BF16), jax.ShapeDtypeStruct((B, G, dh, nc), BF16)],
        compiler_params=pltpu.CompilerParams(dimension_semantics=("parallel",), vmem_limit_bytes=VMEM_LIMIT),
        name="compress",
    )(kvc_raw, kvc_raw, kvc_raw, kpos, vpos, kw1.astype(BF16), kw2.astype(BF16), vw1.astype(BF16),
      vw2.T.astype(BF16), k_norm_cmp.reshape(1, dh).astype(F32), cos, sin)


def _nsa_front(q_ref, kc_ref, vct_ref, ks_ref, vst_ref, kw_ref, vwt_ref, ovt_ref, bias_ref,
               *, sub, nsub, tq, n_cmp, n_slc, seq):
    R = NSA_HPG
    qt = pl.program_id(2) * nsub + sub
    t0 = qt * tq
    t0a = pl.multiple_of(t0, tq)
    q2 = q_ref[:, sub * tq:(sub + 1) * tq, :].reshape(R * tq, HEAD_DIM)
    ncp = kc_ref.shape[0]
    t_lane = t0 + lax.broadcasted_iota(jnp.int32, (1, tq), 1)
    span = min(WIN_SIZE + tq, seq)
    start = pl.multiple_of(jnp.maximum(t0 - WIN_SIZE, 0), tq)

    def heads(a):
        return [a[:, r * tq:(r + 1) * tq] for r in range(R)]

    s_c = _dot_nt(kc_ref[...], q2)
    s_d = _dot_nt(ks_ref[pl.ds(t0a, tq), :], q2)
    s_w = _dot_nt(kw_ref[pl.ds(start, span), :], q2)

    c_idx = lax.broadcasted_iota(jnp.int32, (ncp, tq), 0)
    cmask = ((c_idx * CMP_STRIDE + (CMP_BLOCK - 1)) <= t_lane) & (c_idx < n_cmp)
    ps = []
    for sr in heads(s_c):
        sr = jnp.where(cmask, sr, NEG)
        m = jnp.max(sr, axis=0, keepdims=True)
        e = jnp.exp2(sr - m)
        den = jnp.sum(e, axis=0, keepdims=True)
        ps.append(e * jnp.where(m > 0.5 * NEG, 1.0 / den, 0.0))
    o_cmp = _dot(vct_ref[...], jnp.concatenate(ps, axis=1).astype(BF16))

    psum = ps[0]
    for r in range(1, R):
        psum = psum + ps[r]
    p_hi = psum.astype(BF16)
    p_lo = (psum - p_hi.astype(F32)).astype(BF16)
    nb = -(-n_slc // 8) * 8
    imp = (_dot(ovt_ref[...], p_hi) + _dot(ovt_ref[...], p_lo))[:nb]
    n_b = lax.broadcasted_iota(jnp.int32, (nb, tq), 0)
    cur = (t0 + lax.broadcasted_iota(jnp.int32, (nb, tq), 1)) // SLC_BLOCK
    forced = (n_b == 0) | (n_b == cur) | (n_b == cur - 1)
    visible = n_b <= cur

    top_n = min(SLC_TOPK, n_slc)
    n_f = n_b.astype(F32)
    work = jnp.where(forced, -jnp.inf, jnp.where(visible, imp, -jnp.inf))
    picked = jnp.zeros((nb, tq), F32)
    for _ in range(max(top_n - 3, 0)):
        mx = jnp.max(work, axis=0, keepdims=True)
        first = jnp.min(jnp.where(work == mx, n_f, float(nb)), axis=0, keepdims=True)
        hit = n_f == jnp.where(mx > -jnp.inf, first, -1.0)
        picked = jnp.where(hit, 1.0, picked)
        work = jnp.where(hit, -jnp.inf, work)
    chosen = jnp.where(forced, 1.0, jnp.where(cur < top_n, 1.0, picked))
    live = jnp.where(visible, jnp.where(n_b < t0 // SLC_BLOCK, chosen, 0.0), 0.0)
    bias_ref[sub, :nb] = jnp.where(live > 0.5, 0.0, NEG)
    if nb < LANES:
        bias_ref[sub, nb:] = jnp.full((LANES - nb, tq), NEG, F32)

    tri = lax.broadcasted_iota(jnp.int32, (tq, tq), 0) <= lax.broadcasted_iota(jnp.int32, (tq, tq), 1)
    m_d, p_d = [], []
    for sr in heads(s_d):
        sr = jnp.where(tri, sr, NEG)
        m = jnp.max(sr, axis=0, keepdims=True)
        m_d.append(m)
        p_d.append(jnp.exp2(sr - m).astype(BF16))
    acc0 = _dot(vst_ref[:, pl.ds(t0a, tq)], jnp.concatenate(p_d, axis=1))

    diff = t_lane - (start + lax.broadcasted_iota(jnp.int32, (span, tq), 0))
    wmask = (diff >= 0) & (diff < WIN_SIZE)
    pw = []
    for sr in heads(s_w):
        sr = jnp.where(wmask, sr, NEG)
        m = jnp.max(sr, axis=0, keepdims=True)
        pw.append(jnp.exp2(sr - m).astype(BF16))
    acc_w = _dot(vwt_ref[:, pl.ds(start, span)], jnp.concatenate(pw, axis=1))
    o_win = acc_w[:HEAD_DIM] / acc_w[HEAD_DIM:HEAD_DIM + 1]
    return q2, o_cmp, o_win, jnp.concatenate(m_d, axis=1), acc0


def _nsa_kernel(q_ref, kc_ref, vct_ref, ks_ref, vst_ref, kw_ref, vwt_ref, gate_ref, ovt_ref,
                o_ref, bias_ref, s_ref, m_ref, acc_ref, *, nsub, tq, tk, n_cmp, n_slc, seq):
    R = NSA_HPG
    W = R * tq
    bpt = tk // SLC_BLOCK
    last_tile = seq // tk - 1
    step = pl.program_id(2)
    fronts = [_nsa_front(q_ref, kc_ref, vct_ref, ks_ref, vst_ref, kw_ref, vwt_ref, ovt_ref, bias_ref,
                         sub=sub, nsub=nsub, tq=tq, n_cmp=n_cmp, n_slc=n_slc, seq=seq) for sub in range(nsub)]
    q_all = jnp.concatenate([f[0] for f in fronts], axis=0)

    def qk(kt, slot):
        k0 = pl.multiple_of(jnp.minimum(kt, last_tile) * tk, tk)
        s_ref[slot] = _dot_nt(ks_ref[pl.ds(k0, tk), :], q_all)

    def update(kt, slot):
        k0 = pl.multiple_of(kt * tk, tk)
        v_t = vst_ref[:, pl.ds(k0, tk)]
        for sub in range(nsub):
            brows = [bias_ref[sub, pl.ds(kt * bpt + j, 1), :] for j in range(bpt)]
            p_all, alphas = [], []
            for r in range(R):
                c0 = sub * W + r * tq
                cols = slice(c0, c0 + tq)
                mo = m_ref[:, cols]
                m8 = None
                for j in range(bpt):
                    blk = s_ref[slot, j * SLC_BLOCK:(j + 1) * SLC_BLOCK, cols]
                    b8 = jnp.max(blk.reshape(SLC_BLOCK // 8, 8, tq), axis=0) + brows[j]
                    m8 = b8 if m8 is None else jnp.maximum(m8, b8)
                mn = jnp.maximum(mo, jnp.max(m8, axis=0, keepdims=True))
                p_all.append(jnp.concatenate(
                    [jnp.exp2(s_ref[slot, j * SLC_BLOCK:(j + 1) * SLC_BLOCK, cols] + (brows[j] - mn)).astype(BF16)
                     for j in range(bpt)], axis=0))
                alphas.append(jnp.exp2(mo - mn))
                m_ref[:, cols] = mn
            pv = _dot(v_t, jnp.concatenate(p_all, axis=1))
            cols = slice(sub * W, (sub + 1) * W)
            acc_ref[:, cols] = jnp.concatenate(alphas, axis=1) * acc_ref[:, cols] + pv

    def pair_body(ii, _):
        a = 2 * ii
        qk(a + 1, 1)
        update(a, 0)
        qk(a + 2, 0)
        update(a + 1, 1)
        return 0

    for sub, f in enumerate(fronts):
        m_ref[:, sub * W:(sub + 1) * W] = f[3]
        acc_ref[:, sub * W:(sub + 1) * W] = f[4]
    n_main = ((step * nsub + nsub - 1) * tq + tk - 1) // tk
    qk(0, 0)
    lax.fori_loop(0, (n_main + 1) // 2, pair_body, 0)
    o_slc = acc_ref[:HEAD_DIM, :] / acc_ref[HEAD_DIM:HEAD_DIM + 1, :]

    for sub in range(nsub):
        rows = slice(sub * tq, (sub + 1) * tq)
        _, o_cmp, o_win, _, _ = fronts[sub]
        gate_t = gate_ref[rows, :].T
        ys = []
        for r in range(R):
            cols = slice(r * tq, (r + 1) * tq)
            ys.append(gate_t[3 * r:3 * r + 1] * o_cmp[:, cols]
                      + gate_t[3 * r + 1:3 * r + 2] * o_slc[:, sub * W + r * tq:sub * W + (r + 1) * tq]
                      + gate_t[3 * r + 2:3 * r + 3] * o_win[:, cols])
        o_ref[rows, :] = jnp.concatenate(ys, axis=0).T.astype(o_ref.dtype)


def _overlap_t(ncp, n_cmp, n_slc):
    cs = np.arange(ncp)[None, :] * CMP_STRIDE
    ss = np.arange(LANES)[:, None] * SLC_BLOCK
    ov = np.clip(np.minimum(cs + CMP_BLOCK, ss + SLC_BLOCK) - np.maximum(cs, ss), 0, None) / CMP_BLOCK
    ov = ov * (np.arange(ncp)[None, :] < n_cmp) * (np.arange(LANES)[:, None] < n_slc)
    return jnp.asarray(ov, BF16)


def _nsa(qa, kc, vct, ks, vst, kw, vwt, gns, tq, tk, nsub):
    B, H, S, dh = qa.shape
    ncp = kc.shape[2]
    n_cmp = (S - CMP_BLOCK) // CMP_STRIDE + 1
    n_slc = S // SLC_BLOCK
    assert n_slc <= LANES and S % (2 * tk) == 0 and tk % SLC_BLOCK == 0 and tq == 2 * SLC_BLOCK and S % (nsub * tq) == 0
    ovt = _overlap_t(ncp, n_cmp, n_slc)
    k_c = pl.BlockSpec((None, None, ncp, dh), lambda b, g, t: (b, g, 0, 0))
    v_c = pl.BlockSpec((None, None, dh, ncp), lambda b, g, t: (b, g, 0, 0))
    k_s = pl.BlockSpec((None, None, S, dh), lambda b, g, t: (b, g, 0, 0))
    v_s = pl.BlockSpec((None, None, vst.shape[2], S), lambda b, g, t: (b, g, 0, 0))
    v_w = pl.BlockSpec((None, None, vwt.shape[2], S), lambda b, g, t: (b, g, 0, 0))
    const = lambda b, g, t: (0, 0)
    kern = functools.partial(_nsa_kernel, tq=tq, tk=tk, n_cmp=n_cmp, n_slc=n_slc, seq=S, nsub=nsub)
    return pl.pallas_call(
        kern,
        grid=(B, NSA_GROUPS, S // (nsub * tq)),
        in_specs=[pl.BlockSpec((None, NSA_HPG, nsub * tq, dh), lambda b, g, t: (b, g, t, 0)),
                  k_c, v_c, k_s, v_s, k_s, v_w,
                  pl.BlockSpec((None, nsub * tq, LANES), lambda b, g, t: (b, t, g)),
                  pl.BlockSpec((LANES, ncp), const)],
        out_specs=pl.BlockSpec((None, nsub * tq, NSA_HPG * dh), lambda b, g, t: (b, t, g)),
        out_shape=jax.ShapeDtypeStruct((B, S, H * dh), BF16),
        scratch_shapes=[pltpu.VMEM((nsub, LANES, tq), F32),
                        pltpu.VMEM((2, tk, nsub * NSA_HPG * tq), F32),
                        pltpu.VMEM((1, nsub * NSA_HPG * tq), F32),
                        pltpu.VMEM((vst.shape[2], nsub * NSA_HPG * tq), F32)],
        compiler_params=pltpu.CompilerParams(dimension_semantics=("parallel", "parallel", "arbitrary"),
                                             vmem_limit_bytes=VMEM_LIMIT),
        name="nsa",
    )(qa, kc, vct, ks, vst, kw, vwt, gns, ovt)


def _dilated_kernel(q_ref, kp_ref, kc_ref, vp_ref, vc_ref, o_ref, lse_ref, vt_ref, *, d, tq, nq):
    span = 2 * tq
    nh = LANES // HEAD_DIM
    row = lax.broadcasted_iota(jnp.int32, (span, tq), 0)
    lane = lax.broadcasted_iota(jnp.int32, (span, tq), 1)
    diff = (lane + tq) - row
    band = (diff >= 0) & (diff <= DIL_SPAN)
    has_prev = pl.program_id(1) > 0
    first = band & ((row >= tq) | has_prev)
    head_of_lane = lax.broadcasted_iota(jnp.int32, (tq, LANES), 1) // HEAD_DIM
    vt_ref[:, LANES:, :] = jnp.ones((nq * d, ONES_ROWS, span), vt_ref.dtype)

    def scores(j, r):
        rows = pl.ds(r + j * tq * d, tq, stride=d)
        qb = q_ref[rows, :]
        if j == 0:
            kb = jnp.concatenate([kp_ref[pl.ds(r, tq, stride=d), :], kc_ref[rows, :]], axis=0)
            vb = jnp.concatenate([vp_ref[pl.ds(r, tq, stride=d), :], vc_ref[rows, :]], axis=0)
        else:
            kv_rows = pl.ds(r + (j - 1) * tq * d, span, stride=d)
            kb, vb = kc_ref[kv_rows, :], vc_ref[kv_rows, :]
        vt_ref[j * d + r, 0:LANES, :] = vb.T.astype(BF16)
        q_bd = jnp.concatenate([jnp.where(head_of_lane == h, qb, 0.0) for h in range(nh)], axis=0).astype(BF16)
        return _dot_nt(kb.astype(BF16), q_bd)

    def finish(j, r, s):
        mask = first if j == 0 else band
        ms, ps = [], []
        for h in range(nh):
            sh = jnp.where(mask, s[:, h * tq:(h + 1) * tq], NEG)
            m = jnp.max(sh, axis=0, keepdims=True)
            ms.append(m)
            ps.append(jnp.exp2(sh - m).astype(BF16))
        pv = _dot(vt_ref[j * d + r], jnp.concatenate(ps, axis=1))
        o_t, l_t = [], []
        for h in range(nh):
            den = pv[LANES:LANES + 1, h * tq:(h + 1) * tq]
            o_t.append(pv[h * HEAD_DIM:(h + 1) * HEAD_DIM, h * tq:(h + 1) * tq] / den)
            l_t.append(jnp.broadcast_to(ms[h] + jnp.log2(den), (HEAD_DIM, tq)))
        rows = pl.ds(r + j * tq * d, tq, stride=d)
        o_ref[rows, :] = jnp.concatenate(o_t, axis=0).T
        lse_ref[rows, :] = jnp.concatenate(l_t, axis=0).T

    pending = None
    for j in range(nq):
        for r in range(d):
            s = scores(j, r)
            if pending is not None:
                finish(*pending)
            pending = (j, r, s)
    finish(*pending)


def _dilated(qb, kb, vb, g, block_tokens):
    B, S, _ = qb.shape
    d = DIL_PAIRS[g][1]
    tq = min(LANES, S // d)
    nq = max(block_tokens // (d * tq), 1)
    assert DIL_PAIRS[g][0] // d == DIL_SPAN and DIL_SPAN <= tq and S % (nq * d * tq) == 0
    gw = DIL_HPG * HEAD_DIM
    nb = gw // LANES
    cur = pl.BlockSpec((None, nq * d * tq, LANES), lambda b, i, c: (b, i, g * nb + c))
    prev = pl.BlockSpec((None, d * tq, LANES), lambda b, i, c: (b, jnp.maximum(i * nq - 1, 0), g * nb + c))
    out_blk = pl.BlockSpec((None, nq * d * tq, LANES), lambda b, i, c: (b, i, c))
    return pl.pallas_call(
        functools.partial(_dilated_kernel, d=d, tq=tq, nq=nq),
        grid=(B, S // (nq * d * tq), nb),
        in_specs=[cur, prev, cur, prev, cur],
        out_specs=[out_blk, out_blk],
        out_shape=[jax.ShapeDtypeStruct((B, S, gw), F32)] * 2,
        scratch_shapes=[pltpu.VMEM((nq * d, LANES + ONES_ROWS, 2 * tq), BF16)],
        compiler_params=pltpu.CompilerParams(dimension_semantics=("parallel", "arbitrary", "arbitrary"),
                                             vmem_limit_bytes=VMEM_LIMIT),
        name=f"dilated{g}",
    )(qb, kb, kb, vb, vb)


def _out_mlp_kernel(x_ref, ya_ref, o0_ref, o1_ref, o2_ref, l0_ref, l1_ref, l2_ref, gma_ref, gmb_ref,
                    woa_ref, wob_ref, wout_ref, g2_ref, wup_ref, wdn_ref, out_ref, *, fc):
    l0, l1, l2 = l0_ref[...], l1_ref[...], l2_ref[...]
    mx = jnp.maximum(jnp.maximum(l0, l1), l2)
    e0, e1, e2 = jnp.exp2(l0 - mx), jnp.exp2(l1 - mx), jnp.exp2(l2 - mx)
    yb = (e0 * o0_ref[...] + e1 * o1_ref[...] + e2 * o2_ref[...]) / (e0 + e1 + e2)
    ta = _dot(ya_ref[...], woa_ref[...])
    tb = _dot(yb.astype(BF16), wob_ref[...])
    mixed = gma_ref[...].astype(F32) * ta + gmb_ref[...].astype(F32) * tb
    x1 = x_ref[...] + _dot(mixed.astype(BF16), wout_ref[...])
    ms = jnp.mean(x1 * x1, axis=-1, keepdims=True)
    h = (x1 * lax.rsqrt(ms + NORM_EPS) * g2_ref[...]).astype(BF16)
    acc = x1
    for off, w in _chunks(wup_ref.shape[1], fc):
        u = jnp.maximum(_dot(h, wup_ref[:, off:off + w]), 0.0)
        acc = acc + _dot((u * u).astype(BF16), wdn_ref[off:off + w, :])
    out_ref[...] = acc


def _out_mlp(x, ya, dil, gma, gmb, w_o_a, w_o_b, w_out, norm2_g, w_up, w_down, tm, fc):
    B, S, D = x.shape
    row = lambda w: pl.BlockSpec((None, tm, w), lambda b, m: (b, m, 0))
    full = lambda a: pl.BlockSpec(a.shape, lambda b, m: (0, 0))
    gw = DIL_HPG * HEAD_DIM
    ws = [w.astype(BF16) for w in (w_o_a, w_o_b, w_out)]
    g2 = norm2_g.reshape(1, D)
    wu, wd = w_up.astype(BF16), w_down.astype(BF16)
    return pl.pallas_call(
        functools.partial(_out_mlp_kernel, fc=fc),
        grid=(B, S // tm),
        in_specs=([row(D), row(ya.shape[-1])] + [row(gw)] * 6 + [row(D), row(D)]
                  + [full(w) for w in ws] + [full(g2), full(wu), full(wd)]),
        out_specs=row(D),
        out_shape=jax.ShapeDtypeStruct((B, S, D), F32),
        compiler_params=pltpu.CompilerParams(dimension_semantics=("parallel", "parallel"),
                                             vmem_limit_bytes=VMEM_LIMIT),
        name="out_mlp",
    )(x, ya, dil[0][0], dil[1][0], dil[2][0], dil[0][1], dil[1][1], dil[2][1], gma, gmb, *ws, g2, wu, wd)


def kernel(x, norm1_g, w_in, q_norm_a, k_norm_cmp, k_norm_slc, k_norm_win, cmp_k_pos, cmp_k_w1, cmp_k_w2,
           cmp_v_pos, cmp_v_w1, cmp_v_w2, q_norm_b, k_norm_b, w_o_a, w_o_b, w_out, norm2_g, w_up, w_down):
    depth = w_in.shape[0]
    for i in range(depth):
        (qa, ks, kw, qb, kb, kvc_raw, vb, gma, gmb, gns, vst, vwt) = _inproj(
            x, norm1_g[i], w_in[i], q_norm_a[i], k_norm_slc[i], k_norm_win[i], q_norm_b[i], k_norm_b[i], tm=512)
        kc, vct = _compress(kvc_raw, k_norm_cmp[i], cmp_k_pos[i], cmp_k_w1[i], cmp_k_w2[i],
                           cmp_v_pos[i], cmp_v_w1[i], cmp_v_w2[i])
        ya = _nsa(qa, kc, vct, ks, vst, kw, vwt, gns, tq=128, tk=512, nsub=4)
        dil = [_dilated(qb, kb, vb, g, block_tokens=1024) for g in range(DIL_GROUPS)]
        x = _out_mlp(x, ya, dil, gma, gmb, w_o_a[i], w_o_b[i], w_out[i], norm2_g[i], w_up[i], w_down[i],
                     tm=512, fc=1024)
    return x
```

```python
import functools

import numpy as np
import jax
import jax.numpy as jnp
from jax import lax
from jax.experimental import pallas as pl
from jax.experimental.pallas import tpu as pltpu

HEAD_DIM = 64
HALF = HEAD_DIM // 2
ROPE_THETA = 10000.0
NORM_EPS = 1e-6
NEG = -1e30
NSA_HEADS = 12
NSA_GROUPS = 3
NSA_HPG = NSA_HEADS // NSA_GROUPS
CMP_BLOCK = 32
CMP_STRIDE = 16
SLC_BLOCK = 64
SLC_TOPK = 16
WIN_SIZE = 512
DIL_PAIRS = ((128, 1), (512, 4), (2048, 16))
DIL_GROUPS = len(DIL_PAIRS)
DIL_HPG = 4
DIL_HEADS = DIL_GROUPS * DIL_HPG
DIL_SPAN = 128
LOG2E = float(np.log2(np.e))
ONES_ROWS = 16

LANES = 128
MXU_N = 256
VMEM_LIMIT = 56 * 1024 * 1024

BF16 = jnp.bfloat16
F32 = jnp.float32

ROPE_SEGS = (("qa", 768), ("qb", 768), ("kb", 768), ("ks", 192), ("kw", 192))
PLAIN_SEGS = (("kvc", 384), ("vb", 768))
SIG_SEGS = (("gma", 1024), ("gmb", 1024), ("gns", 384))
ROPE_W = sum(w for _, w in ROPE_SEGS)
PLAIN_W = sum(w for _, w in PLAIN_SEGS)
SIG_W = sum(w for _, w in SIG_SEGS)
HEAD_MAJOR = ("qa", "ks", "kw")


def _dot(a, b):
    return jnp.dot(a, b, preferred_element_type=F32)


def _dot_nt(a, b):
    return lax.dot_general(a, b, (((1,), (1,)), ((), ())), preferred_element_type=F32)


def _chunks(total, width):
    out, off = [], 0
    while off < total:
        w = min(width, total - off)
        out.append((off, w))
        off += w
    return out


def _seg_lookup(segs, col):
    off = 0
    for name, w in segs:
        if col < off + w:
            return name, col - off
        off += w
    raise ValueError(col)


def _inproj_kernel(x_ref, g1_ref, w_ref, wvt_ref, gain_ref, bd_ref, cos_ref, sin_ref,
                   qa_ref, ks_ref, kw_ref, qb_ref, kb_ref,
                   kvc_ref, vb_ref, gma_ref, gmb_ref, gns_ref, vst_ref, vwt_ref):
    outs = dict(qa=qa_ref, ks=ks_ref, kw=kw_ref, qb=qb_ref, kb=kb_ref, kvc=kvc_ref,
                vb=vb_ref, gma=gma_ref, gmb=gmb_ref, gns=gns_ref)

    def emit(segs, col, val):
        name, rel = _seg_lookup(segs, col)
        ref = outs[name]
        if name in HEAD_MAJOR:
            for p in range(LANES // HEAD_DIM):
                nm, r = _seg_lookup(segs, col + p * HEAD_DIM)
                outs[nm][r // HEAD_DIM] = val[:, p * HEAD_DIM:(p + 1) * HEAD_DIM].astype(outs[nm].dtype)
        else:
            nm2, _ = _seg_lookup(segs, col + HEAD_DIM)
            if nm2 != name:
                raise ValueError("unaligned dense segment")
            ref[:, rel:rel + LANES] = val.astype(ref.dtype)

    x = x_ref[...]
    ms = jnp.mean(x * x, axis=-1, keepdims=True)
    h = (x * lax.rsqrt(ms + NORM_EPS) * g1_ref[...]).astype(BF16)

    lane = lax.broadcasted_iota(jnp.int32, (x.shape[0], LANES), 1)
    first_half = (lane % HEAD_DIM) < HALF
    cos = cos_ref[...]
    sin = sin_ref[...]
    bd = bd_ref[...]

    def rope_epilogue(off, w, y):
        msq = _dot((y * y).astype(BF16), bd[:w, :w])
        yn = y * lax.rsqrt(msq + NORM_EPS) * gain_ref[:, off:off + w]
        for u in range(w // LANES):
            v = yn[:, u * LANES:(u + 1) * LANES]
            rot = jnp.where(first_half, pltpu.roll(v, LANES - HALF, 1), pltpu.roll(v, HALF, 1))
            emit(ROPE_SEGS, off + u * LANES, v * cos + rot * sin)

    def plain_epilogue(off, w, y):
        for u in range(w // LANES):
            emit(PLAIN_SEGS, off + u * LANES, y[:, u * LANES:(u + 1) * LANES])

    def sig_epilogue(off, w, y):
        y = jax.nn.sigmoid(y)
        for u in range(w // LANES):
            emit(SIG_SEGS, off + u * LANES, y[:, u * LANES:(u + 1) * LANES])

    work = ([(0, off, w, rope_epilogue) for off, w in _chunks(ROPE_W, MXU_N)]
            + [(ROPE_W, off, w, plain_epilogue) for off, w in _chunks(PLAIN_W, MXU_N)]
            + [(ROPE_W + PLAIN_W, off, w, sig_epilogue) for off, w in _chunks(SIG_W, MXU_N)])
    pending = None
    for base, off, w, epilogue in work:
        y = _dot(h, w_ref[:, base + off:base + off + w])
        if pending is not None:
            pending[0](pending[1], pending[2], pending[3])
        pending = (epilogue, off, w, y)
    pending[0](pending[1], pending[2], pending[3])

    v_t = _dot_nt(wvt_ref[...], h)
    ones = jnp.ones((ONES_ROWS, x.shape[0]), vst_ref.dtype)
    for i, ref in enumerate((vst_ref, vwt_ref)):
        for g in range(NSA_GROUPS):
            r0 = (i * NSA_GROUPS + g) * HEAD_DIM
            ref[g, 0:HEAD_DIM, :] = v_t[r0:r0 + HEAD_DIM].astype(ref.dtype)
            ref[g, HEAD_DIM:HEAD_DIM + ONES_ROWS, :] = ones


def _rope_tables(positions, width):
    inv_freq = np.power(ROPE_THETA, -np.arange(HALF, dtype=np.float64) / HALF)
    ang = np.asarray(positions, np.float64)[:, None] * inv_freq[None, :]
    reps = width // HEAD_DIM
    cos = np.tile(np.concatenate([np.cos(ang), np.cos(ang)], axis=1), (1, reps))
    sin = np.tile(np.concatenate([-np.sin(ang), np.sin(ang)], axis=1), (1, reps))
    return jnp.asarray(cos, F32), jnp.asarray(sin, F32)


def _inproj(x, norm1_g, w_in, q_norm_a, k_norm_slc, k_norm_win, q_norm_b, k_norm_b, tm):
    B, S, D = x.shape
    scale = HEAD_DIM ** -0.5
    sp = np.cumsum((0, 768, 192, 192, 192, 192, 192, 192, 36, 768, 768, 768, 1024, 1024))
    names = ("qa", "kc", "vc", "ks", "vs", "kw", "vw", "gns", "qb", "kb", "vb", "gma", "gmb")
    w_bf = w_in.astype(BF16)
    col = {n: w_bf[:, int(sp[i]):int(sp[i + 1])] for i, n in enumerate(names)}
    gpg = NSA_HPG * 3
    gns = jnp.concatenate(
        [jnp.pad(col["gns"][:, g * gpg:(g + 1) * gpg], ((0, 0), (0, LANES - gpg))) for g in range(NSA_GROUPS)],
        axis=1)
    col["gns"] = gns
    col["kvc"] = jnp.concatenate([col["kc"], col["vc"]], axis=1)
    w_p = jnp.concatenate([col[n] for n, _ in ROPE_SEGS + PLAIN_SEGS + SIG_SEGS], axis=1)
    wvt = jnp.concatenate([col["vs"], col["vw"]], axis=1).T
    ncol = ROPE_W + PLAIN_W + SIG_W
    assert w_p.shape == (D, ncol)

    gain = jnp.concatenate([
        jnp.tile(q_norm_a * (scale * LOG2E), NSA_HEADS), jnp.tile(q_norm_b * (scale * LOG2E), DIL_HEADS),
        jnp.tile(k_norm_b, DIL_HEADS), jnp.tile(k_norm_slc, NSA_GROUPS), jnp.tile(k_norm_win, NSA_GROUPS)
    ]).reshape(1, ROPE_W).astype(F32)
    bd = jnp.asarray(np.kron(np.eye(MXU_N // HEAD_DIM), np.full((HEAD_DIM, HEAD_DIM), 1.0 / HEAD_DIM)), BF16)
    cos, sin = _rope_tables(np.arange(S), LANES)

    def hm(nh, dt):
        return (jax.ShapeDtypeStruct((B, nh, S, HEAD_DIM), dt),
                pl.BlockSpec((None, nh, tm, HEAD_DIM), lambda b, m: (b, 0, m, 0)))

    def vt():
        rows = HEAD_DIM + ONES_ROWS
        return (jax.ShapeDtypeStruct((B, NSA_GROUPS, rows, S), BF16),
                pl.BlockSpec((None, NSA_GROUPS, rows, tm), lambda b, m: (b, 0, 0, m)))

    def dense(w, dt):
        return (jax.ShapeDtypeStruct((B, S, w), dt), pl.BlockSpec((None, tm, w), lambda b, m: (b, m, 0)))

    outs = [hm(NSA_HEADS, BF16), hm(NSA_GROUPS, BF16), hm(NSA_GROUPS, BF16),
            dense(768, F32), dense(768, F32),
            dense(384, F32),
            dense(768, F32),
            dense(1024, BF16), dense(1024, BF16), dense(384, F32),
            vt(), vt()]
    const = lambda b, m: (0, 0)
    return pl.pallas_call(
        _inproj_kernel,
        grid=(B, S // tm),
        in_specs=[pl.BlockSpec((None, tm, D), lambda b, m: (b, m, 0)),
                  pl.BlockSpec((1, D), const),
                  pl.BlockSpec((D, ncol), const),
                  pl.BlockSpec(wvt.shape, const),
                  pl.BlockSpec((1, ROPE_W), const),
                  pl.BlockSpec((MXU_N, MXU_N), const),
                  pl.BlockSpec((tm, LANES), lambda b, m: (m, 0)),
                  pl.BlockSpec((tm, LANES), lambda b, m: (m, 0))],
        out_specs=[o[1] for o in outs],
        out_shape=[o[0] for o in outs],
        compiler_params=pltpu.CompilerParams(dimension_semantics=("parallel", "arbitrary"),
                                             vmem_limit_bytes=VMEM_LIMIT),
        name="inproj",
    )(x, norm1_g.reshape(1, D), w_p, wvt, gain, bd, cos, sin)


def _gelu_tanh(x):
    return 0.5 * x * (1.0 + jnp.tanh(np.sqrt(2.0 / np.pi) * (x + 0.044715 * (x * x * x))))


def _compress_kernel(x0_ref, x1_ref, x2_ref, pk_ref, pv_ref, w1k_ref, w2k_ref, w1v_ref, w2vt_ref,
                     gain_ref, cos_ref, sin_ref, kc_ref, vct_ref):
    nc = x0_ref.shape[0] // CMP_STRIDE
    half = CMP_STRIDE * HEAD_DIM
    per_blk = LANES // HEAD_DIM
    xs = [[x_ref[pl.ds(l, nc, stride=CMP_STRIDE), :] for l in range(CMP_STRIDE)] for x_ref in (x0_ref, x1_ref, x2_ref)]

    def hidden(head, pos_ref, w1_ref):
        blk, sub = divmod(head, per_blk)
        cols = slice(sub * HEAD_DIM, (sub + 1) * HEAD_DIM)
        x = [xs[blk][l][:, cols] for l in range(CMP_STRIDE)]
        xa = jnp.concatenate([x[l] + pos_ref[l:l + 1, :] for l in range(CMP_STRIDE)], axis=1)
        xb = jnp.concatenate([x[l] + pos_ref[CMP_STRIDE + l:CMP_STRIDE + l + 1, :] for l in range(CMP_STRIDE)], axis=1)
        a = _dot(xa.astype(BF16), w1_ref[0:half, :])
        b = _dot(xb.astype(BF16), w1_ref[half:2 * half, :])
        pre = a + pltpu.roll(b, nc - 1, 0)
        return _gelu_tanh(pre).astype(BF16)

    for g in range(NSA_GROUPS):
        kc = _dot(hidden(g, pk_ref, w1k_ref), w2k_ref[...])
        vct = _dot_nt(w2vt_ref[...], hidden(NSA_GROUPS + g, pv_ref, w1v_ref))
        ms = jnp.mean(kc * kc, axis=-1, keepdims=True)
        kn = kc * lax.rsqrt(ms + NORM_EPS) * gain_ref[...]
        rot = jnp.concatenate([kn[:, HALF:], kn[:, :HALF]], axis=-1)
        kc_ref[g] = (kn * cos_ref[...] + rot * sin_ref[...]).astype(kc_ref.dtype)
        vct_ref[g] = vct.astype(vct_ref.dtype)


def _compress(kvc_raw, k_norm_cmp, kpos, kw1, kw2, vpos, vw1, vw2):
    B, S, w = kvc_raw.shape
    G, dh = NSA_GROUPS, HEAD_DIM
    assert w == 2 * G * dh == 3 * LANES
    nc = S // CMP_STRIDE
    cos, sin = _rope_tables(np.arange(nc) * CMP_STRIDE + CMP_BLOCK - 1, dh)
    const = lambda b: (0, 0)
    xblk = lambda j: pl.BlockSpec((None, S, LANES), lambda b: (b, 0, j))
    hid = kw1.shape[1]
    return pl.pallas_call(
        _compress_kernel,
        grid=(B,),
        in_specs=[xblk(0), xblk(1), xblk(2),
                  pl.BlockSpec((CMP_BLOCK, dh), const), pl.BlockSpec((CMP_BLOCK, dh), const),
                  pl.BlockSpec((CMP_BLOCK * dh, hid), const), pl.BlockSpec((hid, dh), const),
                  pl.BlockSpec((CMP_BLOCK * dh, hid), const), pl.BlockSpec((dh, hid), const),
                  pl.BlockSpec((1, dh), const), pl.BlockSpec((nc, dh), const), pl.BlockSpec((nc, dh), const)],
        out_specs=[pl.BlockSpec((None, G, nc, dh), lambda b: (b, 0, 0, 0)),
                   pl.BlockSpec((None, G, dh, nc), lambda b: (b, 0, 0, 0))],
        out_shape=[jax.ShapeDtypeStruct((B, G, nc, dh), BF16), jax.ShapeDtypeStruct((B, G, dh, nc), BF16)],
        compiler_params=pltpu.CompilerParams(dimension_semantics=("parallel",), vmem_limit_bytes=VMEM_LIMIT),
        name="compress",
    )(kvc_raw, kvc_raw, kvc_raw, kpos, vpos, kw1.astype(BF16), kw2.astype(BF16), vw1.astype(BF16),
      vw2.T.astype(BF16), k_norm_cmp.reshape(1, dh).astype(F32), cos, sin)


def _nsa_front(q_ref, kc_ref, vct_ref, ks_ref, vst_ref, kw_ref, vwt_ref, ovt_ref, bias_ref,
               *, sub, nsub, tq, n_cmp, n_slc, seq):
    R = NSA_HPG
    qt = pl.program_id(2) * nsub + sub
    t0 = qt * tq
    t0a = pl.multiple_of(t0, tq)
    q2 = q_ref[:, sub * tq:(sub + 1) * tq, :].reshape(R * tq, HEAD_DIM)
    ncp = kc_ref.shape[0]
    t_lane = t0 + lax.broadcasted_iota(jnp.int32, (1, tq), 1)
    span = min(WIN_SIZE + tq, seq)
    start = pl.multiple_of(jnp.maximum(t0 - WIN_SIZE, 0), tq)

    def heads(a):
        return [a[:, r * tq:(r + 1) * tq] for r in range(R)]

    s_c = _dot_nt(kc_ref[...], q2)
    s_d = _dot_nt(ks_ref[pl.ds(t0a, tq), :], q2)
    s_w = _dot_nt(kw_ref[pl.ds(start, span), :], q2)

    c_idx = lax.broadcasted_iota(jnp.int32, (ncp, tq), 0)
    cmask = ((c_idx * CMP_STRIDE + (CMP_BLOCK - 1)) <= t_lane) & (c_idx < n_cmp)
    ps = []
    for sr in heads(s_c):
        sr = jnp.where(cmask, sr, NEG)
        m = jnp.max(sr, axis=0, keepdims=True)
        e = jnp.exp2(sr - m)
        den = jnp.sum(e, axis=0, keepdims=True)
        ps.append(e * jnp.where(m > 0.5 * NEG, 1.0 / den, 0.0))
    o_cmp = _dot(vct_ref[...], jnp.concatenate(ps, axis=1).astype(BF16))

    psum = ps[0]
    for r in range(1, R):
        psum = psum + ps[r]
    p_hi = psum.astype(BF16)
    p_lo = (psum - p_hi.astype(F32)).astype(BF16)
    nb = -(-n_slc // 8) * 8
    imp = (_dot(ovt_ref[...], p_hi) + _dot(ovt_ref[...], p_lo))[:nb]
    n_b = lax.broadcasted_iota(jnp.int32, (nb, tq), 0)
    cur = (t0 + lax.broadcasted_iota(jnp.int32, (nb, tq), 1)) // SLC_BLOCK
    forced = (n_b == 0) | (n_b == cur) | (n_b == cur - 1)
    visible = n_b <= cur

    top_n = min(SLC_TOPK, n_slc)
    n_f = n_b.astype(F32)
    work = jnp.where(forced, -jnp.inf, jnp.where(visible, imp, -jnp.inf))
    picked = jnp.zeros((nb, tq), F32)
    for _ in range(max(top_n - 3, 0)):
        mx = jnp.max(work, axis=0, keepdims=True)
        first = jnp.min(jnp.where(work == mx, n_f, float(nb)), axis=0, keepdims=True)
        hit = n_f == jnp.where(mx > -jnp.inf, first, -1.0)
        picked = jnp.where(hit, 1.0, picked)
        work = jnp.where(hit, -jnp.inf, work)
    chosen = jnp.where(forced, 1.0, jnp.where(cur < top_n, 1.0, picked))
    live = jnp.where(visible, jnp.where(n_b < t0 // SLC_BLOCK, chosen, 0.0), 0.0)
    bias_ref[sub, :nb] = jnp.where(live > 0.5, 0.0, NEG)
    if nb < LANES:
        bias_ref[sub, nb:] = jnp.full((LANES - nb, tq), NEG, F32)

    tri = lax.broadcasted_iota(jnp.int32, (tq, tq), 0) <= lax.broadcasted_iota(jnp.int32, (tq, tq), 1)
    m_d, p_d = [], []
    for sr in heads(s_d):
        sr = jnp.where(tri, sr, NEG)
        m = jnp.max(sr, axis=0, keepdims=True)
        m_d.append(m)
        p_d.append(jnp.exp2(sr - m).astype(BF16))
    acc0 = _dot(vst_ref[:, pl.ds(t0a, tq)], jnp.concatenate(p_d, axis=1))

    diff = t_lane - (start + lax.broadcasted_iota(jnp.int32, (span, tq), 0))
    wmask = (diff >= 0) & (diff < WIN_SIZE)
    pw = []
    for sr in heads(s_w):
        sr = jnp.where(wmask, sr, NEG)
        m = jnp.max(sr, axis=0, keepdims=True)
        pw.append(jnp.exp2(sr - m).astype(BF16))
    acc_w = _dot(vwt_ref[:, pl.ds(start, span)], jnp.concatenate(pw, axis=1))
    o_win = acc_w[:HEAD_DIM] / acc_w[HEAD_DIM:HEAD_DIM + 1]
    return q2, o_cmp, o_win, jnp.concatenate(m_d, axis=1), acc0


def _nsa_kernel(q_ref, kc_ref, vct_ref, ks_ref, vst_ref, kw_ref, vwt_ref, gate_ref, ovt_ref,
                o_ref, bias_ref, s_ref, m_ref, acc_ref, *, nsub, tq, tk, n_cmp, n_slc, seq):
    R = NSA_HPG
    W = R * tq
    bpt = tk // SLC_BLOCK
    last_tile = seq // tk - 1
    step = pl.program_id(2)
    fronts = [_nsa_front(q_ref, kc_ref, vct_ref, ks_ref, vst_ref, kw_ref, vwt_ref, ovt_ref, bias_ref,
                         sub=sub, nsub=nsub, tq=tq, n_cmp=n_cmp, n_slc=n_slc, seq=seq) for sub in range(nsub)]
    q_all = jnp.concatenate([f[0] for f in fronts], axis=0)

    def qk(kt, slot, first_sub=0):
        k0 = pl.multiple_of(jnp.minimum(kt, last_tile) * tk, tk)
        s_ref[slot, :, first_sub * W:] = _dot_nt(ks_ref[pl.ds(k0, tk), :], q_all[first_sub * W:])

    def update(kt, slot, first_sub=0):
        k0 = pl.multiple_of(kt * tk, tk)
        v_t = vst_ref[:, pl.ds(k0, tk)]
        for sub in range(first_sub, nsub):
            brows = [bias_ref[sub, pl.ds(kt * bpt + j, 1), :] for j in range(bpt)]
            p_all, alphas = [], []
            for r in range(R):
                c0 = sub * W + r * tq
                cols = slice(c0, c0 + tq)
                mo = m_ref[:, cols]
                m8 = None
                for j in range(bpt):
                    blk = s_ref[slot, j * SLC_BLOCK:(j + 1) * SLC_BLOCK, cols]
                    b8 = jnp.max(blk.reshape(SLC_BLOCK // 8, 8, tq), axis=0) + brows[j]
                    m8 = b8 if m8 is None else jnp.maximum(m8, b8)
                mn = jnp.maximum(mo, jnp.max(m8, axis=0, keepdims=True))
                p_all.append(jnp.concatenate(
                    [jnp.exp2(s_ref[slot, j * SLC_BLOCK:(j + 1) * SLC_BLOCK, cols] + (brows[j] - mn)).astype(BF16)
                     for j in range(bpt)], axis=0))
                alphas.append(jnp.exp2(mo - mn))
                m_ref[:, cols] = mn
            pv = _dot(v_t, jnp.concatenate(p_all, axis=1))
            cols = slice(sub * W, (sub + 1) * W)
            acc_ref[:, cols] = jnp.concatenate(alphas, axis=1) * acc_ref[:, cols] + pv

    def pair_body(ii, _):
        a = 2 * ii
        qk(a + 1, 1)
        update(a, 0)
        qk(a + 2, 0)
        update(a + 1, 1)
        return 0

    for sub, f in enumerate(fronts):
        m_ref[:, sub * W:(sub + 1) * W] = f[3]
        acc_ref[:, sub * W:(sub + 1) * W] = f[4]
    span_tiles = nsub * tq // tk
    qk(0, 0)
    lax.fori_loop(0, step * span_tiles // 2, pair_body, 0)
    for j in range(span_tiles):
        kt = step * span_tiles + j
        slot = j % 2
        first_sub = j * tk // tq + 1
        if j + 1 < span_tiles:
            qk(kt + 1, 1 - slot, (j + 1) * tk // tq + 1)
        update(kt, slot, first_sub)
    o_slc = acc_ref[:HEAD_DIM, :] / acc_ref[HEAD_DIM:HEAD_DIM + 1, :]

    for sub in range(nsub):
        rows = slice(sub * tq, (sub + 1) * tq)
        _, o_cmp, o_win, _, _ = fronts[sub]
        gate_t = gate_ref[rows, :].T
        ys = []
        for r in range(R):
            cols = slice(r * tq, (r + 1) * tq)
            ys.append(gate_t[3 * r:3 * r + 1] * o_cmp[:, cols]
                      + gate_t[3 * r + 1:3 * r + 2] * o_slc[:, sub * W + r * tq:sub * W + (r + 1) * tq]
                      + gate_t[3 * r + 2:3 * r + 3] * o_win[:, cols])
        o_ref[rows, :] = jnp.concatenate(ys, axis=0).T.astype(o_ref.dtype)


def _overlap_t(ncp, n_cmp, n_slc):
    cs = np.arange(ncp)[None, :] * CMP_STRIDE
    ss = np.arange(LANES)[:, None] * SLC_BLOCK
    ov = np.clip(np.minimum(cs + CMP_BLOCK, ss + SLC_BLOCK) - np.maximum(cs, ss), 0, None) / CMP_BLOCK
    ov = ov * (np.arange(ncp)[None, :] < n_cmp) * (np.arange(LANES)[:, None] < n_slc)
    return jnp.asarray(ov, BF16)


def _nsa(qa, kc, vct, ks, vst, kw, vwt, gns, tq, tk, nsub):
    B, H, S, dh = qa.shape
    ncp = kc.shape[2]
    n_cmp = (S - CMP_BLOCK) // CMP_STRIDE + 1
    n_slc = S // SLC_BLOCK
    assert n_slc <= LANES and S % (2 * tk) == 0 and tk % SLC_BLOCK == 0 and tq == 2 * SLC_BLOCK and S % (nsub * tq) == 0 and (nsub * tq) % (2 * tk) == 0
    ovt = _overlap_t(ncp, n_cmp, n_slc)
    k_c = pl.BlockSpec((None, None, ncp, dh), lambda b, g, t: (b, g, 0, 0))
    v_c = pl.BlockSpec((None, None, dh, ncp), lambda b, g, t: (b, g, 0, 0))
    k_s = pl.BlockSpec((None, None, S, dh), lambda b, g, t: (b, g, 0, 0))
    v_s = pl.BlockSpec((None, None, vst.shape[2], S), lambda b, g, t: (b, g, 0, 0))
    v_w = pl.BlockSpec((None, None, vwt.shape[2], S), lambda b, g, t: (b, g, 0, 0))
    const = lambda b, g, t: (0, 0)
    kern = functools.partial(_nsa_kernel, tq=tq, tk=tk, n_cmp=n_cmp, n_slc=n_slc, seq=S, nsub=nsub)
    return pl.pallas_call(
        kern,
        grid=(B, NSA_GROUPS, S // (nsub * tq)),
        in_specs=[pl.BlockSpec((None, NSA_HPG, nsub * tq, dh), lambda b, g, t: (b, g, t, 0)),
                  k_c, v_c, k_s, v_s, k_s, v_w,
                  pl.BlockSpec((None, nsub * tq, LANES), lambda b, g, t: (b, t, g)),
                  pl.BlockSpec((LANES, ncp), const)],
        out_specs=pl.BlockSpec((None, nsub * tq, NSA_HPG * dh), lambda b, g, t: (b, t, g)),
        out_shape=jax.ShapeDtypeStruct((B, S, H * dh), BF16),
        scratch_shapes=[pltpu.VMEM((nsub, LANES, tq), F32),
                        pltpu.VMEM((2, tk, nsub * NSA_HPG * tq), F32),
                        pltpu.VMEM((1, nsub * NSA_HPG * tq), F32),
                        pltpu.VMEM((vst.shape[2], nsub * NSA_HPG * tq), F32)],
        compiler_params=pltpu.CompilerParams(dimension_semantics=("parallel", "parallel", "arbitrary"),
                                             vmem_limit_bytes=VMEM_LIMIT),
        name="nsa",
    )(qa, kc, vct, ks, vst, kw, vwt, gns, ovt)


def _dilated_kernel(q_ref, kp_ref, kc_ref, vp_ref, vc_ref, o_ref, lse_ref, vt_ref, *, d, tq, nq):
    span = 2 * tq
    nh = LANES // HEAD_DIM
    row = lax.broadcasted_iota(jnp.int32, (span, tq), 0)
    lane = lax.broadcasted_iota(jnp.int32, (span, tq), 1)
    diff = (lane + tq) - row
    band = (diff >= 0) & (diff <= DIL_SPAN)
    has_prev = pl.program_id(1) > 0
    first = band & ((row >= tq) | has_prev)
    head_of_lane = lax.broadcasted_iota(jnp.int32, (tq, LANES), 1) // HEAD_DIM
    vt_ref[:, LANES:, :] = jnp.ones((nq * d, ONES_ROWS, span), vt_ref.dtype)

    def scores(j, r):
        rows = pl.ds(r + j * tq * d, tq, stride=d)
        qb = q_ref[rows, :]
        if j == 0:
            kb = jnp.concatenate([kp_ref[pl.ds(r, tq, stride=d), :], kc_ref[rows, :]], axis=0)
            vb = jnp.concatenate([vp_ref[pl.ds(r, tq, stride=d), :], vc_ref[rows, :]], axis=0)
        else:
            kv_rows = pl.ds(r + (j - 1) * tq * d, span, stride=d)
            kb, vb = kc_ref[kv_rows, :], vc_ref[kv_rows, :]
        vt_ref[j * d + r, 0:LANES, :] = vb.T.astype(BF16)
        q_bd = jnp.concatenate([jnp.where(head_of_lane == h, qb, 0.0) for h in range(nh)], axis=0).astype(BF16)
        return _dot_nt(kb.astype(BF16), q_bd)

    def finish(j, r, s):
        mask = first if j == 0 else band
        ms, ps = [], []
        for h in range(nh):
            sh = jnp.where(mask, s[:, h * tq:(h + 1) * tq], NEG)
            m = jnp.max(sh, axis=0, keepdims=True)
            ms.append(m)
            ps.append(jnp.exp2(sh - m).astype(BF16))
        pv = _dot(vt_ref[j * d + r], jnp.concatenate(ps, axis=1))
        o_t, l_t = [], []
        for h in range(nh):
            den = pv[LANES:LANES + 1, h * tq:(h + 1) * tq]
            o_t.append(pv[h * HEAD_DIM:(h + 1) * HEAD_DIM, h * tq:(h + 1) * tq] / den)
            l_t.append(jnp.broadcast_to(ms[h] + jnp.log2(den), (HEAD_DIM, tq)))
        rows = pl.ds(r + j * tq * d, tq, stride=d)
        o_ref[rows, :] = jnp.concatenate(o_t, axis=0).T
        lse_ref[rows, :] = jnp.concatenate(l_t, axis=0).T

    pending = None
    for j in range(nq):
        for r in range(d):
            s = scores(j, r)
            if pending is not None:
                finish(*pending)
            pending = (j, r, s)
    finish(*pending)


def _dilated(qb, kb, vb, g, block_tokens):
    B, S, _ = qb.shape
    d = DIL_PAIRS[g][1]
    tq = min(LANES, S // d)
    nq = max(block_tokens // (d * tq), 1)
    assert DIL_PAIRS[g][0] // d == DIL_SPAN and DIL_SPAN <= tq and S % (nq * d * tq) == 0
    gw = DIL_HPG * HEAD_DIM
    nb = gw // LANES
    cur = pl.BlockSpec((None, nq * d * tq, LANES), lambda b, i, c: (b, i, g * nb + c))
    prev = pl.BlockSpec((None, d * tq, LANES), lambda b, i, c: (b, jnp.maximum(i * nq - 1, 0), g * nb + c))
    out_blk = pl.BlockSpec((None, nq * d * tq, LANES), lambda b, i, c: (b, i, c))
    return pl.pallas_call(
        functools.partial(_dilated_kernel, d=d, tq=tq, nq=nq),
        grid=(B, S // (nq * d * tq), nb),
        in_specs=[cur, prev, cur, prev, cur],
        out_specs=[out_blk, out_blk],
        out_shape=[jax.ShapeDtypeStruct((B, S, gw), F32)] * 2,
        scratch_shapes=[pltpu.VMEM((nq * d, LANES + ONES_ROWS, 2 * tq), BF16)],
        compiler_params=pltpu.CompilerParams(dimension_semantics=("parallel", "arbitrary", "arbitrary"),
                                             vmem_limit_bytes=VMEM_LIMIT),
        name=f"dilated{g}",
    )(qb, kb, kb, vb, vb)


def _out_mlp_kernel(x_ref, ya_ref, o0_ref, o1_ref, o2_ref, l0_ref, l1_ref, l2_ref, gma_ref, gmb_ref,
                    woa_ref, wob_ref, wout_ref, g2_ref, wup_ref, wdn_ref, out_ref, *, fc):
    l0, l1, l2 = l0_ref[...], l1_ref[...], l2_ref[...]
    mx = jnp.maximum(jnp.maximum(l0, l1), l2)
    e0, e1, e2 = jnp.exp2(l0 - mx), jnp.exp2(l1 - mx), jnp.exp2(l2 - mx)
    yb = (e0 * o0_ref[...] + e1 * o1_ref[...] + e2 * o2_ref[...]) / (e0 + e1 + e2)
    ta = _dot(ya_ref[...], woa_ref[...])
    tb = _dot(yb.astype(BF16), wob_ref[...])
    mixed = gma_ref[...].astype(F32) * ta + gmb_ref[...].astype(F32) * tb
    x1 = x_ref[...] + _dot(mixed.astype(BF16), wout_ref[...])
    ms = jnp.mean(x1 * x1, axis=-1, keepdims=True)
    h = (x1 * lax.rsqrt(ms + NORM_EPS) * g2_ref[...]).astype(BF16)
    acc = x1
    for off, w in _chunks(wup_ref.shape[1], fc):
        u = jnp.maximum(_dot(h, wup_ref[:, off:off + w]), 0.0)
        acc = acc + _dot((u * u).astype(BF16), wdn_ref[off:off + w, :])
    out_ref[...] = acc


def _out_mlp(x, ya, dil, gma, gmb, w_o_a, w_o_b, w_out, norm2_g, w_up, w_down, tm, fc):
    B, S, D = x.shape
    row = lambda w: pl.BlockSpec((None, tm, w), lambda b, m: (b, m, 0))
    full = lambda a: pl.BlockSpec(a.shape, lambda b, m: (0, 0))
    gw = DIL_HPG * HEAD_DIM
    ws = [w.astype(BF16) for w in (w_o_a, w_o_b, w_out)]
    g2 = norm2_g.reshape(1, D)
    wu, wd = w_up.astype(BF16), w_down.astype(BF16)
    return pl.pallas_call(
        functools.partial(_out_mlp_kernel, fc=fc),
        grid=(B, S // tm),
        in_specs=([row(D), row(ya.shape[-1])] + [row(gw)] * 6 + [row(D), row(D)]
                  + [full(w) for w in ws] + [full(g2), full(wu), full(wd)]),
        out_specs=row(D),
        out_shape=jax.ShapeDtypeStruct((B, S, D), F32),
        compiler_params=pltpu.CompilerParams(dimension_semantics=("parallel", "parallel"),
                                             vmem_limit_bytes=VMEM_LIMIT),
        name="out_mlp",
    )(x, ya, dil[0][0], dil[1][0], dil[2][0], dil[0][1], dil[1][1], dil[2][1], gma, gmb, *ws, g2, wu, wd)


def kernel(x, norm1_g, w_in, q_norm_a, k_norm_cmp, k_norm_slc, k_norm_win, cmp_k_pos, cmp_k_w1, cmp_k_w2,
           cmp_v_pos, cmp_v_w1, cmp_v_w2, q_norm_b, k_norm_b, w_o_a, w_o_b, w_out, norm2_g, w_up, w_down):
    depth = w_in.shape[0]
    for i in range(depth):
        (qa, ks, kw, qb, kb, kvc_raw, vb, gma, gmb, gns, vst, vwt) = _inproj(
            x, norm1_g[i], w_in[i], q_norm_a[i], k_norm_slc[i], k_norm_win[i], q_norm_b[i], k_norm_b[i], tm=512)
        kc, vct = _compress(kvc_raw, k_norm_cmp[i], cmp_k_pos[i], cmp_k_w1[i], cmp_k_w2[i],
                           cmp_v_pos[i], cmp_v_w1[i], cmp_v_w2[i])
        ya = _nsa(qa, kc, vct, ks, vst, kw, vwt, gns, tq=128, tk=512, nsub=8)
        dil = [_dilated(qb, kb, vb, g, block_tokens=1024) for g in range(DIL_GROUPS)]
        x = _out_mlp(x, ya, dil, gma, gmb, w_o_a[i], w_o_b[i], w_out[i], norm2_g[i], w_up[i], w_down[i],
                     tm=512, fc=1024)
    return x
```

```python
import functools

import numpy as np
import jax
import jax.numpy as jnp
from jax import lax
from jax.experimental import pallas as pl
from jax.experimental.pallas import tpu as pltpu

HEAD_DIM = 64
HALF = HEAD_DIM // 2
ROPE_THETA = 10000.0
NORM_EPS = 1e-6
NEG = -1e30
NSA_HEADS = 12
NSA_GROUPS = 3
NSA_HPG = NSA_HEADS // NSA_GROUPS
CMP_BLOCK = 32
CMP_STRIDE = 16
SLC_BLOCK = 64
SLC_TOPK = 16
WIN_SIZE = 512
DIL_PAIRS = ((128, 1), (512, 4), (2048, 16))
DIL_GROUPS = len(DIL_PAIRS)
DIL_HPG = 4
DIL_HEADS = DIL_GROUPS * DIL_HPG
DIL_SPAN = 128
LOG2E = float(np.log2(np.e))
ONES_ROWS = 16

LANES = 128
MXU_N = 256
VMEM_LIMIT = 56 * 1024 * 1024

BF16 = jnp.bfloat16
F32 = jnp.float32

ROPE_SEGS = (("qa", 768), ("qb", 768), ("kb", 768), ("ks", 192), ("kw", 192))
PLAIN_SEGS = (("kvc", 384), ("vb", 768))
SIG_SEGS = (("gma", 1024), ("gmb", 1024), ("gns", 384))
ROPE_W = sum(w for _, w in ROPE_SEGS)
PLAIN_W = sum(w for _, w in PLAIN_SEGS)
SIG_W = sum(w for _, w in SIG_SEGS)
HEAD_MAJOR = ("qa", "ks", "kw")


def _dot(a, b):
    return jnp.dot(a, b, preferred_element_type=F32)


def _dot_nt(a, b):
    return lax.dot_general(a, b, (((1,), (1,)), ((), ())), preferred_element_type=F32)


def _chunks(total, width):
    out, off = [], 0
    while off < total:
        w = min(width, total - off)
        out.append((off, w))
        off += w
    return out


def _seg_lookup(segs, col):
    off = 0
    for name, w in segs:
        if col < off + w:
            return name, col - off
        off += w
    raise ValueError(col)


def _inproj_kernel(x_ref, g1_ref, w_ref, wvt_ref, gain_ref, bd_ref, cos_ref, sin_ref,
                   qa_ref, ks_ref, kw_ref, qb_ref, kb_ref,
                   kvc_ref, vb_ref, gma_ref, gmb_ref, gns_ref, vst_ref, vwt_ref):
    outs = dict(qa=qa_ref, ks=ks_ref, kw=kw_ref, qb=qb_ref, kb=kb_ref, kvc=kvc_ref,
                vb=vb_ref, gma=gma_ref, gmb=gmb_ref, gns=gns_ref)

    def emit(segs, col, val):
        name, rel = _seg_lookup(segs, col)
        ref = outs[name]
        if name in HEAD_MAJOR:
            for p in range(LANES // HEAD_DIM):
                nm, r = _seg_lookup(segs, col + p * HEAD_DIM)
                outs[nm][r // HEAD_DIM] = val[:, p * HEAD_DIM:(p + 1) * HEAD_DIM].astype(outs[nm].dtype)
        else:
            nm2, _ = _seg_lookup(segs, col + HEAD_DIM)
            if nm2 != name:
                raise ValueError("unaligned dense segment")
            ref[:, rel:rel + LANES] = val.astype(ref.dtype)

    x = x_ref[...]
    ms = jnp.mean(x * x, axis=-1, keepdims=True)
    h = (x * lax.rsqrt(ms + NORM_EPS) * g1_ref[...]).astype(BF16)

    lane = lax.broadcasted_iota(jnp.int32, (x.shape[0], LANES), 1)
    first_half = (lane % HEAD_DIM) < HALF
    cos = cos_ref[...]
    sin = sin_ref[...]
    bd = bd_ref[...]

    def rope_epilogue(off, w, y):
        msq = _dot((y * y).astype(BF16), bd[:w, :w])
        yn = y * lax.rsqrt(msq + NORM_EPS) * gain_ref[:, off:off + w]
        for u in range(w // LANES):
            v = yn[:, u * LANES:(u + 1) * LANES]
            rot = jnp.where(first_half, pltpu.roll(v, LANES - HALF, 1), pltpu.roll(v, HALF, 1))
            emit(ROPE_SEGS, off + u * LANES, v * cos + rot * sin)

    def plain_epilogue(off, w, y):
        for u in range(w // LANES):
            emit(PLAIN_SEGS, off + u * LANES, y[:, u * LANES:(u + 1) * LANES])

    def sig_epilogue(off, w, y):
        y = jax.nn.sigmoid(y)
        for u in range(w // LANES):
            emit(SIG_SEGS, off + u * LANES, y[:, u * LANES:(u + 1) * LANES])

    work = ([(0, off, w, rope_epilogue) for off, w in _chunks(ROPE_W, MXU_N)]
            + [(ROPE_W, off, w, plain_epilogue) for off, w in _chunks(PLAIN_W, MXU_N)]
            + [(ROPE_W + PLAIN_W, off, w, sig_epilogue) for off, w in _chunks(SIG_W, MXU_N)])
    pending = None
    for base, off, w, epilogue in work:
        y = _dot(h, w_ref[:, base + off:base + off + w])
        if pending is not None:
            pending[0](pending[1], pending[2], pending[3])
        pending = (epilogue, off, w, y)
    pending[0](pending[1], pending[2], pending[3])

    v_t = _dot_nt(wvt_ref[...], h)
    ones = jnp.ones((ONES_ROWS, x.shape[0]), vst_ref.dtype)
    for i, ref in enumerate((vst_ref, vwt_ref)):
        for g in range(NSA_GROUPS):
            r0 = (i * NSA_GROUPS + g) * HEAD_DIM
            ref[g, 0:HEAD_DIM, :] = v_t[r0:r0 + HEAD_DIM].astype(ref.dtype)
            ref[g, HEAD_DIM:HEAD_DIM + ONES_ROWS, :] = ones


def _rope_tables(positions, width):
    inv_freq = np.power(ROPE_THETA, -np.arange(HALF, dtype=np.float64) / HALF)
    ang = np.asarray(positions, np.float64)[:, None] * inv_freq[None, :]
    reps = width // HEAD_DIM
    cos = np.tile(np.concatenate([np.cos(ang), np.cos(ang)], axis=1), (1, reps))
    sin = np.tile(np.concatenate([-np.sin(ang), np.sin(ang)], axis=1), (1, reps))
    return jnp.asarray(cos, F32), jnp.asarray(sin, F32)


def _inproj(x, norm1_g, w_in, q_norm_a, k_norm_slc, k_norm_win, q_norm_b, k_norm_b, tm):
    B, S, D = x.shape
    scale = HEAD_DIM ** -0.5
    sp = np.cumsum((0, 768, 192, 192, 192, 192, 192, 192, 36, 768, 768, 768, 1024, 1024))
    names = ("qa", "kc", "vc", "ks", "vs", "kw", "vw", "gns", "qb", "kb", "vb", "gma", "gmb")
    w_bf = w_in.astype(BF16)
    col = {n: w_bf[:, int(sp[i]):int(sp[i + 1])] for i, n in enumerate(names)}
    gpg = NSA_HPG * 3
    gns = jnp.concatenate(
        [jnp.pad(col["gns"][:, g * gpg:(g + 1) * gpg], ((0, 0), (0, LANES - gpg))) for g in range(NSA_GROUPS)],
        axis=1)
    col["gns"] = gns
    col["kvc"] = jnp.concatenate([col["kc"], col["vc"]], axis=1)
    w_p = jnp.concatenate([col[n] for n, _ in ROPE_SEGS + PLAIN_SEGS + SIG_SEGS], axis=1)
    wvt = jnp.concatenate([col["vs"], col["vw"]], axis=1).T
    ncol = ROPE_W + PLAIN_W + SIG_W
    assert w_p.shape == (D, ncol)

    gain = jnp.concatenate([
        jnp.tile(q_norm_a * (scale * LOG2E), NSA_HEADS), jnp.tile(q_norm_b * (scale * LOG2E), DIL_HEADS),
        jnp.tile(k_norm_b, DIL_HEADS), jnp.tile(k_norm_slc, NSA_GROUPS), jnp.tile(k_norm_win, NSA_GROUPS)
    ]).reshape(1, ROPE_W).astype(F32)
    bd = jnp.asarray(np.kron(np.eye(MXU_N // HEAD_DIM), np.full((HEAD_DIM, HEAD_DIM), 1.0 / HEAD_DIM)), BF16)
    cos, sin = _rope_tables(np.arange(S), LANES)

    def hm(nh, dt):
        return (jax.ShapeDtypeStruct((B, nh, S, HEAD_DIM), dt),
                pl.BlockSpec((None, nh, tm, HEAD_DIM), lambda b, m: (b, 0, m, 0)))

    def vt():
        rows = HEAD_DIM + ONES_ROWS
        return (jax.ShapeDtypeStruct((B, NSA_GROUPS, rows, S), BF16),
                pl.BlockSpec((None, NSA_GROUPS, rows, tm), lambda b, m: (b, 0, 0, m)))

    def dense(w, dt):
        return (jax.ShapeDtypeStruct((B, S, w), dt), pl.BlockSpec((None, tm, w), lambda b, m: (b, m, 0)))

    outs = [hm(NSA_HEADS, BF16), hm(NSA_GROUPS, BF16), hm(NSA_GROUPS, BF16),
            dense(768, F32), dense(768, F32),
            dense(384, F32),
            dense(768, F32),
            dense(1024, BF16), dense(1024, BF16), dense(384, F32),
            vt(), vt()]
    const = lambda b, m: (0, 0)
    return pl.pallas_call(
        _inproj_kernel,
        grid=(B, S // tm),
        in_specs=[pl.BlockSpec((None, tm, D), lambda b, m: (b, m, 0)),
                  pl.BlockSpec((1, D), const),
                  pl.BlockSpec((D, ncol), const),
                  pl.BlockSpec(wvt.shape, const),
                  pl.BlockSpec((1, ROPE_W), const),
                  pl.BlockSpec((MXU_N, MXU_N), const),
                  pl.BlockSpec((tm, LANES), lambda b, m: (m, 0)),
                  pl.BlockSpec((tm, LANES), lambda b, m: (m, 0))],
        out_specs=[o[1] for o in outs],
        out_shape=[o[0] for o in outs],
        compiler_params=pltpu.CompilerParams(dimension_semantics=("parallel", "arbitrary"),
                                             vmem_limit_bytes=VMEM_LIMIT),
        name="inproj",
    )(x, norm1_g.reshape(1, D), w_p, wvt, gain, bd, cos, sin)


def _gelu_tanh(x):
    return 0.5 * x * (1.0 + jnp.tanh(np.sqrt(2.0 / np.pi) * (x + 0.044715 * (x * x * x))))


def _compress_kernel(x0_ref, x1_ref, x2_ref, pk_ref, pv_ref, w1k_ref, w2k_ref, w1v_ref, w2vt_ref,
                     gain_ref, cos_ref, sin_ref, kc_ref, vct_ref):
    nc = x0_ref.shape[0] // CMP_STRIDE
    half = CMP_STRIDE * HEAD_DIM
    per_blk = LANES // HEAD_DIM
    xs = [[x_ref[pl.ds(l, nc, stride=CMP_STRIDE), :] for l in range(CMP_STRIDE)] for x_ref in (x0_ref, x1_ref, x2_ref)]

    def hidden(head, pos_ref, w1_ref):
        blk, sub = divmod(head, per_blk)
        cols = slice(sub * HEAD_DIM, (sub + 1) * HEAD_DIM)
        x = [xs[blk][l][:, cols] for l in range(CMP_STRIDE)]
        xa = jnp.concatenate([x[l] + pos_ref[l:l + 1, :] for l in range(CMP_STRIDE)], axis=1)
        xb = jnp.concatenate([x[l] + pos_ref[CMP_STRIDE + l:CMP_STRIDE + l + 1, :] for l in range(CMP_STRIDE)], axis=1)
        a = _dot(xa.astype(BF16), w1_ref[0:half, :])
        b = _dot(xb.astype(BF16), w1_ref[half:2 * half, :])
        pre = a + pltpu.roll(b, nc - 1, 0)
        return _gelu_tanh(pre).astype(BF16)

    for g in range(NSA_GROUPS):
        kc = _dot(hidden(g, pk_ref, w1k_ref), w2k_ref[...])
        vct = _dot_nt(w2vt_ref[...], hidden(NSA_GROUPS + g, pv_ref, w1v_ref))
        ms = jnp.mean(kc * kc, axis=-1, keepdims=True)
        kn = kc * lax.rsqrt(ms + NORM_EPS) * gain_ref[...]
        rot = jnp.concatenate([kn[:, HALF:], kn[:, :HALF]], axis=-1)
        kc_ref[g] = (kn * cos_ref[...] + rot * sin_ref[...]).astype(kc_ref.dtype)
        vct_ref[g] = vct.astype(vct_ref.dtype)


def _compress(kvc_raw, k_norm_cmp, kpos, kw1, kw2, vpos, vw1, vw2):
    B, S, w = kvc_raw.shape
    G, dh = NSA_GROUPS, HEAD_DIM
    assert w == 2 * G * dh == 3 * LANES
    nc = S // CMP_STRIDE
    cos, sin = _rope_tables(np.arange(nc) * CMP_STRIDE + CMP_BLOCK - 1, dh)
    const = lambda b: (0, 0)
    xblk = lambda j: pl.BlockSpec((None, S, LANES), lambda b: (b, 0, j))
    hid = kw1.shape[1]
    return pl.pallas_call(
        _compress_kernel,
        grid=(B,),
        in_specs=[xblk(0), xblk(1), xblk(2),
                  pl.BlockSpec((CMP_BLOCK, dh), const), pl.BlockSpec((CMP_BLOCK, dh), const),
                  pl.BlockSpec((CMP_BLOCK * dh, hid), const), pl.BlockSpec((hid, dh), const),
                  pl.BlockSpec((CMP_BLOCK * dh, hid), const), pl.BlockSpec((dh, hid), const),
                  pl.BlockSpec((1, dh), const), pl.BlockSpec((nc, dh), const), pl.BlockSpec((nc, dh), const)],
        out_specs=[pl.BlockSpec((None, G, nc, dh), lambda b: (b, 0, 0, 0)),
                   pl.BlockSpec((None, G, dh, nc), lambda b: (b, 0, 0, 0))],
        out_shape=[jax.ShapeDtypeStruct((B, G, nc, dh), BF16), jax.ShapeDtypeStruct((B, G, dh, nc), BF16)],
        compiler_params=pltpu.CompilerParams(dimension_semantics=("parallel",), vmem_limit_bytes=VMEM_LIMIT),
        name="compress",
    )(kvc_raw, kvc_raw, kvc_raw, kpos, vpos, kw1.astype(BF16), kw2.astype(BF16), vw1.astype(BF16),
      vw2.T.astype(BF16), k_norm_cmp.reshape(1, dh).astype(F32), cos, sin)


def _nsa_front(q_ref, kc_ref, vct_ref, ks_ref, vst_ref, kw_ref, vwt_ref, ovt_ref, bias_ref,
               *, sub, nsub, tq, n_cmp, n_slc, seq):
    R = NSA_HPG
    qt = pl.program_id(2) * nsub + sub
    t0 = qt * tq
    t0a = pl.multiple_of(t0, tq)
    q2 = q_ref[:, sub * tq:(sub + 1) * tq, :].reshape(R * tq, HEAD_DIM)
    ncp = kc_ref.shape[0]
    t_lane = t0 + lax.broadcasted_iota(jnp.int32, (1, tq), 1)
    span = min(WIN_SIZE + tq, seq)
    start = pl.multiple_of(jnp.maximum(t0 - WIN_SIZE, 0), tq)

    def heads(a):
        return [a[:, r * tq:(r + 1) * tq] for r in range(R)]

    s_c = _dot_nt(kc_ref[...], q2)
    s_d = _dot_nt(ks_ref[pl.ds(t0a, tq), :], q2)
    s_w = _dot_nt(kw_ref[pl.ds(start, span), :], q2)

    c_idx = lax.broadcasted_iota(jnp.int32, (ncp, tq), 0)
    cmask = ((c_idx * CMP_STRIDE + (CMP_BLOCK - 1)) <= t_lane) & (c_idx < n_cmp)
    ps = []
    for sr in heads(s_c):
        sr = jnp.where(cmask, sr, NEG)
        m = jnp.max(sr, axis=0, keepdims=True)
        e = jnp.exp2(sr - m)
        den = jnp.sum(e, axis=0, keepdims=True)
        ps.append(e * jnp.where(m > 0.5 * NEG, 1.0 / den, 0.0))
    o_cmp = _dot(vct_ref[...], jnp.concatenate(ps, axis=1).astype(BF16))

    psum = ps[0]
    for r in range(1, R):
        psum = psum + ps[r]
    p_hi = psum.astype(BF16)
    p_lo = (psum - p_hi.astype(F32)).astype(BF16)
    nb = -(-n_slc // 8) * 8
    imp = (_dot(ovt_ref[...], p_hi) + _dot(ovt_ref[...], p_lo))[:nb]
    n_b = lax.broadcasted_iota(jnp.int32, (nb, tq), 0)
    cur = (t0 + lax.broadcasted_iota(jnp.int32, (nb, tq), 1)) // SLC_BLOCK
    forced = (n_b == 0) | (n_b == cur) | (n_b == cur - 1)
    visible = n_b <= cur

    top_n = min(SLC_TOPK, n_slc)
    n_f = n_b.astype(F32)
    work = jnp.where(forced, -jnp.inf, jnp.where(visible, imp, -jnp.inf))
    picked = jnp.zeros((nb, tq), F32)
    for _ in range(max(top_n - 3, 0)):
        mx = jnp.max(work, axis=0, keepdims=True)
        first = jnp.min(jnp.where(work == mx, n_f, float(nb)), axis=0, keepdims=True)
        hit = n_f == jnp.where(mx > -jnp.inf, first, -1.0)
        picked = jnp.where(hit, 1.0, picked)
        work = jnp.where(hit, -jnp.inf, work)
    chosen = jnp.where(forced, 1.0, jnp.where(cur < top_n, 1.0, picked))
    live = jnp.where(visible, jnp.where(n_b < t0 // SLC_BLOCK, chosen, 0.0), 0.0)
    bias_ref[sub, :nb] = jnp.where(live > 0.5, 0.0, NEG)
    if nb < LANES:
        bias_ref[sub, nb:] = jnp.full((LANES - nb, tq), NEG, F32)

    tri = lax.broadcasted_iota(jnp.int32, (tq, tq), 0) <= lax.broadcasted_iota(jnp.int32, (tq, tq), 1)
    m_d, p_d = [], []
    for sr in heads(s_d):
        sr = jnp.where(tri, sr, NEG)
        m = jnp.max(sr, axis=0, keepdims=True)
        m_d.append(m)
        p_d.append(jnp.exp2(sr - m).astype(BF16))
    acc0 = _dot(vst_ref[:, pl.ds(t0a, tq)], jnp.concatenate(p_d, axis=1))

    pw = []
    if sub * tq >= WIN_SIZE and span == WIN_SIZE + tq:
        for sr in heads(s_w):
            old = jnp.where(tri, NEG, sr[:tq])
            mid = sr[tq:WIN_SIZE]
            new = jnp.where(tri, sr[WIN_SIZE:], NEG)
            m = jnp.maximum(jnp.maximum(jnp.max(old, axis=0, keepdims=True), jnp.max(mid, axis=0, keepdims=True)),
                            jnp.max(new, axis=0, keepdims=True))
            pw.append(jnp.concatenate([jnp.exp2(old - m), jnp.exp2(mid - m), jnp.exp2(new - m)],
                                      axis=0).astype(BF16))
    else:
        diff = t_lane - (start + lax.broadcasted_iota(jnp.int32, (span, tq), 0))
        wmask = (diff >= 0) & (diff < WIN_SIZE)
        for sr in heads(s_w):
            sr = jnp.where(wmask, sr, NEG)
            m = jnp.max(sr, axis=0, keepdims=True)
            pw.append(jnp.exp2(sr - m).astype(BF16))
    acc_w = _dot(vwt_ref[:, pl.ds(start, span)], jnp.concatenate(pw, axis=1))
    o_win = acc_w[:HEAD_DIM] / acc_w[HEAD_DIM:HEAD_DIM + 1]
    return q2, o_cmp, o_win, jnp.concatenate(m_d, axis=1), acc0


def _nsa_kernel(q_ref, kc_ref, vct_ref, ks_ref, vst_ref, kw_ref, vwt_ref, gate_ref, ovt_ref,
                o_ref, bias_ref, s_ref, m_ref, acc_ref, *, nsub, tq, tk, n_cmp, n_slc, seq):
    R = NSA_HPG
    W = R * tq
    bpt = tk // SLC_BLOCK
    last_tile = seq // tk - 1
    step = pl.program_id(2)
    fronts = [_nsa_front(q_ref, kc_ref, vct_ref, ks_ref, vst_ref, kw_ref, vwt_ref, ovt_ref, bias_ref,
                         sub=sub, nsub=nsub, tq=tq, n_cmp=n_cmp, n_slc=n_slc, seq=seq) for sub in range(nsub)]
    q_all = jnp.concatenate([f[0] for f in fronts], axis=0)

    def qk(kt, slot, first_sub=0):
        k0 = pl.multiple_of(jnp.minimum(kt, last_tile) * tk, tk)
        s_ref[slot, :, first_sub * W:] = _dot_nt(ks_ref[pl.ds(k0, tk), :], q_all[first_sub * W:])

    def update(kt, slot, first_sub=0):
        k0 = pl.multiple_of(kt * tk, tk)
        v_t = vst_ref[:, pl.ds(k0, tk)]
        for sub in range(first_sub, nsub):
            brows = [bias_ref[sub, pl.ds(kt * bpt + j, 1), :] for j in range(bpt)]
            p_all, alphas = [], []
            for r in range(R):
                c0 = sub * W + r * tq
                cols = slice(c0, c0 + tq)
                mo = m_ref[:, cols]
                m8 = None
                for j in range(bpt):
                    blk = s_ref[slot, j * SLC_BLOCK:(j + 1) * SLC_BLOCK, cols]
                    b8 = jnp.max(blk.reshape(SLC_BLOCK // 8, 8, tq), axis=0) + brows[j]
                    m8 = b8 if m8 is None else jnp.maximum(m8, b8)
                mn = jnp.maximum(mo, jnp.max(m8, axis=0, keepdims=True))
                p_all.append(jnp.concatenate(
                    [jnp.exp2(s_ref[slot, j * SLC_BLOCK:(j + 1) * SLC_BLOCK, cols] + (brows[j] - mn)).astype(BF16)
                     for j in range(bpt)], axis=0))
                alphas.append(jnp.exp2(mo - mn))
                m_ref[:, cols] = mn
            pv = _dot(v_t, jnp.concatenate(p_all, axis=1))
            cols = slice(sub * W, (sub + 1) * W)
            acc_ref[:, cols] = jnp.concatenate(alphas, axis=1) * acc_ref[:, cols] + pv

    def pair_body(ii, _):
        a = 2 * ii
        qk(a + 1, 1)
        update(a, 0)
        qk(a + 2, 0)
        update(a + 1, 1)
        return 0

    for sub, f in enumerate(fronts):
        m_ref[:, sub * W:(sub + 1) * W] = f[3]
        acc_ref[:, sub * W:(sub + 1) * W] = f[4]
    span_tiles = nsub * tq // tk
    qk(0, 0)
    lax.fori_loop(0, step * span_tiles // 2, pair_body, 0)
    for j in range(span_tiles):
        kt = step * span_tiles + j
        slot = j % 2
        first_sub = j * tk // tq + 1
        if j + 1 < span_tiles:
            qk(kt + 1, 1 - slot, (j + 1) * tk // tq + 1)
        update(kt, slot, first_sub)
    o_slc = acc_ref[:HEAD_DIM, :] / acc_ref[HEAD_DIM:HEAD_DIM + 1, :]

    for sub in range(nsub):
        rows = slice(sub * tq, (sub + 1) * tq)
        _, o_cmp, o_win, _, _ = fronts[sub]
        gate_t = gate_ref[rows, :].T
        ys = []
        for r in range(R):
            cols = slice(r * tq, (r + 1) * tq)
            ys.append(gate_t[3 * r:3 * r + 1] * o_cmp[:, cols]
                      + gate_t[3 * r + 1:3 * r + 2] * o_slc[:, sub * W + r * tq:sub * W + (r + 1) * tq]
                      + gate_t[3 * r + 2:3 * r + 3] * o_win[:, cols])
        o_ref[rows, :] = jnp.concatenate(ys, axis=0).T.astype(o_ref.dtype)


def _overlap_t(ncp, n_cmp, n_slc):
    cs = np.arange(ncp)[None, :] * CMP_STRIDE
    ss = np.arange(LANES)[:, None] * SLC_BLOCK
    ov = np.clip(np.minimum(cs + CMP_BLOCK, ss + SLC_BLOCK) - np.maximum(cs, ss), 0, None) / CMP_BLOCK
    ov = ov * (np.arange(ncp)[None, :] < n_cmp) * (np.arange(LANES)[:, None] < n_slc)
    return jnp.asarray(ov, BF16)


def _nsa(qa, kc, vct, ks, vst, kw, vwt, gns, tq, tk, nsub):
    B, H, S, dh = qa.shape
    ncp = kc.shape[2]
    n_cmp = (S - CMP_BLOCK) // CMP_STRIDE + 1
    n_slc = S // SLC_BLOCK
    assert n_slc <= LANES and S % (2 * tk) == 0 and tk % SLC_BLOCK == 0 and tq == 2 * SLC_BLOCK and S % (nsub * tq) == 0 and (nsub * tq) % (2 * tk) == 0
    ovt = _overlap_t(ncp, n_cmp, n_slc)
    k_c = pl.BlockSpec((None, None, ncp, dh), lambda b, g, t: (b, g, 0, 0))
    v_c = pl.BlockSpec((None, None, dh, ncp), lambda b, g, t: (b, g, 0, 0))
    k_s = pl.BlockSpec((None, None, S, dh), lambda b, g, t: (b, g, 0, 0))
    v_s = pl.BlockSpec((None, None, vst.shape[2], S), lambda b, g, t: (b, g, 0, 0))
    v_w = pl.BlockSpec((None, None, vwt.shape[2], S), lambda b, g, t: (b, g, 0, 0))
    const = lambda b, g, t: (0, 0)
    kern = functools.partial(_nsa_kernel, tq=tq, tk=tk, n_cmp=n_cmp, n_slc=n_slc, seq=S, nsub=nsub)
    return pl.pallas_call(
        kern,
        grid=(B, NSA_GROUPS, S // (nsub * tq)),
        in_specs=[pl.BlockSpec((None, NSA_HPG, nsub * tq, dh), lambda b, g, t: (b, g, t, 0)),
                  k_c, v_c, k_s, v_s, k_s, v_w,
                  pl.BlockSpec((None, nsub * tq, LANES), lambda b, g, t: (b, t, g)),
                  pl.BlockSpec((LANES, ncp), const)],
        out_specs=pl.BlockSpec((None, nsub * tq, NSA_HPG * dh), lambda b, g, t: (b, t, g)),
        out_shape=jax.ShapeDtypeStruct((B, S, H * dh), BF16),
        scratch_shapes=[pltpu.VMEM((nsub, LANES, tq), F32),
                        pltpu.VMEM((2, tk, nsub * NSA_HPG * tq), F32),
                        pltpu.VMEM((1, nsub * NSA_HPG * tq), F32),
                        pltpu.VMEM((vst.shape[2], nsub * NSA_HPG * tq), F32)],
        compiler_params=pltpu.CompilerParams(dimension_semantics=("parallel", "parallel", "arbitrary"),
                                             vmem_limit_bytes=VMEM_LIMIT),
        name="nsa",
    )(qa, kc, vct, ks, vst, kw, vwt, gns, ovt)


def _dilated_kernel(q_ref, kp_ref, kc_ref, vp_ref, vc_ref, o_ref, lse_ref, vt_ref, *, d, tq, nq):
    span = 2 * tq
    nh = LANES // HEAD_DIM
    row = lax.broadcasted_iota(jnp.int32, (span, tq), 0)
    lane = lax.broadcasted_iota(jnp.int32, (span, tq), 1)
    diff = (lane + tq) - row
    band = (diff >= 0) & (diff <= DIL_SPAN)
    has_prev = pl.program_id(1) > 0
    first = band & ((row >= tq) | has_prev)
    head_of_lane = lax.broadcasted_iota(jnp.int32, (tq, LANES), 1) // HEAD_DIM
    vt_ref[:, LANES:, :] = jnp.ones((nq * d, ONES_ROWS, span), vt_ref.dtype)

    def scores(j, r):
        rows = pl.ds(r + j * tq * d, tq, stride=d)
        qb = q_ref[rows, :]
        if j == 0:
            kb = jnp.concatenate([kp_ref[pl.ds(r, tq, stride=d), :], kc_ref[rows, :]], axis=0)
            vb = jnp.concatenate([vp_ref[pl.ds(r, tq, stride=d), :], vc_ref[rows, :]], axis=0)
        else:
            kv_rows = pl.ds(r + (j - 1) * tq * d, span, stride=d)
            kb, vb = kc_ref[kv_rows, :], vc_ref[kv_rows, :]
        vt_ref[j * d + r, 0:LANES, :] = vb.T.astype(BF16)
        q_bd = jnp.concatenate([jnp.where(head_of_lane == h, qb, 0.0) for h in range(nh)], axis=0).astype(BF16)
        return _dot_nt(kb.astype(BF16), q_bd)

    def finish(j, r, s):
        mask = first if j == 0 else band
        ms, ps = [], []
        for h in range(nh):
            sh = jnp.where(mask, s[:, h * tq:(h + 1) * tq], NEG)
            m = jnp.max(sh, axis=0, keepdims=True)
            ms.append(m)
            ps.append(jnp.exp2(sh - m).astype(BF16))
        pv = _dot(vt_ref[j * d + r], jnp.concatenate(ps, axis=1))
        o_t, l_t = [], []
        for h in range(nh):
            den = pv[LANES:LANES + 1, h * tq:(h + 1) * tq]
            o_t.append(pv[h * HEAD_DIM:(h + 1) * HEAD_DIM, h * tq:(h + 1) * tq] / den)
            l_t.append(jnp.broadcast_to(ms[h] + jnp.log2(den), (HEAD_DIM, tq)))
        rows = pl.ds(r + j * tq * d, tq, stride=d)
        o_ref[rows, :] = jnp.concatenate(o_t, axis=0).T
        lse_ref[rows, :] = jnp.concatenate(l_t, axis=0).T

    pending = None
    for j in range(nq):
        for r in range(d):
            s = scores(j, r)
            if pending is not None:
                finish(*pending)
            pending = (j, r, s)
    finish(*pending)


def _dilated(qb, kb, vb, g, block_tokens):
    B, S, _ = qb.shape
    d = DIL_PAIRS[g][1]
    tq = min(LANES, S // d)
    nq = max(block_tokens // (d * tq), 1)
    assert DIL_PAIRS[g][0] // d == DIL_SPAN and DIL_SPAN <= tq and S % (nq * d * tq) == 0
    gw = DIL_HPG * HEAD_DIM
    nb = gw // LANES
    cur = pl.BlockSpec((None, nq * d * tq, LANES), lambda b, i, c: (b, i, g * nb + c))
    prev = pl.BlockSpec((None, d * tq, LANES), lambda b, i, c: (b, jnp.maximum(i * nq - 1, 0), g * nb + c))
    out_blk = pl.BlockSpec((None, nq * d * tq, LANES), lambda b, i, c: (b, i, c))
    return pl.pallas_call(
        functools.partial(_dilated_kernel, d=d, tq=tq, nq=nq),
        grid=(B, S // (nq * d * tq), nb),
        in_specs=[cur, prev, cur, prev, cur],
        out_specs=[out_blk, out_blk],
        out_shape=[jax.ShapeDtypeStruct((B, S, gw), F32)] * 2,
        scratch_shapes=[pltpu.VMEM((nq * d, LANES + ONES_ROWS, 2 * tq), BF16)],
        compiler_params=pltpu.CompilerParams(dimension_semantics=("parallel", "arbitrary", "arbitrary"),
                                             vmem_limit_bytes=VMEM_LIMIT),
        name=f"dilated{g}",
    )(qb, kb, kb, vb, vb)


def _out_mlp_kernel(x_ref, ya_ref, o0_ref, o1_ref, o2_ref, l0_ref, l1_ref, l2_ref, gma_ref, gmb_ref,
                    woa_ref, wob_ref, wout_ref, g2_ref, wup_ref, wdn_ref, out_ref, *, fc):
    l0, l1, l2 = l0_ref[...], l1_ref[...], l2_ref[...]
    mx = jnp.maximum(jnp.maximum(l0, l1), l2)
    e0, e1, e2 = jnp.exp2(l0 - mx), jnp.exp2(l1 - mx), jnp.exp2(l2 - mx)
    yb = (e0 * o0_ref[...] + e1 * o1_ref[...] + e2 * o2_ref[...]) / (e0 + e1 + e2)
    ta = _dot(ya_ref[...], woa_ref[...])
    tb = _dot(yb.astype(BF16), wob_ref[...])
    mixed = gma_ref[...].astype(F32) * ta + gmb_ref[...].astype(F32) * tb
    x1 = x_ref[...] + _dot(mixed.astype(BF16), wout_ref[...])
    ms = jnp.mean(x1 * x1, axis=-1, keepdims=True)
    h = (x1 * lax.rsqrt(ms + NORM_EPS) * g2_ref[...]).astype(BF16)
    acc = x1
    for off, w in _chunks(wup_ref.shape[1], fc):
        u = jnp.maximum(_dot(h, wup_ref[:, off:off + w]), 0.0)
        acc = acc + _dot((u * u).astype(BF16), wdn_ref[off:off + w, :])
    out_ref[...] = acc


def _out_mlp(x, ya, dil, gma, gmb, w_o_a, w_o_b, w_out, norm2_g, w_up, w_down, tm, fc):
    B, S, D = x.shape
    row = lambda w: pl.BlockSpec((None, tm, w), lambda b, m: (b, m, 0))
    full = lambda a: pl.BlockSpec(a.shape, lambda b, m: (0, 0))
    gw = DIL_HPG * HEAD_DIM
    ws = [w.astype(BF16) for w in (w_o_a, w_o_b, w_out)]
    g2 = norm2_g.reshape(1, D)
    wu, wd = w_up.astype(BF16), w_down.astype(BF16)
    return pl.pallas_call(
        functools.partial(_out_mlp_kernel, fc=fc),
        grid=(B, S // tm),
        in_specs=([row(D), row(ya.shape[-1])] + [row(gw)] * 6 + [row(D), row(D)]
                  + [full(w) for w in ws] + [full(g2), full(wu), full(wd)]),
        out_specs=row(D),
        out_shape=jax.ShapeDtypeStruct((B, S, D), F32),
        compiler_params=pltpu.CompilerParams(dimension_semantics=("parallel", "parallel"),
                                             vmem_limit_bytes=VMEM_LIMIT),
        name="out_mlp",
    )(x, ya, dil[0][0], dil[1][0], dil[2][0], dil[0][1], dil[1][1], dil[2][1], gma, gmb, *ws, g2, wu, wd)


def kernel(x, norm1_g, w_in, q_norm_a, k_norm_cmp, k_norm_slc, k_norm_win, cmp_k_pos, cmp_k_w1, cmp_k_w2,
           cmp_v_pos, cmp_v_w1, cmp_v_w2, q_norm_b, k_norm_b, w_o_a, w_o_b, w_out, norm2_g, w_up, w_down):
    depth = w_in.shape[0]
    for i in range(depth):
        (qa, ks, kw, qb, kb, kvc_raw, vb, gma, gmb, gns, vst, vwt) = _inproj(
            x, norm1_g[i], w_in[i], q_norm_a[i], k_norm_slc[i], k_norm_win[i], q_norm_b[i], k_norm_b[i], tm=512)
        kc, vct = _compress(kvc_raw, k_norm_cmp[i], cmp_k_pos[i], cmp_k_w1[i], cmp_k_w2[i],
                           cmp_v_pos[i], cmp_v_w1[i], cmp_v_w2[i])
        ya = _nsa(qa, kc, vct, ks, vst, kw, vwt, gns, tq=128, tk=512, nsub=8)
        dil = [_dilated(qb, kb, vb, g, block_tokens=2048) for g in range(DIL_GROUPS)]
        x = _out_mlp(x, ya, dil, gma, gmb, w_o_a[i], w_o_b[i], w_out[i], norm2_g[i], w_up[i], w_down[i],
                     tm=512, fc=1024)
    return x
```

```python
import functools

import numpy as np
import jax
import jax.numpy as jnp
from jax import lax
from jax.experimental import pallas as pl
from jax.experimental.pallas import tpu as pltpu

HEAD_DIM = 64
HALF = HEAD_DIM // 2
ROPE_THETA = 10000.0
NORM_EPS = 1e-6
NEG = -1e30
NSA_HEADS = 12
NSA_GROUPS = 3
NSA_HPG = NSA_HEADS // NSA_GROUPS
CMP_BLOCK = 32
CMP_STRIDE = 16
SLC_BLOCK = 64
SLC_TOPK = 16
WIN_SIZE = 512
DIL_PAIRS = ((128, 1), (512, 4), (2048, 16))
DIL_GROUPS = len(DIL_PAIRS)
DIL_HPG = 4
DIL_HEADS = DIL_GROUPS * DIL_HPG
DIL_SPAN = 128
LOG2E = float(np.log2(np.e))
ONES_ROWS = 16

LANES = 128
MXU_N = 256
VMEM_LIMIT = 56 * 1024 * 1024

BF16 = jnp.bfloat16
F32 = jnp.float32

INPROJ_ROWS = 512
OUT_MLP_ROWS = 512
MLP_FF_CHUNK = 1024
NSA_TQ = 2 * SLC_BLOCK
NSA_TK = 512
NSA_TILES_PER_STEP = 8
DIL_BLOCK_TOKENS = 2048

D_MODEL = 1024
Q_A, KV_A, GATE_A, QKV_B = NSA_HEADS * HEAD_DIM, NSA_GROUPS * HEAD_DIM, 3 * NSA_HEADS, DIL_HEADS * HEAD_DIM
IN_NAMES = ("qa", "kc", "vc", "ks", "vs", "kw", "vw", "gns", "qb", "kb", "vb", "gma", "gmb")
IN_SIZES = (Q_A, KV_A, KV_A, KV_A, KV_A, KV_A, KV_A, GATE_A, QKV_B, QKV_B, QKV_B, D_MODEL, D_MODEL)
ROPE_SEGS = (("qa", Q_A), ("qb", QKV_B), ("kb", QKV_B), ("ks", KV_A), ("kw", KV_A))
PLAIN_SEGS = (("kvc", 2 * KV_A), ("vb", QKV_B))
SIG_SEGS = (("gma", D_MODEL), ("gmb", D_MODEL), ("gns", NSA_GROUPS * LANES))
ROPE_W = sum(w for _, w in ROPE_SEGS)
PLAIN_W = sum(w for _, w in PLAIN_SEGS)
SIG_W = sum(w for _, w in SIG_SEGS)
HEAD_MAJOR = ("qa", "ks", "kw")


def _dot(a, b):
    return jnp.dot(a, b, preferred_element_type=F32)


def _dot_nt(a, b):
    return lax.dot_general(a, b, (((1,), (1,)), ((), ())), preferred_element_type=F32)


def _chunks(total, width):
    out, off = [], 0
    while off < total:
        w = min(width, total - off)
        out.append((off, w))
        off += w
    return out


def _seg_lookup(segs, col):
    off = 0
    for name, w in segs:
        if col < off + w:
            return name, col - off
        off += w
    raise ValueError(col)


def _inproj_kernel(x_ref, g1_ref, w_ref, wvt_ref, gain_ref, bd_ref, cos_ref, sin_ref,
                   qa_ref, ks_ref, kw_ref, qb_ref, kb_ref,
                   kvc_ref, vb_ref, gma_ref, gmb_ref, gns_ref, vst_ref, vwt_ref):
    outs = dict(qa=qa_ref, ks=ks_ref, kw=kw_ref, qb=qb_ref, kb=kb_ref, kvc=kvc_ref,
                vb=vb_ref, gma=gma_ref, gmb=gmb_ref, gns=gns_ref)

    def emit(segs, col, val):
        name, rel = _seg_lookup(segs, col)
        ref = outs[name]
        if name in HEAD_MAJOR:
            for p in range(LANES // HEAD_DIM):
                nm, r = _seg_lookup(segs, col + p * HEAD_DIM)
                outs[nm][r // HEAD_DIM] = val[:, p * HEAD_DIM:(p + 1) * HEAD_DIM].astype(outs[nm].dtype)
        else:
            nm2, _ = _seg_lookup(segs, col + HEAD_DIM)
            if nm2 != name:
                raise ValueError("unaligned dense segment")
            ref[:, rel:rel + LANES] = val.astype(ref.dtype)

    x = x_ref[...]
    ms = jnp.mean(x * x, axis=-1, keepdims=True)
    h = (x * lax.rsqrt(ms + NORM_EPS) * g1_ref[...]).astype(BF16)

    lane = lax.broadcasted_iota(jnp.int32, (x.shape[0], LANES), 1)
    first_half = (lane % HEAD_DIM) < HALF
    cos = cos_ref[...]
    sin = sin_ref[...]
    bd = bd_ref[...]

    def rope_epilogue(off, w, y):
        msq = _dot((y * y).astype(BF16), bd[:w, :w])
        yn = y * lax.rsqrt(msq + NORM_EPS) * gain_ref[:, off:off + w]
        for u in range(w // LANES):
            v = yn[:, u * LANES:(u + 1) * LANES]
            rot = jnp.where(first_half, pltpu.roll(v, LANES - HALF, 1), pltpu.roll(v, HALF, 1))
            emit(ROPE_SEGS, off + u * LANES, v * cos + rot * sin)

    def plain_epilogue(off, w, y):
        for u in range(w // LANES):
            emit(PLAIN_SEGS, off + u * LANES, y[:, u * LANES:(u + 1) * LANES])

    def sig_epilogue(off, w, y):
        y = jax.nn.sigmoid(y)
        for u in range(w // LANES):
            emit(SIG_SEGS, off + u * LANES, y[:, u * LANES:(u + 1) * LANES])

    work = ([(0, off, w, rope_epilogue) for off, w in _chunks(ROPE_W, MXU_N)]
            + [(ROPE_W, off, w, plain_epilogue) for off, w in _chunks(PLAIN_W, MXU_N)]
            + [(ROPE_W + PLAIN_W, off, w, sig_epilogue) for off, w in _chunks(SIG_W, MXU_N)])
    pending = None
    for base, off, w, epilogue in work:
        y = _dot(h, w_ref[:, base + off:base + off + w])
        if pending is not None:
            pending[0](pending[1], pending[2], pending[3])
        pending = (epilogue, off, w, y)
    pending[0](pending[1], pending[2], pending[3])

    v_t = _dot_nt(wvt_ref[...], h)
    ones = jnp.ones((ONES_ROWS, x.shape[0]), vst_ref.dtype)
    for i, ref in enumerate((vst_ref, vwt_ref)):
        for g in range(NSA_GROUPS):
            r0 = (i * NSA_GROUPS + g) * HEAD_DIM
            ref[g, 0:HEAD_DIM, :] = v_t[r0:r0 + HEAD_DIM].astype(ref.dtype)
            ref[g, HEAD_DIM:HEAD_DIM + ONES_ROWS, :] = ones


def _rope_tables(positions, width):
    inv_freq = np.power(ROPE_THETA, -np.arange(HALF, dtype=np.float64) / HALF)
    ang = np.asarray(positions, np.float64)[:, None] * inv_freq[None, :]
    reps = width // HEAD_DIM
    cos = np.tile(np.concatenate([np.cos(ang), np.cos(ang)], axis=1), (1, reps))
    sin = np.tile(np.concatenate([-np.sin(ang), np.sin(ang)], axis=1), (1, reps))
    return jnp.asarray(cos, F32), jnp.asarray(sin, F32)


def _inproj(x, norm1_g, w_in, q_norm_a, k_norm_slc, k_norm_win, q_norm_b, k_norm_b, tm):
    B, S, D = x.shape
    scale = HEAD_DIM ** -0.5
    assert D == D_MODEL and w_in.shape == (D, sum(IN_SIZES))
    sp = np.cumsum((0,) + IN_SIZES)
    w_bf = w_in.astype(BF16)
    col = {n: w_bf[:, int(sp[i]):int(sp[i + 1])] for i, n in enumerate(IN_NAMES)}
    gpg = GATE_A // NSA_GROUPS
    gns = jnp.concatenate(
        [jnp.pad(col["gns"][:, g * gpg:(g + 1) * gpg], ((0, 0), (0, LANES - gpg))) for g in range(NSA_GROUPS)],
        axis=1)
    col["gns"] = gns
    col["kvc"] = jnp.concatenate([col["kc"], col["vc"]], axis=1)
    w_p = jnp.concatenate([col[n] for n, _ in ROPE_SEGS + PLAIN_SEGS + SIG_SEGS], axis=1)
    wvt = jnp.concatenate([col["vs"], col["vw"]], axis=1).T
    ncol = ROPE_W + PLAIN_W + SIG_W
    assert w_p.shape == (D, ncol)

    gain = jnp.concatenate([
        jnp.tile(q_norm_a * (scale * LOG2E), NSA_HEADS), jnp.tile(q_norm_b * (scale * LOG2E), DIL_HEADS),
        jnp.tile(k_norm_b, DIL_HEADS), jnp.tile(k_norm_slc, NSA_GROUPS), jnp.tile(k_norm_win, NSA_GROUPS)
    ]).reshape(1, ROPE_W).astype(F32)
    bd = jnp.asarray(np.kron(np.eye(MXU_N // HEAD_DIM), np.full((HEAD_DIM, HEAD_DIM), 1.0 / HEAD_DIM)), BF16)
    cos, sin = _rope_tables(np.arange(S), LANES)

    def hm(nh, dt):
        return (jax.ShapeDtypeStruct((B, nh, S, HEAD_DIM), dt),
                pl.BlockSpec((None, nh, tm, HEAD_DIM), lambda b, m: (b, 0, m, 0)))

    def vt():
        rows = HEAD_DIM + ONES_ROWS
        return (jax.ShapeDtypeStruct((B, NSA_GROUPS, rows, S), BF16),
                pl.BlockSpec((None, NSA_GROUPS, rows, tm), lambda b, m: (b, 0, 0, m)))

    def dense(w, dt):
        return (jax.ShapeDtypeStruct((B, S, w), dt), pl.BlockSpec((None, tm, w), lambda b, m: (b, m, 0)))

    outs = [hm(NSA_HEADS, BF16), hm(NSA_GROUPS, BF16), hm(NSA_GROUPS, BF16),
            dense(QKV_B, F32), dense(QKV_B, F32),
            dense(2 * KV_A, F32),
            dense(QKV_B, F32),
            dense(D, BF16), dense(D, BF16), dense(NSA_GROUPS * LANES, F32),
            vt(), vt()]
    const = lambda b, m: (0, 0)
    return pl.pallas_call(
        _inproj_kernel,
        grid=(B, S // tm),
        in_specs=[pl.BlockSpec((None, tm, D), lambda b, m: (b, m, 0)),
                  pl.BlockSpec((1, D), const),
                  pl.BlockSpec((D, ncol), const),
                  pl.BlockSpec(wvt.shape, const),
                  pl.BlockSpec((1, ROPE_W), const),
                  pl.BlockSpec((MXU_N, MXU_N), const),
                  pl.BlockSpec((tm, LANES), lambda b, m: (m, 0)),
                  pl.BlockSpec((tm, LANES), lambda b, m: (m, 0))],
        out_specs=[o[1] for o in outs],
        out_shape=[o[0] for o in outs],
        compiler_params=pltpu.CompilerParams(dimension_semantics=("parallel", "arbitrary"),
                                             vmem_limit_bytes=VMEM_LIMIT),
        name="inproj",
    )(x, norm1_g.reshape(1, D), w_p, wvt, gain, bd, cos, sin)


def _gelu_tanh(x):
    return 0.5 * x * (1.0 + jnp.tanh(np.sqrt(2.0 / np.pi) * (x + 0.044715 * (x * x * x))))


def _compress_kernel(x0_ref, x1_ref, x2_ref, pk_ref, pv_ref, w1k_ref, w2k_ref, w1v_ref, w2vt_ref,
                     gain_ref, cos_ref, sin_ref, kc_ref, vct_ref):
    nc = x0_ref.shape[0] // CMP_STRIDE
    half = CMP_STRIDE * HEAD_DIM
    per_blk = LANES // HEAD_DIM
    xs = [[x_ref[pl.ds(l, nc, stride=CMP_STRIDE), :] for l in range(CMP_STRIDE)] for x_ref in (x0_ref, x1_ref, x2_ref)]

    def hidden(head, pos_ref, w1_ref):
        blk, sub = divmod(head, per_blk)
        cols = slice(sub * HEAD_DIM, (sub + 1) * HEAD_DIM)
        x = [xs[blk][l][:, cols] for l in range(CMP_STRIDE)]
        xa = jnp.concatenate([x[l] + pos_ref[l:l + 1, :] for l in range(CMP_STRIDE)], axis=1)
        xb = jnp.concatenate([x[l] + pos_ref[CMP_STRIDE + l:CMP_STRIDE + l + 1, :] for l in range(CMP_STRIDE)], axis=1)
        a = _dot(xa.astype(BF16), w1_ref[0:half, :])
        b = _dot(xb.astype(BF16), w1_ref[half:2 * half, :])
        pre = a + pltpu.roll(b, nc - 1, 0)
        return _gelu_tanh(pre).astype(BF16)

    for g in range(NSA_GROUPS):
        kc = _dot(hidden(g, pk_ref, w1k_ref), w2k_ref[...])
        vct = _dot_nt(w2vt_ref[...], hidden(NSA_GROUPS + g, pv_ref, w1v_ref))
        ms = jnp.mean(kc * kc, axis=-1, keepdims=True)
        kn = kc * lax.rsqrt(ms + NORM_EPS) * gain_ref[...]
        rot = jnp.concatenate([kn[:, HALF:], kn[:, :HALF]], axis=-1)
        kc_ref[g] = (kn * cos_ref[...] + rot * sin_ref[...]).astype(kc_ref.dtype)
        vct_ref[g] = vct.astype(vct_ref.dtype)


def _compress(kvc_raw, k_norm_cmp, kpos, kw1, kw2, vpos, vw1, vw2):
    B, S, w = kvc_raw.shape
    G, dh = NSA_GROUPS, HEAD_DIM
    assert w == 2 * G * dh == 3 * LANES
    nc = S // CMP_STRIDE
    cos, sin = _rope_tables(np.arange(nc) * CMP_STRIDE + CMP_BLOCK - 1, dh)
    const = lambda b: (0, 0)
    xblk = lambda j: pl.BlockSpec((None, S, LANES), lambda b: (b, 0, j))
    hid = kw1.shape[1]
    return pl.pallas_call(
        _compress_kernel,
        grid=(B,),
        in_specs=[xblk(0), xblk(1), xblk(2),
                  pl.BlockSpec((CMP_BLOCK, dh), const), pl.BlockSpec((CMP_BLOCK, dh), const),
                  pl.BlockSpec((CMP_BLOCK * dh, hid), const), pl.BlockSpec((hid, dh), const),
                  pl.BlockSpec((CMP_BLOCK * dh, hid), const), pl.BlockSpec((dh, hid), const),
                  pl.BlockSpec((1, dh), const), pl.BlockSpec((nc, dh), const), pl.BlockSpec((nc, dh), const)],
        out_specs=[pl.BlockSpec((None, G, nc, dh), lambda b: (b, 0, 0, 0)),
                   pl.BlockSpec((None, G, dh, nc), lambda b: (b, 0, 0, 0))],
        out_shape=[jax.ShapeDtypeStruct((B, G, nc, dh), BF16), jax.ShapeDtypeStruct((B, G, dh, nc), BF16)],
        compiler_params=pltpu.CompilerParams(dimension_semantics=("parallel",), vmem_limit_bytes=VMEM_LIMIT),
        name="compress",
    )(kvc_raw, kvc_raw, kvc_raw, kpos, vpos, kw1.astype(BF16), kw2.astype(BF16), vw1.astype(BF16),
      vw2.T.astype(BF16), k_norm_cmp.reshape(1, dh).astype(F32), cos, sin)


def _nsa_front(q_ref, kc_ref, vct_ref, ks_ref, vst_ref, kw_ref, vwt_ref, ovt_ref, bias_ref,
               *, sub, nsub, tq, n_cmp, n_slc, seq):
    R = NSA_HPG
    qt = pl.program_id(2) * nsub + sub
    t0 = qt * tq
    t0a = pl.multiple_of(t0, tq)
    q2 = q_ref[:, sub * tq:(sub + 1) * tq, :].reshape(R * tq, HEAD_DIM)
    ncp = kc_ref.shape[0]
    t_lane = t0 + lax.broadcasted_iota(jnp.int32, (1, tq), 1)
    span = min(WIN_SIZE + tq, seq)
    start = pl.multiple_of(jnp.maximum(t0 - WIN_SIZE, 0), tq)

    def heads(a):
        return [a[:, r * tq:(r + 1) * tq] for r in range(R)]

    s_c = _dot_nt(kc_ref[...], q2)
    s_d = _dot_nt(ks_ref[pl.ds(t0a, tq), :], q2)
    s_w = _dot_nt(kw_ref[pl.ds(start, span), :], q2)

    c_idx = lax.broadcasted_iota(jnp.int32, (ncp, tq), 0)
    cmask = ((c_idx * CMP_STRIDE + (CMP_BLOCK - 1)) <= t_lane) & (c_idx < n_cmp)
    ps = []
    for sr in heads(s_c):
        sr = jnp.where(cmask, sr, NEG)
        m = jnp.max(sr, axis=0, keepdims=True)
        e = jnp.exp2(sr - m)
        den = jnp.sum(e, axis=0, keepdims=True)
        ps.append(e * jnp.where(m > 0.5 * NEG, 1.0 / den, 0.0))
    o_cmp = _dot(vct_ref[...], jnp.concatenate(ps, axis=1).astype(BF16))

    psum = ps[0]
    for r in range(1, R):
        psum = psum + ps[r]
    p_hi = psum.astype(BF16)
    p_lo = (psum - p_hi.astype(F32)).astype(BF16)
    nb = -(-n_slc // 8) * 8
    imp = (_dot(ovt_ref[...], p_hi) + _dot(ovt_ref[...], p_lo))[:nb]
    n_b = lax.broadcasted_iota(jnp.int32, (nb, tq), 0)
    cur = (t0 + lax.broadcasted_iota(jnp.int32, (nb, tq), 1)) // SLC_BLOCK
    forced = (n_b == 0) | (n_b == cur) | (n_b == cur - 1)
    visible = n_b <= cur

    top_n = min(SLC_TOPK, n_slc)
    n_f = n_b.astype(F32)
    work = jnp.where(forced, -jnp.inf, jnp.where(visible, imp, -jnp.inf))
    picked = jnp.zeros((nb, tq), F32)
    for _ in range(max(top_n - 3, 0)):
        mx = jnp.max(work, axis=0, keepdims=True)
        first = jnp.min(jnp.where(work == mx, n_f, float(nb)), axis=0, keepdims=True)
        hit = n_f == jnp.where(mx > -jnp.inf, first, -1.0)
        picked = jnp.where(hit, 1.0, picked)
        work = jnp.where(hit, -jnp.inf, work)
    chosen = jnp.where(forced, 1.0, jnp.where(cur < top_n, 1.0, picked))
    live = jnp.where(visible, jnp.where(n_b < t0 // SLC_BLOCK, chosen, 0.0), 0.0)
    bias_ref[sub, :nb] = jnp.where(live > 0.5, 0.0, NEG)
    if nb < LANES:
        bias_ref[sub, nb:] = jnp.full((LANES - nb, tq), NEG, F32)

    tri = lax.broadcasted_iota(jnp.int32, (tq, tq), 0) <= lax.broadcasted_iota(jnp.int32, (tq, tq), 1)
    m_d, p_d = [], []
    for sr in heads(s_d):
        sr = jnp.where(tri, sr, NEG)
        m = jnp.max(sr, axis=0, keepdims=True)
        m_d.append(m)
        p_d.append(jnp.exp2(sr - m).astype(BF16))
    acc0 = _dot(vst_ref[:, pl.ds(t0a, tq)], jnp.concatenate(p_d, axis=1))

    pw = []
    if sub * tq >= WIN_SIZE and span == WIN_SIZE + tq:
        for sr in heads(s_w):
            old = jnp.where(tri, NEG, sr[:tq])
            mid = sr[tq:WIN_SIZE]
            new = jnp.where(tri, sr[WIN_SIZE:], NEG)
            m = jnp.maximum(jnp.maximum(jnp.max(old, axis=0, keepdims=True), jnp.max(mid, axis=0, keepdims=True)),
                            jnp.max(new, axis=0, keepdims=True))
            pw.append(jnp.concatenate([jnp.exp2(old - m), jnp.exp2(mid - m), jnp.exp2(new - m)],
                                      axis=0).astype(BF16))
    else:
        diff = t_lane - (start + lax.broadcasted_iota(jnp.int32, (span, tq), 0))
        wmask = (diff >= 0) & (diff < WIN_SIZE)
        for sr in heads(s_w):
            sr = jnp.where(wmask, sr, NEG)
            m = jnp.max(sr, axis=0, keepdims=True)
            pw.append(jnp.exp2(sr - m).astype(BF16))
    acc_w = _dot(vwt_ref[:, pl.ds(start, span)], jnp.concatenate(pw, axis=1))
    o_win = acc_w[:HEAD_DIM] / acc_w[HEAD_DIM:HEAD_DIM + 1]
    return q2, o_cmp, o_win, jnp.concatenate(m_d, axis=1), acc0


def _nsa_kernel(q_ref, kc_ref, vct_ref, ks_ref, vst_ref, kw_ref, vwt_ref, gate_ref, ovt_ref,
                o_ref, bias_ref, s_ref, m_ref, acc_ref, *, nsub, tq, tk, n_cmp, n_slc, seq):
    R = NSA_HPG
    W = R * tq
    bpt = tk // SLC_BLOCK
    last_tile = seq // tk - 1
    step = pl.program_id(2)
    fronts = [_nsa_front(q_ref, kc_ref, vct_ref, ks_ref, vst_ref, kw_ref, vwt_ref, ovt_ref, bias_ref,
                         sub=sub, nsub=nsub, tq=tq, n_cmp=n_cmp, n_slc=n_slc, seq=seq) for sub in range(nsub)]
    q_all = jnp.concatenate([f[0] for f in fronts], axis=0)

    def qk(kt, slot, first_sub=0):
        k0 = pl.multiple_of(jnp.minimum(kt, last_tile) * tk, tk)
        s_ref[slot, :, first_sub * W:] = _dot_nt(ks_ref[pl.ds(k0, tk), :], q_all[first_sub * W:])

    def update(kt, slot, first_sub=0):
        k0 = pl.multiple_of(kt * tk, tk)
        v_t = vst_ref[:, pl.ds(k0, tk)]
        for sub in range(first_sub, nsub):
            brows = [bias_ref[sub, pl.ds(kt * bpt + j, 1), :] for j in range(bpt)]
            p_all, alphas = [], []
            for r in range(R):
                c0 = sub * W + r * tq
                cols = slice(c0, c0 + tq)
                mo = m_ref[:, cols]
                m8 = None
                for j in range(bpt):
                    blk = s_ref[slot, j * SLC_BLOCK:(j + 1) * SLC_BLOCK, cols]
                    b8 = jnp.max(blk.reshape(SLC_BLOCK // 8, 8, tq), axis=0) + brows[j]
                    m8 = b8 if m8 is None else jnp.maximum(m8, b8)
                mn = jnp.maximum(mo, jnp.max(m8, axis=0, keepdims=True))
                p_all.append(jnp.concatenate(
                    [jnp.exp2(s_ref[slot, j * SLC_BLOCK:(j + 1) * SLC_BLOCK, cols] + (brows[j] - mn)).astype(BF16)
                     for j in range(bpt)], axis=0))
                alphas.append(jnp.exp2(mo - mn))
                m_ref[:, cols] = mn
            pv = _dot(v_t, jnp.concatenate(p_all, axis=1))
            cols = slice(sub * W, (sub + 1) * W)
            acc_ref[:, cols] = jnp.concatenate(alphas, axis=1) * acc_ref[:, cols] + pv

    def pair_body(ii, _):
        a = 2 * ii
        qk(a + 1, 1)
        update(a, 0)
        qk(a + 2, 0)
        update(a + 1, 1)
        return 0

    for sub, f in enumerate(fronts):
        m_ref[:, sub * W:(sub + 1) * W] = f[3]
        acc_ref[:, sub * W:(sub + 1) * W] = f[4]
    span_tiles = nsub * tq // tk
    qk(0, 0)
    lax.fori_loop(0, step * span_tiles // 2, pair_body, 0)
    for j in range(span_tiles):
        kt = step * span_tiles + j
        slot = j % 2
        first_sub = j * tk // tq + 1
        if j + 1 < span_tiles:
            qk(kt + 1, 1 - slot, (j + 1) * tk // tq + 1)
        update(kt, slot, first_sub)
    o_slc = acc_ref[:HEAD_DIM, :] / acc_ref[HEAD_DIM:HEAD_DIM + 1, :]

    for sub in range(nsub):
        rows = slice(sub * tq, (sub + 1) * tq)
        _, o_cmp, o_win, _, _ = fronts[sub]
        gate_t = gate_ref[rows, :].T
        ys = []
        for r in range(R):
            cols = slice(r * tq, (r + 1) * tq)
            ys.append(gate_t[3 * r:3 * r + 1] * o_cmp[:, cols]
                      + gate_t[3 * r + 1:3 * r + 2] * o_slc[:, sub * W + r * tq:sub * W + (r + 1) * tq]
                      + gate_t[3 * r + 2:3 * r + 3] * o_win[:, cols])
        o_ref[rows, :] = jnp.concatenate(ys, axis=0).T.astype(o_ref.dtype)


def _overlap_t(ncp, n_cmp, n_slc):
    cs = np.arange(ncp)[None, :] * CMP_STRIDE
    ss = np.arange(LANES)[:, None] * SLC_BLOCK
    ov = np.clip(np.minimum(cs + CMP_BLOCK, ss + SLC_BLOCK) - np.maximum(cs, ss), 0, None) / CMP_BLOCK
    ov = ov * (np.arange(ncp)[None, :] < n_cmp) * (np.arange(LANES)[:, None] < n_slc)
    return jnp.asarray(ov, BF16)


def _nsa(qa, kc, vct, ks, vst, kw, vwt, gns, tq, tk, nsub):
    B, H, S, dh = qa.shape
    ncp = kc.shape[2]
    n_cmp = (S - CMP_BLOCK) // CMP_STRIDE + 1
    n_slc = S // SLC_BLOCK
    assert n_slc <= LANES and S % (2 * tk) == 0 and tk % SLC_BLOCK == 0 and tq == 2 * SLC_BLOCK and S % (nsub * tq) == 0 and (nsub * tq) % (2 * tk) == 0
    ovt = _overlap_t(ncp, n_cmp, n_slc)
    k_c = pl.BlockSpec((None, None, ncp, dh), lambda b, g, t: (b, g, 0, 0))
    v_c = pl.BlockSpec((None, None, dh, ncp), lambda b, g, t: (b, g, 0, 0))
    k_s = pl.BlockSpec((None, None, S, dh), lambda b, g, t: (b, g, 0, 0))
    v_s = pl.BlockSpec((None, None, vst.shape[2], S), lambda b, g, t: (b, g, 0, 0))
    v_w = pl.BlockSpec((None, None, vwt.shape[2], S), lambda b, g, t: (b, g, 0, 0))
    const = lambda b, g, t: (0, 0)
    kern = functools.partial(_nsa_kernel, tq=tq, tk=tk, n_cmp=n_cmp, n_slc=n_slc, seq=S, nsub=nsub)
    return pl.pallas_call(
        kern,
        grid=(B, NSA_GROUPS, S // (nsub * tq)),
        in_specs=[pl.BlockSpec((None, NSA_HPG, nsub * tq, dh), lambda b, g, t: (b, g, t, 0)),
                  k_c, v_c, k_s, v_s, k_s, v_w,
                  pl.BlockSpec((None, nsub * tq, LANES), lambda b, g, t: (b, t, g)),
                  pl.BlockSpec((LANES, ncp), const)],
        out_specs=pl.BlockSpec((None, nsub * tq, NSA_HPG * dh), lambda b, g, t: (b, t, g)),
        out_shape=jax.ShapeDtypeStruct((B, S, H * dh), BF16),
        scratch_shapes=[pltpu.VMEM((nsub, LANES, tq), F32),
                        pltpu.VMEM((2, tk, nsub * NSA_HPG * tq), F32),
                        pltpu.VMEM((1, nsub * NSA_HPG * tq), F32),
                        pltpu.VMEM((vst.shape[2], nsub * NSA_HPG * tq), F32)],
        compiler_params=pltpu.CompilerParams(dimension_semantics=("parallel", "parallel", "arbitrary"),
                                             vmem_limit_bytes=VMEM_LIMIT),
        name="nsa",
    )(qa, kc, vct, ks, vst, kw, vwt, gns, ovt)


def _dilated_kernel(q_ref, kp_ref, kc_ref, vp_ref, vc_ref, o_ref, lse_ref, vt_ref, *, d, tq, nq):
    span = 2 * tq
    nh = LANES // HEAD_DIM
    row = lax.broadcasted_iota(jnp.int32, (span, tq), 0)
    lane = lax.broadcasted_iota(jnp.int32, (span, tq), 1)
    diff = (lane + tq) - row
    band = (diff >= 0) & (diff <= DIL_SPAN)
    has_prev = pl.program_id(1) > 0
    first = band & ((row >= tq) | has_prev)
    head_of_lane = lax.broadcasted_iota(jnp.int32, (tq, LANES), 1) // HEAD_DIM
    vt_ref[:, LANES:, :] = jnp.ones((nq * d, ONES_ROWS, span), vt_ref.dtype)

    def scores(j, r):
        rows = pl.ds(r + j * tq * d, tq, stride=d)
        qb = q_ref[rows, :]
        if j == 0:
            kb = jnp.concatenate([kp_ref[pl.ds(r, tq, stride=d), :], kc_ref[rows, :]], axis=0)
            vb = jnp.concatenate([vp_ref[pl.ds(r, tq, stride=d), :], vc_ref[rows, :]], axis=0)
        else:
            kv_rows = pl.ds(r + (j - 1) * tq * d, span, stride=d)
            kb, vb = kc_ref[kv_rows, :], vc_ref[kv_rows, :]
        vt_ref[j * d + r, 0:LANES, :] = vb.T.astype(BF16)
        q_bd = jnp.concatenate([jnp.where(head_of_lane == h, qb, 0.0) for h in range(nh)], axis=0).astype(BF16)
        return _dot_nt(kb.astype(BF16), q_bd)

    def finish(j, r, s):
        mask = first if j == 0 else band
        ms, ps = [], []
        for h in range(nh):
            sh = jnp.where(mask, s[:, h * tq:(h + 1) * tq], NEG)
            m = jnp.max(sh, axis=0, keepdims=True)
            ms.append(m)
            ps.append(jnp.exp2(sh - m).astype(BF16))
        pv = _dot(vt_ref[j * d + r], jnp.concatenate(ps, axis=1))
        o_t, l_t = [], []
        for h in range(nh):
            den = pv[LANES:LANES + 1, h * tq:(h + 1) * tq]
            o_t.append(pv[h * HEAD_DIM:(h + 1) * HEAD_DIM, h * tq:(h + 1) * tq] / den)
            l_t.append(jnp.broadcast_to(ms[h] + jnp.log2(den), (HEAD_DIM, tq)))
        rows = pl.ds(r + j * tq * d, tq, stride=d)
        o_ref[rows, :] = jnp.concatenate(o_t, axis=0).T
        lse_ref[rows, :] = jnp.concatenate(l_t, axis=0).T

    pending = None
    for j in range(nq):
        for r in range(d):
            s = scores(j, r)
            if pending is not None:
                finish(*pending)
            pending = (j, r, s)
    finish(*pending)


def _dilated(qb, kb, vb, g, block_tokens):
    B, S, _ = qb.shape
    d = DIL_PAIRS[g][1]
    tq = min(LANES, S // d)
    nq = max(block_tokens // (d * tq), 1)
    assert DIL_PAIRS[g][0] // d == DIL_SPAN and DIL_SPAN <= tq and S % (nq * d * tq) == 0
    gw = DIL_HPG * HEAD_DIM
    nb = gw // LANES
    cur = pl.BlockSpec((None, nq * d * tq, LANES), lambda b, i, c: (b, i, g * nb + c))
    prev = pl.BlockSpec((None, d * tq, LANES), lambda b, i, c: (b, jnp.maximum(i * nq - 1, 0), g * nb + c))
    out_blk = pl.BlockSpec((None, nq * d * tq, LANES), lambda b, i, c: (b, i, c))
    return pl.pallas_call(
        functools.partial(_dilated_kernel, d=d, tq=tq, nq=nq),
        grid=(B, S // (nq * d * tq), nb),
        in_specs=[cur, prev, cur, prev, cur],
        out_specs=[out_blk, out_blk],
        out_shape=[jax.ShapeDtypeStruct((B, S, gw), F32)] * 2,
        scratch_shapes=[pltpu.VMEM((nq * d, LANES + ONES_ROWS, 2 * tq), BF16)],
        compiler_params=pltpu.CompilerParams(dimension_semantics=("parallel", "arbitrary", "arbitrary"),
                                             vmem_limit_bytes=VMEM_LIMIT),
        name=f"dilated{g}",
    )(qb, kb, kb, vb, vb)


def _out_mlp_kernel(x_ref, ya_ref, o0_ref, o1_ref, o2_ref, l0_ref, l1_ref, l2_ref, gma_ref, gmb_ref,
                    woa_ref, wob_ref, wout_ref, g2_ref, wup_ref, wdn_ref, out_ref, *, fc):
    l0, l1, l2 = l0_ref[...], l1_ref[...], l2_ref[...]
    mx = jnp.maximum(jnp.maximum(l0, l1), l2)
    e0, e1, e2 = jnp.exp2(l0 - mx), jnp.exp2(l1 - mx), jnp.exp2(l2 - mx)
    yb = (e0 * o0_ref[...] + e1 * o1_ref[...] + e2 * o2_ref[...]) / (e0 + e1 + e2)
    ta = _dot(ya_ref[...], woa_ref[...])
    tb = _dot(yb.astype(BF16), wob_ref[...])
    mixed = gma_ref[...].astype(F32) * ta + gmb_ref[...].astype(F32) * tb
    x1 = x_ref[...] + _dot(mixed.astype(BF16), wout_ref[...])
    ms = jnp.mean(x1 * x1, axis=-1, keepdims=True)
    h = (x1 * lax.rsqrt(ms + NORM_EPS) * g2_ref[...]).astype(BF16)
    acc = x1
    for off, w in _chunks(wup_ref.shape[1], fc):
        u = jnp.maximum(_dot(h, wup_ref[:, off:off + w]), 0.0)
        acc = acc + _dot((u * u).astype(BF16), wdn_ref[off:off + w, :])
    out_ref[...] = acc


def _out_mlp(x, ya, dil, gma, gmb, w_o_a, w_o_b, w_out, norm2_g, w_up, w_down, tm, fc):
    B, S, D = x.shape
    row = lambda w: pl.BlockSpec((None, tm, w), lambda b, m: (b, m, 0))
    full = lambda a: pl.BlockSpec(a.shape, lambda b, m: (0, 0))
    gw = DIL_HPG * HEAD_DIM
    ws = [w.astype(BF16) for w in (w_o_a, w_o_b, w_out)]
    g2 = norm2_g.reshape(1, D)
    wu, wd = w_up.astype(BF16), w_down.astype(BF16)
    return pl.pallas_call(
        functools.partial(_out_mlp_kernel, fc=fc),
        grid=(B, S // tm),
        in_specs=([row(D), row(ya.shape[-1])] + [row(gw)] * 6 + [row(D), row(D)]
                  + [full(w) for w in ws] + [full(g2), full(wu), full(wd)]),
        out_specs=row(D),
        out_shape=jax.ShapeDtypeStruct((B, S, D), F32),
        compiler_params=pltpu.CompilerParams(dimension_semantics=("parallel", "parallel"),
                                             vmem_limit_bytes=VMEM_LIMIT),
        name="out_mlp",
    )(x, ya, dil[0][0], dil[1][0], dil[2][0], dil[0][1], dil[1][1], dil[2][1], gma, gmb, *ws, g2, wu, wd)


def kernel(x, norm1_g, w_in, q_norm_a, k_norm_cmp, k_norm_slc, k_norm_win, cmp_k_pos, cmp_k_w1, cmp_k_w2,
           cmp_v_pos, cmp_v_w1, cmp_v_w2, q_norm_b, k_norm_b, w_o_a, w_o_b, w_out, norm2_g, w_up, w_down):
    depth = w_in.shape[0]
    for i in range(depth):
        (qa, ks, kw, qb, kb, kvc_raw, vb, gma, gmb, gns, vst, vwt) = _inproj(
            x, norm1_g[i], w_in[i], q_norm_a[i], k_norm_slc[i], k_norm_win[i], q_norm_b[i], k_norm_b[i], tm=INPROJ_ROWS)
        kc, vct = _compress(kvc_raw, k_norm_cmp[i], cmp_k_pos[i], cmp_k_w1[i], cmp_k_w2[i],
                           cmp_v_pos[i], cmp_v_w1[i], cmp_v_w2[i])
        ya = _nsa(qa, kc, vct, ks, vst, kw, vwt, gns, tq=NSA_TQ, tk=NSA_TK, nsub=NSA_TILES_PER_STEP)
        dil = [_dilated(qb, kb, vb, g, block_tokens=DIL_BLOCK_TOKENS) for g in range(DIL_GROUPS)]
        x = _out_mlp(x, ya, dil, gma, gmb, w_o_a[i], w_o_b[i], w_out[i], norm2_g[i], w_up[i], w_down[i],
                     tm=OUT_MLP_ROWS, fc=MLP_FF_CHUNK)
    return x
```

```python
import functools

import numpy as np
import jax
import jax.numpy as jnp
from jax import lax
from jax.experimental import pallas as pl
from jax.experimental.pallas import tpu as pltpu

HEAD_DIM = 64
HALF = HEAD_DIM // 2
ROPE_THETA = 10000.0
NORM_EPS = 1e-6
NEG = -1e30
NSA_HEADS = 12
NSA_GROUPS = 3
NSA_HPG = NSA_HEADS // NSA_GROUPS
CMP_BLOCK = 32
CMP_STRIDE = 16
SLC_BLOCK = 64
SLC_TOPK = 16
WIN_SIZE = 512
DIL_PAIRS = ((128, 1), (512, 4), (2048, 16))
DIL_GROUPS = len(DIL_PAIRS)
DIL_HPG = 4
DIL_HEADS = DIL_GROUPS * DIL_HPG
DIL_SPAN = 128
LOG2E = float(np.log2(np.e))
ONES_ROWS = 16

LANES = 128
MXU_N = 256
VMEM_LIMIT = 56 * 1024 * 1024

BF16 = jnp.bfloat16
F32 = jnp.float32

INPROJ_ROWS = 512
OUT_MLP_ROWS = 512
MLP_FF_CHUNK = 1024
NSA_TQ = 2 * SLC_BLOCK
NSA_TK = 512
NSA_TILES_PER_STEP = 8
DIL_BLOCK_TOKENS = 2048

D_MODEL = 1024
Q_A, KV_A, GATE_A, QKV_B = NSA_HEADS * HEAD_DIM, NSA_GROUPS * HEAD_DIM, 3 * NSA_HEADS, DIL_HEADS * HEAD_DIM
IN_NAMES = ("qa", "kc", "vc", "ks", "vs", "kw", "vw", "gns", "qb", "kb", "vb", "gma", "gmb")
IN_SIZES = (Q_A, KV_A, KV_A, KV_A, KV_A, KV_A, KV_A, GATE_A, QKV_B, QKV_B, QKV_B, D_MODEL, D_MODEL)
ROPE_SEGS = (("qa", Q_A), ("qb", QKV_B), ("kb", QKV_B), ("ks", KV_A), ("kw", KV_A))
PLAIN_SEGS = (("kvc", 2 * KV_A), ("vb", QKV_B))
SIG_SEGS = (("gma", D_MODEL), ("gmb", D_MODEL), ("gns", NSA_GROUPS * LANES))
ROPE_W = sum(w for _, w in ROPE_SEGS)
PLAIN_W = sum(w for _, w in PLAIN_SEGS)
SIG_W = sum(w for _, w in SIG_SEGS)
HEAD_MAJOR = ("qa", "ks", "kw")


def _dot(a, b):
    return jnp.dot(a, b, preferred_element_type=F32)


def _dot_nt(a, b):
    return lax.dot_general(a, b, (((1,), (1,)), ((), ())), preferred_element_type=F32)


def _chunks(total, width):
    out, off = [], 0
    while off < total:
        w = min(width, total - off)
        out.append((off, w))
        off += w
    return out


def _seg_lookup(segs, col):
    off = 0
    for name, w in segs:
        if col < off + w:
            return name, col - off
        off += w
    raise ValueError(col)


def _inproj_kernel(x_ref, g1_ref, w_ref, wvt_ref, gain_ref, bd_ref, cos_ref, sin_ref,
                   qa_ref, ks_ref, kw_ref, qb_ref, kb_ref,
                   kvc_ref, vb_ref, gma_ref, gmb_ref, gns_ref, vst_ref, vwt_ref):
    outs = dict(qa=qa_ref, ks=ks_ref, kw=kw_ref, qb=qb_ref, kb=kb_ref, kvc=kvc_ref,
                vb=vb_ref, gma=gma_ref, gmb=gmb_ref, gns=gns_ref)

    def emit(segs, col, val):
        name, rel = _seg_lookup(segs, col)
        ref = outs[name]
        if name in HEAD_MAJOR:
            for p in range(LANES // HEAD_DIM):
                nm, r = _seg_lookup(segs, col + p * HEAD_DIM)
                outs[nm][r // HEAD_DIM] = val[:, p * HEAD_DIM:(p + 1) * HEAD_DIM].astype(outs[nm].dtype)
        else:
            nm2, _ = _seg_lookup(segs, col + HEAD_DIM)
            if nm2 != name:
                raise ValueError("unaligned dense segment")
            ref[:, rel:rel + LANES] = val.astype(ref.dtype)

    x = x_ref[...]
    ms = jnp.mean(x * x, axis=-1, keepdims=True)
    h = (x * lax.rsqrt(ms + NORM_EPS) * g1_ref[...]).astype(BF16)

    lane = lax.broadcasted_iota(jnp.int32, (x.shape[0], LANES), 1)
    first_half = (lane % HEAD_DIM) < HALF
    cos = cos_ref[...]
    sin = sin_ref[...]
    bd = bd_ref[...]

    def rope_epilogue(off, w, y):
        msq = _dot((y * y).astype(BF16), bd[:w, :w])
        yn = y * lax.rsqrt(msq + NORM_EPS) * gain_ref[:, off:off + w]
        for u in range(w // LANES):
            v = yn[:, u * LANES:(u + 1) * LANES]
            rot = jnp.where(first_half, pltpu.roll(v, LANES - HALF, 1), pltpu.roll(v, HALF, 1))
            emit(ROPE_SEGS, off + u * LANES, v * cos + rot * sin)

    def plain_epilogue(off, w, y):
        for u in range(w // LANES):
            emit(PLAIN_SEGS, off + u * LANES, y[:, u * LANES:(u + 1) * LANES])

    def sig_epilogue(off, w, y):
        y = jax.nn.sigmoid(y)
        for u in range(w // LANES):
            emit(SIG_SEGS, off + u * LANES, y[:, u * LANES:(u + 1) * LANES])

    work = ([(0, off, w, rope_epilogue) for off, w in _chunks(ROPE_W, MXU_N)]
            + [(ROPE_W, off, w, plain_epilogue) for off, w in _chunks(PLAIN_W, MXU_N)]
            + [(ROPE_W + PLAIN_W, off, w, sig_epilogue) for off, w in _chunks(SIG_W, MXU_N)])
    pending = None
    for base, off, w, epilogue in work:
        y = _dot(h, w_ref[:, base + off:base + off + w])
        if pending is not None:
            pending[0](pending[1], pending[2], pending[3])
        pending = (epilogue, off, w, y)
    pending[0](pending[1], pending[2], pending[3])

    v_t = _dot_nt(wvt_ref[...], h)
    ones = jnp.ones((ONES_ROWS, x.shape[0]), vst_ref.dtype)
    for i, ref in enumerate((vst_ref, vwt_ref)):
        for g in range(NSA_GROUPS):
            r0 = (i * NSA_GROUPS + g) * HEAD_DIM
            ref[g, 0:HEAD_DIM, :] = v_t[r0:r0 + HEAD_DIM].astype(ref.dtype)
            ref[g, HEAD_DIM:HEAD_DIM + ONES_ROWS, :] = ones


def _rope_tables(positions, width):
    inv_freq = np.power(ROPE_THETA, -np.arange(HALF, dtype=np.float64) / HALF)
    ang = np.asarray(positions, np.float64)[:, None] * inv_freq[None, :]
    reps = width // HEAD_DIM
    cos = np.tile(np.concatenate([np.cos(ang), np.cos(ang)], axis=1), (1, reps))
    sin = np.tile(np.concatenate([-np.sin(ang), np.sin(ang)], axis=1), (1, reps))
    return jnp.asarray(cos, F32), jnp.asarray(sin, F32)


def _inproj(x, norm1_g, w_in, q_norm_a, k_norm_slc, k_norm_win, q_norm_b, k_norm_b, tm):
    B, S, D = x.shape
    scale = HEAD_DIM ** -0.5
    assert D == D_MODEL and w_in.shape == (D, sum(IN_SIZES))
    sp = np.cumsum((0,) + IN_SIZES)
    w_bf = w_in.astype(BF16)
    col = {n: w_bf[:, int(sp[i]):int(sp[i + 1])] for i, n in enumerate(IN_NAMES)}
    gpg = GATE_A // NSA_GROUPS
    gns = jnp.concatenate(
        [jnp.pad(col["gns"][:, g * gpg:(g + 1) * gpg], ((0, 0), (0, LANES - gpg))) for g in range(NSA_GROUPS)],
        axis=1)
    col["gns"] = gns
    col["kvc"] = jnp.concatenate([col["kc"], col["vc"]], axis=1)
    w_p = jnp.concatenate([col[n] for n, _ in ROPE_SEGS + PLAIN_SEGS + SIG_SEGS], axis=1)
    wvt = jnp.concatenate([col["vs"], col["vw"]], axis=1).T
    ncol = ROPE_W + PLAIN_W + SIG_W
    assert w_p.shape == (D, ncol)

    gain = jnp.concatenate([
        jnp.tile(q_norm_a * (scale * LOG2E), NSA_HEADS), jnp.tile(q_norm_b * (scale * LOG2E), DIL_HEADS),
        jnp.tile(k_norm_b, DIL_HEADS), jnp.tile(k_norm_slc, NSA_GROUPS), jnp.tile(k_norm_win, NSA_GROUPS)
    ]).reshape(1, ROPE_W).astype(F32)
    bd = jnp.asarray(np.kron(np.eye(MXU_N // HEAD_DIM), np.full((HEAD_DIM, HEAD_DIM), 1.0 / HEAD_DIM)), BF16)
    cos, sin = _rope_tables(np.arange(S), LANES)

    def hm(nh, dt):
        return (jax.ShapeDtypeStruct((B, nh, S, HEAD_DIM), dt),
                pl.BlockSpec((None, nh, tm, HEAD_DIM), lambda b, m: (b, 0, m, 0)))

    def vt():
        rows = HEAD_DIM + ONES_ROWS
        return (jax.ShapeDtypeStruct((B, NSA_GROUPS, rows, S), BF16),
                pl.BlockSpec((None, NSA_GROUPS, rows, tm), lambda b, m: (b, 0, 0, m)))

    def dense(w, dt):
        return (jax.ShapeDtypeStruct((B, S, w), dt), pl.BlockSpec((None, tm, w), lambda b, m: (b, m, 0)))

    outs = [hm(NSA_HEADS, BF16), hm(NSA_GROUPS, BF16), hm(NSA_GROUPS, BF16),
            dense(QKV_B, F32), dense(QKV_B, F32),
            dense(2 * KV_A, F32),
            dense(QKV_B, F32),
            dense(D, BF16), dense(D, BF16), dense(NSA_GROUPS * LANES, F32),
            vt(), vt()]
    const = lambda b, m: (0, 0)
    return pl.pallas_call(
        _inproj_kernel,
        grid=(B, S // tm),
        in_specs=[pl.BlockSpec((None, tm, D), lambda b, m: (b, m, 0)),
                  pl.BlockSpec((1, D), const),
                  pl.BlockSpec((D, ncol), const),
                  pl.BlockSpec(wvt.shape, const),
                  pl.BlockSpec((1, ROPE_W), const),
                  pl.BlockSpec((MXU_N, MXU_N), const),
                  pl.BlockSpec((tm, LANES), lambda b, m: (m, 0)),
                  pl.BlockSpec((tm, LANES), lambda b, m: (m, 0))],
        out_specs=[o[1] for o in outs],
        out_shape=[o[0] for o in outs],
        compiler_params=pltpu.CompilerParams(dimension_semantics=("parallel", "arbitrary"),
                                             vmem_limit_bytes=VMEM_LIMIT),
        name="inproj",
    )(x, norm1_g.reshape(1, D), w_p, wvt, gain, bd, cos, sin)


def _gelu_tanh(x):
    return 0.5 * x * (1.0 + jnp.tanh(np.sqrt(2.0 / np.pi) * (x + 0.044715 * (x * x * x))))


def _compress_kernel(x0_ref, x1_ref, x2_ref, pk_ref, pv_ref, w1k_ref, w2k_ref, w1v_ref, w2vt_ref,
                     gain_ref, cos_ref, sin_ref, kc_ref, vct_ref):
    nc = x0_ref.shape[0] // CMP_STRIDE
    half = CMP_STRIDE * HEAD_DIM
    per_blk = LANES // HEAD_DIM
    xs = [[x_ref[pl.ds(l, nc, stride=CMP_STRIDE), :] for l in range(CMP_STRIDE)] for x_ref in (x0_ref, x1_ref, x2_ref)]

    def hidden(head, pos_ref, w1_ref):
        blk, sub = divmod(head, per_blk)
        cols = slice(sub * HEAD_DIM, (sub + 1) * HEAD_DIM)
        x = [xs[blk][l][:, cols] for l in range(CMP_STRIDE)]
        xa = jnp.concatenate([x[l] + pos_ref[l:l + 1, :] for l in range(CMP_STRIDE)], axis=1)
        xb = jnp.concatenate([x[l] + pos_ref[CMP_STRIDE + l:CMP_STRIDE + l + 1, :] for l in range(CMP_STRIDE)], axis=1)
        a = _dot(xa.astype(BF16), w1_ref[0:half, :])
        b = _dot(xb.astype(BF16), w1_ref[half:2 * half, :])
        pre = a + pltpu.roll(b, nc - 1, 0)
        return _gelu_tanh(pre).astype(BF16)

    for g in range(NSA_GROUPS):
        kc = _dot(hidden(g, pk_ref, w1k_ref), w2k_ref[...])
        vct = _dot_nt(w2vt_ref[...], hidden(NSA_GROUPS + g, pv_ref, w1v_ref))
        ms = jnp.mean(kc * kc, axis=-1, keepdims=True)
        kn = kc * lax.rsqrt(ms + NORM_EPS) * gain_ref[...]
        rot = jnp.concatenate([kn[:, HALF:], kn[:, :HALF]], axis=-1)
        kc_ref[g] = (kn * cos_ref[...] + rot * sin_ref[...]).astype(kc_ref.dtype)
        vct_ref[g] = vct.astype(vct_ref.dtype)


def _compress(kvc_raw, k_norm_cmp, kpos, kw1, kw2, vpos, vw1, vw2):
    B, S, w = kvc_raw.shape
    G, dh = NSA_GROUPS, HEAD_DIM
    assert w == 2 * G * dh == 3 * LANES
    nc = S // CMP_STRIDE
    cos, sin = _rope_tables(np.arange(nc) * CMP_STRIDE + CMP_BLOCK - 1, dh)
    const = lambda b: (0, 0)
    xblk = lambda j: pl.BlockSpec((None, S, LANES), lambda b: (b, 0, j))
    hid = kw1.shape[1]
    return pl.pallas_call(
        _compress_kernel,
        grid=(B,),
        in_specs=[xblk(0), xblk(1), xblk(2),
                  pl.BlockSpec((CMP_BLOCK, dh), const), pl.BlockSpec((CMP_BLOCK, dh), const),
                  pl.BlockSpec((CMP_BLOCK * dh, hid), const), pl.BlockSpec((hid, dh), const),
                  pl.BlockSpec((CMP_BLOCK * dh, hid), const), pl.BlockSpec((dh, hid), const),
                  pl.BlockSpec((1, dh), const), pl.BlockSpec((nc, dh), const), pl.BlockSpec((nc, dh), const)],
        out_specs=[pl.BlockSpec((None, G, nc, dh), lambda b: (b, 0, 0, 0)),
                   pl.BlockSpec((None, G, dh, nc), lambda b: (b, 0, 0, 0))],
        out_shape=[jax.ShapeDtypeStruct((B, G, nc, dh), BF16), jax.ShapeDtypeStruct((B, G, dh, nc), BF16)],
        compiler_params=pltpu.CompilerParams(dimension_semantics=("parallel",), vmem_limit_bytes=VMEM_LIMIT),
        name="compress",
    )(kvc_raw, kvc_raw, kvc_raw, kpos, vpos, kw1.astype(BF16), kw2.astype(BF16), vw1.astype(BF16),
      vw2.T.astype(BF16), k_norm_cmp.reshape(1, dh).astype(F32), cos, sin)


def _nsa_front(q_ref, kc_ref, vct_ref, ks_ref, vst_ref, kw_ref, vwt_ref, ovt_ref, bias_ref,
               *, sub, nsub, tq, n_cmp, n_slc, seq):
    R = NSA_HPG
    qt = pl.program_id(2) * nsub + sub
    t0 = qt * tq
    t0a = pl.multiple_of(t0, tq)
    q2 = q_ref[:, sub * tq:(sub + 1) * tq, :].reshape(R * tq, HEAD_DIM)
    ncp = kc_ref.shape[0]
    t_lane = t0 + lax.broadcasted_iota(jnp.int32, (1, tq), 1)
    span = min(WIN_SIZE + tq, seq)
    start = pl.multiple_of(jnp.maximum(t0 - WIN_SIZE, 0), tq)

    def heads(a):
        return [a[:, r * tq:(r + 1) * tq] for r in range(R)]

    s_c = _dot_nt(kc_ref[...], q2)
    s_d = _dot_nt(ks_ref[pl.ds(t0a, tq), :], q2)
    s_w = _dot_nt(kw_ref[pl.ds(start, span), :], q2)

    c_idx = lax.broadcasted_iota(jnp.int32, (ncp, tq), 0)
    cmask = ((c_idx * CMP_STRIDE + (CMP_BLOCK - 1)) <= t_lane) & (c_idx < n_cmp)
    ps = []
    for sr in heads(s_c):
        sr = jnp.where(cmask, sr, NEG)
        m = jnp.max(sr, axis=0, keepdims=True)
        e = jnp.exp2(sr - m)
        den = jnp.sum(e, axis=0, keepdims=True)
        ps.append(e * jnp.where(m > 0.5 * NEG, 1.0 / den, 0.0))
    o_cmp = _dot(vct_ref[...], jnp.concatenate(ps, axis=1).astype(BF16))

    psum = ps[0]
    for r in range(1, R):
        psum = psum + ps[r]
    p_hi = psum.astype(BF16)
    p_lo = (psum - p_hi.astype(F32)).astype(BF16)
    nb = -(-n_slc // 8) * 8
    imp = (_dot(ovt_ref[...], p_hi) + _dot(ovt_ref[...], p_lo))[:nb]
    n_b = lax.broadcasted_iota(jnp.int32, (nb, tq), 0)
    cur = (t0 + lax.broadcasted_iota(jnp.int32, (nb, tq), 1)) // SLC_BLOCK
    forced = (n_b == 0) | (n_b == cur) | (n_b == cur - 1)
    visible = n_b <= cur

    top_n = min(SLC_TOPK, n_slc)
    n_f = n_b.astype(F32)
    work = jnp.where(forced, -jnp.inf, jnp.where(visible, imp, -jnp.inf))
    picked = jnp.zeros((nb, tq), F32)
    for _ in range(max(top_n - 3, 0)):
        mx = jnp.max(work, axis=0, keepdims=True)
        first = jnp.min(jnp.where(work == mx, n_f, float(nb)), axis=0, keepdims=True)
        hit = n_f == jnp.where(mx > -jnp.inf, first, -1.0)
        picked = jnp.where(hit, 1.0, picked)
        work = jnp.where(hit, -jnp.inf, work)
    chosen = jnp.where(forced, 1.0, jnp.where(cur < top_n, 1.0, picked))
    live = jnp.where(visible, jnp.where(n_b < t0 // SLC_BLOCK, chosen, 0.0), 0.0)
    bias_ref[sub, :nb] = jnp.where(live > 0.5, 0.0, NEG)
    if nb < LANES:
        bias_ref[sub, nb:] = jnp.full((LANES - nb, tq), NEG, F32)

    tri = lax.broadcasted_iota(jnp.int32, (tq, tq), 0) <= lax.broadcasted_iota(jnp.int32, (tq, tq), 1)
    m_d, p_d = [], []
    for sr in heads(s_d):
        sr = jnp.where(tri, sr, NEG)
        m = jnp.max(sr, axis=0, keepdims=True)
        m_d.append(m)
        p_d.append(jnp.exp2(sr - m).astype(BF16))
    acc0 = _dot(vst_ref[:, pl.ds(t0a, tq)], jnp.concatenate(p_d, axis=1))

    pw = []
    if sub * tq >= WIN_SIZE and span == WIN_SIZE + tq:
        for sr in heads(s_w):
            old = jnp.where(tri, NEG, sr[:tq])
            mid = sr[tq:WIN_SIZE]
            new = jnp.where(tri, sr[WIN_SIZE:], NEG)
            m = jnp.maximum(jnp.maximum(jnp.max(old, axis=0, keepdims=True), jnp.max(mid, axis=0, keepdims=True)),
                            jnp.max(new, axis=0, keepdims=True))
            pw.append(jnp.concatenate([jnp.exp2(old - m), jnp.exp2(mid - m), jnp.exp2(new - m)],
                                      axis=0).astype(BF16))
    else:
        diff = t_lane - (start + lax.broadcasted_iota(jnp.int32, (span, tq), 0))
        wmask = (diff >= 0) & (diff < WIN_SIZE)
        for sr in heads(s_w):
            sr = jnp.where(wmask, sr, NEG)
            m = jnp.max(sr, axis=0, keepdims=True)
            pw.append(jnp.exp2(sr - m).astype(BF16))
    acc_w = _dot(vwt_ref[:, pl.ds(start, span)], jnp.concatenate(pw, axis=1))
    o_win = acc_w[:HEAD_DIM] / acc_w[HEAD_DIM:HEAD_DIM + 1]
    return q2, o_cmp, o_win, jnp.concatenate(m_d, axis=1), acc0


def _nsa_kernel(q_ref, kc_ref, vct_ref, ks_ref, vst_ref, kw_ref, vwt_ref, gate_ref, ovt_ref,
                o_ref, bias_ref, s_ref, m_ref, acc_ref, *, nsub, tq, tk, n_cmp, n_slc, seq):
    R = NSA_HPG
    W = R * tq
    bpt = tk // SLC_BLOCK
    last_tile = seq // tk - 1
    step = pl.program_id(2)
    fronts = [_nsa_front(q_ref, kc_ref, vct_ref, ks_ref, vst_ref, kw_ref, vwt_ref, ovt_ref, bias_ref,
                         sub=sub, nsub=nsub, tq=tq, n_cmp=n_cmp, n_slc=n_slc, seq=seq) for sub in range(nsub)]
    q_all = jnp.concatenate([f[0] for f in fronts], axis=0)

    def qk(kt, slot, first_sub=0):
        k0 = pl.multiple_of(jnp.minimum(kt, last_tile) * tk, tk)
        s_ref[slot, :, first_sub * W:] = _dot_nt(ks_ref[pl.ds(k0, tk), :], q_all[first_sub * W:])

    def update(kt, slot, first_sub=0, span_tile=None):
        k0 = pl.multiple_of(kt * tk, tk)
        v_t = vst_ref[:, pl.ds(k0, tk)]
        for sub in range(first_sub, nsub):
            nblk = bpt if span_tile is None else min(bpt, (sub * tq - span_tile * tk) // SLC_BLOCK)
            brows = [bias_ref[sub, pl.ds(kt * bpt + j, 1), :] for j in range(nblk)]
            p_all, alphas = [], []
            for r in range(R):
                c0 = sub * W + r * tq
                cols = slice(c0, c0 + tq)
                mo = m_ref[:, cols]
                m8 = None
                for j in range(nblk):
                    blk = s_ref[slot, j * SLC_BLOCK:(j + 1) * SLC_BLOCK, cols]
                    b8 = jnp.max(blk.reshape(SLC_BLOCK // 8, 8, tq), axis=0) + brows[j]
                    m8 = b8 if m8 is None else jnp.maximum(m8, b8)
                mn = jnp.maximum(mo, jnp.max(m8, axis=0, keepdims=True))
                p_all.append(jnp.concatenate(
                    [jnp.exp2(s_ref[slot, j * SLC_BLOCK:(j + 1) * SLC_BLOCK, cols] + (brows[j] - mn)).astype(BF16)
                     for j in range(nblk)], axis=0))
                alphas.append(jnp.exp2(mo - mn))
                m_ref[:, cols] = mn
            pv = _dot(v_t[:, :nblk * SLC_BLOCK], jnp.concatenate(p_all, axis=1))
            cols = slice(sub * W, (sub + 1) * W)
            acc_ref[:, cols] = jnp.concatenate(alphas, axis=1) * acc_ref[:, cols] + pv

    def pair_body(ii, _):
        a = 2 * ii
        qk(a + 1, 1)
        update(a, 0)
        qk(a + 2, 0)
        update(a + 1, 1)
        return 0

    for sub, f in enumerate(fronts):
        m_ref[:, sub * W:(sub + 1) * W] = f[3]
        acc_ref[:, sub * W:(sub + 1) * W] = f[4]
    span_tiles = nsub * tq // tk
    qk(0, 0)
    lax.fori_loop(0, step * span_tiles // 2, pair_body, 0)
    for j in range(span_tiles):
        kt = step * span_tiles + j
        slot = j % 2
        first_sub = j * tk // tq + 1
        if j + 1 < span_tiles:
            qk(kt + 1, 1 - slot, (j + 1) * tk // tq + 1)
        update(kt, slot, first_sub, span_tile=j)
    o_slc = acc_ref[:HEAD_DIM, :] / acc_ref[HEAD_DIM:HEAD_DIM + 1, :]

    for sub in range(nsub):
        rows = slice(sub * tq, (sub + 1) * tq)
        _, o_cmp, o_win, _, _ = fronts[sub]
        gate_t = gate_ref[rows, :].T
        ys = []
        for r in range(R):
            cols = slice(r * tq, (r + 1) * tq)
            ys.append(gate_t[3 * r:3 * r + 1] * o_cmp[:, cols]
                      + gate_t[3 * r + 1:3 * r + 2] * o_slc[:, sub * W + r * tq:sub * W + (r + 1) * tq]
                      + gate_t[3 * r + 2:3 * r + 3] * o_win[:, cols])
        o_ref[rows, :] = jnp.concatenate(ys, axis=0).T.astype(o_ref.dtype)


def _overlap_t(ncp, n_cmp, n_slc):
    cs = np.arange(ncp)[None, :] * CMP_STRIDE
    ss = np.arange(LANES)[:, None] * SLC_BLOCK
    ov = np.clip(np.minimum(cs + CMP_BLOCK, ss + SLC_BLOCK) - np.maximum(cs, ss), 0, None) / CMP_BLOCK
    ov = ov * (np.arange(ncp)[None, :] < n_cmp) * (np.arange(LANES)[:, None] < n_slc)
    return jnp.asarray(ov, BF16)


def _nsa(qa, kc, vct, ks, vst, kw, vwt, gns, tq, tk, nsub):
    B, H, S, dh = qa.shape
    ncp = kc.shape[2]
    n_cmp = (S - CMP_BLOCK) // CMP_STRIDE + 1
    n_slc = S // SLC_BLOCK
    assert n_slc <= LANES and S % (2 * tk) == 0 and tk % SLC_BLOCK == 0 and tq == 2 * SLC_BLOCK and S % (nsub * tq) == 0 and (nsub * tq) % (2 * tk) == 0
    ovt = _overlap_t(ncp, n_cmp, n_slc)
    k_c = pl.BlockSpec((None, None, ncp, dh), lambda b, g, t: (b, g, 0, 0))
    v_c = pl.BlockSpec((None, None, dh, ncp), lambda b, g, t: (b, g, 0, 0))
    k_s = pl.BlockSpec((None, None, S, dh), lambda b, g, t: (b, g, 0, 0))
    v_s = pl.BlockSpec((None, None, vst.shape[2], S), lambda b, g, t: (b, g, 0, 0))
    v_w = pl.BlockSpec((None, None, vwt.shape[2], S), lambda b, g, t: (b, g, 0, 0))
    const = lambda b, g, t: (0, 0)
    kern = functools.partial(_nsa_kernel, tq=tq, tk=tk, n_cmp=n_cmp, n_slc=n_slc, seq=S, nsub=nsub)
    return pl.pallas_call(
        kern,
        grid=(B, NSA_GROUPS, S // (nsub * tq)),
        in_specs=[pl.BlockSpec((None, NSA_HPG, nsub * tq, dh), lambda b, g, t: (b, g, t, 0)),
                  k_c, v_c, k_s, v_s, k_s, v_w,
                  pl.BlockSpec((None, nsub * tq, LANES), lambda b, g, t: (b, t, g)),
                  pl.BlockSpec((LANES, ncp), const)],
        out_specs=pl.BlockSpec((None, nsub * tq, NSA_HPG * dh), lambda b, g, t: (b, t, g)),
        out_shape=jax.ShapeDtypeStruct((B, S, H * dh), BF16),
        scratch_shapes=[pltpu.VMEM((nsub, LANES, tq), F32),
                        pltpu.VMEM((2, tk, nsub * NSA_HPG * tq), F32),
                        pltpu.VMEM((1, nsub * NSA_HPG * tq), F32),
                        pltpu.VMEM((vst.shape[2], nsub * NSA_HPG * tq), F32)],
        compiler_params=pltpu.CompilerParams(dimension_semantics=("parallel", "parallel", "arbitrary"),
                                             vmem_limit_bytes=VMEM_LIMIT),
        name="nsa",
    )(qa, kc, vct, ks, vst, kw, vwt, gns, ovt)


def _dilated_kernel(q_ref, kp_ref, kc_ref, vp_ref, vc_ref, o_ref, lse_ref, vt_ref, *, d, tq, nq):
    span = 2 * tq
    nh = LANES // HEAD_DIM
    row = lax.broadcasted_iota(jnp.int32, (span, tq), 0)
    lane = lax.broadcasted_iota(jnp.int32, (span, tq), 1)
    diff = (lane + tq) - row
    band = (diff >= 0) & (diff <= DIL_SPAN)
    has_prev = pl.program_id(1) > 0
    first = band & ((row >= tq) | has_prev)
    head_of_lane = lax.broadcasted_iota(jnp.int32, (tq, LANES), 1) // HEAD_DIM
    vt_ref[:, LANES:, :] = jnp.ones((nq * d, ONES_ROWS, span), vt_ref.dtype)

    def scores(j, r):
        rows = pl.ds(r + j * tq * d, tq, stride=d)
        qb = q_ref[rows, :]
        if j == 0:
            kb = jnp.concatenate([kp_ref[pl.ds(r, tq, stride=d), :], kc_ref[rows, :]], axis=0)
            vb = jnp.concatenate([vp_ref[pl.ds(r, tq, stride=d), :], vc_ref[rows, :]], axis=0)
        else:
            kv_rows = pl.ds(r + (j - 1) * tq * d, span, stride=d)
            kb, vb = kc_ref[kv_rows, :], vc_ref[kv_rows, :]
        vt_ref[j * d + r, 0:LANES, :] = vb.T.astype(BF16)
        q_bd = jnp.concatenate([jnp.where(head_of_lane == h, qb, 0.0) for h in range(nh)], axis=0).astype(BF16)
        return _dot_nt(kb.astype(BF16), q_bd)

    def finish(j, r, s):
        mask = first if j == 0 else band
        ms, ps = [], []
        for h in range(nh):
            sh = jnp.where(mask, s[:, h * tq:(h + 1) * tq], NEG)
            m = jnp.max(sh, axis=0, keepdims=True)
            ms.append(m)
            ps.append(jnp.exp2(sh - m).astype(BF16))
        pv = _dot(vt_ref[j * d + r], jnp.concatenate(ps, axis=1))
        o_t, l_t = [], []
        for h in range(nh):
            den = pv[LANES:LANES + 1, h * tq:(h + 1) * tq]
            o_t.append(pv[h * HEAD_DIM:(h + 1) * HEAD_DIM, h * tq:(h + 1) * tq] / den)
            l_t.append(jnp.broadcast_to(ms[h] + jnp.log2(den), (HEAD_DIM, tq)))
        rows = pl.ds(r + j * tq * d, tq, stride=d)
        o_ref[rows, :] = jnp.concatenate(o_t, axis=0).T
        lse_ref[rows, :] = jnp.concatenate(l_t, axis=0).T

    pending = None
    for j in range(nq):
        for r in range(d):
            s = scores(j, r)
            if pending is not None:
                finish(*pending)
            pending = (j, r, s)
    finish(*pending)


def _dilated(qb, kb, vb, g, block_tokens):
    B, S, _ = qb.shape
    d = DIL_PAIRS[g][1]
    tq = min(LANES, S // d)
    nq = max(block_tokens // (d * tq), 1)
    assert DIL_PAIRS[g][0] // d == DIL_SPAN and DIL_SPAN <= tq and S % (nq * d * tq) == 0
    gw = DIL_HPG * HEAD_DIM
    nb = gw // LANES
    cur = pl.BlockSpec((None, nq * d * tq, LANES), lambda b, i, c: (b, i, g * nb + c))
    prev = pl.BlockSpec((None, d * tq, LANES), lambda b, i, c: (b, jnp.maximum(i * nq - 1, 0), g * nb + c))
    out_blk = pl.BlockSpec((None, nq * d * tq, LANES), lambda b, i, c: (b, i, c))
    return pl.pallas_call(
        functools.partial(_dilated_kernel, d=d, tq=tq, nq=nq),
        grid=(B, S // (nq * d * tq), nb),
        in_specs=[cur, prev, cur, prev, cur],
        out_specs=[out_blk, out_blk],
        out_shape=[jax.ShapeDtypeStruct((B, S, gw), F32)] * 2,
        scratch_shapes=[pltpu.VMEM((nq * d, LANES + ONES_ROWS, 2 * tq), BF16)],
        compiler_params=pltpu.CompilerParams(dimension_semantics=("parallel", "arbitrary", "arbitrary"),
                                             vmem_limit_bytes=VMEM_LIMIT),
        name=f"dilated{g}",
    )(qb, kb, kb, vb, vb)


def _out_mlp_kernel(x_ref, ya_ref, o0_ref, o1_ref, o2_ref, l0_ref, l1_ref, l2_ref, gma_ref, gmb_ref,
                    woa_ref, wob_ref, wout_ref, g2_ref, wup_ref, wdn_ref, out_ref, *, fc):
    l0, l1, l2 = l0_ref[...], l1_ref[...], l2_ref[...]
    mx = jnp.maximum(jnp.maximum(l0, l1), l2)
    e0, e1, e2 = jnp.exp2(l0 - mx), jnp.exp2(l1 - mx), jnp.exp2(l2 - mx)
    yb = (e0 * o0_ref[...] + e1 * o1_ref[...] + e2 * o2_ref[...]) / (e0 + e1 + e2)
    ta = _dot(ya_ref[...], woa_ref[...])
    tb = _dot(yb.astype(BF16), wob_ref[...])
    mixed = gma_ref[...].astype(F32) * ta + gmb_ref[...].astype(F32) * tb
    x1 = x_ref[...] + _dot(mixed.astype(BF16), wout_ref[...])
    ms = jnp.mean(x1 * x1, axis=-1, keepdims=True)
    h = (x1 * lax.rsqrt(ms + NORM_EPS) * g2_ref[...]).astype(BF16)
    acc = x1
    for off, w in _chunks(wup_ref.shape[1], fc):
        u = jnp.maximum(_dot(h, wup_ref[:, off:off + w]), 0.0)
        acc = acc + _dot((u * u).astype(BF16), wdn_ref[off:off + w, :])
    out_ref[...] = acc


def _out_mlp(x, ya, dil, gma, gmb, w_o_a, w_o_b, w_out, norm2_g, w_up, w_down, tm, fc):
    B, S, D = x.shape
    row = lambda w: pl.BlockSpec((None, tm, w), lambda b, m: (b, m, 0))
    full = lambda a: pl.BlockSpec(a.shape, lambda b, m: (0, 0))
    gw = DIL_HPG * HEAD_DIM
    ws = [w.astype(BF16) for w in (w_o_a, w_o_b, w_out)]
    g2 = norm2_g.reshape(1, D)
    wu, wd = w_up.astype(BF16), w_down.astype(BF16)
    return pl.pallas_call(
        functools.partial(_out_mlp_kernel, fc=fc),
        grid=(B, S // tm),
        in_specs=([row(D), row(ya.shape[-1])] + [row(gw)] * 6 + [row(D), row(D)]
                  + [full(w) for w in ws] + [full(g2), full(wu), full(wd)]),
        out_specs=row(D),
        out_shape=jax.ShapeDtypeStruct((B, S, D), F32),
        compiler_params=pltpu.CompilerParams(dimension_semantics=("parallel", "parallel"),
                                             vmem_limit_bytes=VMEM_LIMIT),
        name="out_mlp",
    )(x, ya, dil[0][0], dil[1][0], dil[2][0], dil[0][1], dil[1][1], dil[2][1], gma, gmb, *ws, g2, wu, wd)


def kernel(x, norm1_g, w_in, q_norm_a, k_norm_cmp, k_norm_slc, k_norm_win, cmp_k_pos, cmp_k_w1, cmp_k_w2,
           cmp_v_pos, cmp_v_w1, cmp_v_w2, q_norm_b, k_norm_b, w_o_a, w_o_b, w_out, norm2_g, w_up, w_down):
    depth = w_in.shape[0]
    for i in range(depth):
        (qa, ks, kw, qb, kb, kvc_raw, vb, gma, gmb, gns, vst, vwt) = _inproj(
            x, norm1_g[i], w_in[i], q_norm_a[i], k_norm_slc[i], k_norm_win[i], q_norm_b[i], k_norm_b[i], tm=INPROJ_ROWS)
        kc, vct = _compress(kvc_raw, k_norm_cmp[i], cmp_k_pos[i], cmp_k_w1[i], cmp_k_w2[i],
                           cmp_v_pos[i], cmp_v_w1[i], cmp_v_w2[i])
        ya = _nsa(qa, kc, vct, ks, vst, kw, vwt, gns, tq=NSA_TQ, tk=NSA_TK, nsub=NSA_TILES_PER_STEP)
        dil = [_dilated(qb, kb, vb, g, block_tokens=DIL_BLOCK_TOKENS) for g in range(DIL_GROUPS)]
        x = _out_mlp(x, ya, dil, gma, gmb, w_o_a[i], w_o_b[i], w_out[i], norm2_g[i], w_up[i], w_down[i],
                     tm=OUT_MLP_ROWS, fc=MLP_FF_CHUNK)
    return x
```

```python
import functools

import numpy as np
import jax
import jax.numpy as jnp
from jax import lax
from jax.experimental import pallas as pl
from jax.experimental.pallas import tpu as pltpu

HEAD_DIM = 64
HALF = HEAD_DIM // 2
ROPE_THETA = 10000.0
NORM_EPS = 1e-6
NEG = -1e30
NSA_HEADS = 12
NSA_GROUPS = 3
NSA_HPG = NSA_HEADS // NSA_GROUPS
CMP_BLOCK = 32
CMP_STRIDE = 16
SLC_BLOCK = 64
SLC_TOPK = 16
WIN_SIZE = 512
DIL_PAIRS = ((128, 1), (512, 4), (2048, 16))
DIL_GROUPS = len(DIL_PAIRS)
DIL_HPG = 4
DIL_HEADS = DIL_GROUPS * DIL_HPG
DIL_SPAN = 128
LOG2E = float(np.log2(np.e))
ONES_ROWS = 16

LANES = 128
MXU_N = 256
VMEM_LIMIT = 56 * 1024 * 1024

BF16 = jnp.bfloat16
F32 = jnp.float32

INPROJ_ROWS = 512
OUT_MLP_ROWS = 512
MLP_FF_CHUNK = 1024
NSA_TQ = 2 * SLC_BLOCK
NSA_TK = 512
NSA_TILES_PER_STEP = 8
DIL_BLOCK_TOKENS = 4096

D_MODEL = 1024
Q_A, KV_A, GATE_A, QKV_B = NSA_HEADS * HEAD_DIM, NSA_GROUPS * HEAD_DIM, 3 * NSA_HEADS, DIL_HEADS * HEAD_DIM
IN_NAMES = ("qa", "kc", "vc", "ks", "vs", "kw", "vw", "gns", "qb", "kb", "vb", "gma", "gmb")
IN_SIZES = (Q_A, KV_A, KV_A, KV_A, KV_A, KV_A, KV_A, GATE_A, QKV_B, QKV_B, QKV_B, D_MODEL, D_MODEL)
ROPE_SEGS = (("qa", Q_A), ("qb", QKV_B), ("kb", QKV_B), ("ks", KV_A), ("kw", KV_A))
PLAIN_SEGS = (("kvc", 2 * KV_A), ("vb", QKV_B))
SIG_SEGS = (("gma", D_MODEL), ("gmb", D_MODEL), ("gns", NSA_GROUPS * LANES))
ROPE_W = sum(w for _, w in ROPE_SEGS)
PLAIN_W = sum(w for _, w in PLAIN_SEGS)
SIG_W = sum(w for _, w in SIG_SEGS)
HEAD_MAJOR = ("qa", "ks", "kw")


def _dot(a, b):
    return jnp.dot(a, b, preferred_element_type=F32)


def _dot_nt(a, b):
    return lax.dot_general(a, b, (((1,), (1,)), ((), ())), preferred_element_type=F32)


def _chunks(total, width):
    out, off = [], 0
    while off < total:
        w = min(width, total - off)
        out.append((off, w))
        off += w
    return out


def _seg_lookup(segs, col):
    off = 0
    for name, w in segs:
        if col < off + w:
            return name, col - off
        off += w
    raise ValueError(col)


def _inproj_kernel(x_ref, g1_ref, w_ref, wvt_ref, gain_ref, bd_ref, cos_ref, sin_ref,
                   qa_ref, ks_ref, kw_ref, qb_ref, kb_ref,
                   kvc_ref, vb_ref, gma_ref, gmb_ref, gns_ref, vst_ref, vwt_ref):
    outs = dict(qa=qa_ref, ks=ks_ref, kw=kw_ref, qb=qb_ref, kb=kb_ref, kvc=kvc_ref,
                vb=vb_ref, gma=gma_ref, gmb=gmb_ref, gns=gns_ref)

    def emit(segs, col, val):
        name, rel = _seg_lookup(segs, col)
        ref = outs[name]
        if name in HEAD_MAJOR:
            for p in range(LANES // HEAD_DIM):
                nm, r = _seg_lookup(segs, col + p * HEAD_DIM)
                outs[nm][r // HEAD_DIM] = val[:, p * HEAD_DIM:(p + 1) * HEAD_DIM].astype(outs[nm].dtype)
        else:
            nm2, _ = _seg_lookup(segs, col + HEAD_DIM)
            if nm2 != name:
                raise ValueError("unaligned dense segment")
            ref[:, rel:rel + LANES] = val.astype(ref.dtype)

    x = x_ref[...]
    ms = jnp.mean(x * x, axis=-1, keepdims=True)
    h = (x * lax.rsqrt(ms + NORM_EPS) * g1_ref[...]).astype(BF16)

    lane = lax.broadcasted_iota(jnp.int32, (x.shape[0], LANES), 1)
    first_half = (lane % HEAD_DIM) < HALF
    cos = cos_ref[...]
    sin = sin_ref[...]
    bd = bd_ref[...]

    def rope_epilogue(off, w, y):
        msq = _dot((y * y).astype(BF16), bd[:w, :w])
        yn = y * lax.rsqrt(msq + NORM_EPS) * gain_ref[:, off:off + w]
        for u in range(w // LANES):
            v = yn[:, u * LANES:(u + 1) * LANES]
            rot = jnp.where(first_half, pltpu.roll(v, LANES - HALF, 1), pltpu.roll(v, HALF, 1))
            emit(ROPE_SEGS, off + u * LANES, v * cos + rot * sin)

    def plain_epilogue(off, w, y):
        for u in range(w // LANES):
            emit(PLAIN_SEGS, off + u * LANES, y[:, u * LANES:(u + 1) * LANES])

    def sig_epilogue(off, w, y):
        y = jax.nn.sigmoid(y)
        for u in range(w // LANES):
            emit(SIG_SEGS, off + u * LANES, y[:, u * LANES:(u + 1) * LANES])

    work = ([(0, off, w, rope_epilogue) for off, w in _chunks(ROPE_W, MXU_N)]
            + [(ROPE_W, off, w, plain_epilogue) for off, w in _chunks(PLAIN_W, MXU_N)]
            + [(ROPE_W + PLAIN_W, off, w, sig_epilogue) for off, w in _chunks(SIG_W, MXU_N)])
    pending = None
    for base, off, w, epilogue in work:
        y = _dot(h, w_ref[:, base + off:base + off + w])
        if pending is not None:
            pending[0](pending[1], pending[2], pending[3])
        pending = (epilogue, off, w, y)
    pending[0](pending[1], pending[2], pending[3])

    v_t = _dot_nt(wvt_ref[...], h)
    ones = jnp.ones((ONES_ROWS, x.shape[0]), vst_ref.dtype)
    for i, ref in enumerate((vst_ref, vwt_ref)):
        for g in range(NSA_GROUPS):
            r0 = (i * NSA_GROUPS + g) * HEAD_DIM
            ref[g, 0:HEAD_DIM, :] = v_t[r0:r0 + HEAD_DIM].astype(ref.dtype)
            ref[g, HEAD_DIM:HEAD_DIM + ONES_ROWS, :] = ones


def _rope_tables(positions, width):
    inv_freq = np.power(ROPE_THETA, -np.arange(HALF, dtype=np.float64) / HALF)
    ang = np.asarray(positions, np.float64)[:, None] * inv_freq[None, :]
    reps = width // HEAD_DIM
    cos = np.tile(np.concatenate([np.cos(ang), np.cos(ang)], axis=1), (1, reps))
    sin = np.tile(np.concatenate([-np.sin(ang), np.sin(ang)], axis=1), (1, reps))
    return jnp.asarray(cos, F32), jnp.asarray(sin, F32)


def _inproj(x, norm1_g, w_in, q_norm_a, k_norm_slc, k_norm_win, q_norm_b, k_norm_b, tm):
    B, S, D = x.shape
    scale = HEAD_DIM ** -0.5
    assert D == D_MODEL and w_in.shape == (D, sum(IN_SIZES))
    sp = np.cumsum((0,) + IN_SIZES)
    w_bf = w_in.astype(BF16)
    col = {n: w_bf[:, int(sp[i]):int(sp[i + 1])] for i, n in enumerate(IN_NAMES)}
    gpg = GATE_A // NSA_GROUPS
    gns = jnp.concatenate(
        [jnp.pad(col["gns"][:, g * gpg:(g + 1) * gpg], ((0, 0), (0, LANES - gpg))) for g in range(NSA_GROUPS)],
        axis=1)
    col["gns"] = gns
    col["kvc"] = jnp.concatenate([col["kc"], col["vc"]], axis=1)
    w_p = jnp.concatenate([col[n] for n, _ in ROPE_SEGS + PLAIN_SEGS + SIG_SEGS], axis=1)
    wvt = jnp.concatenate([col["vs"], col["vw"]], axis=1).T
    ncol = ROPE_W + PLAIN_W + SIG_W
    assert w_p.shape == (D, ncol)

    gain = jnp.concatenate([
        jnp.tile(q_norm_a * (scale * LOG2E), NSA_HEADS), jnp.tile(q_norm_b * (scale * LOG2E), DIL_HEADS),
        jnp.tile(k_norm_b, DIL_HEADS), jnp.tile(k_norm_slc, NSA_GROUPS), jnp.tile(k_norm_win, NSA_GROUPS)
    ]).reshape(1, ROPE_W).astype(F32)
    bd = jnp.asarray(np.kron(np.eye(MXU_N // HEAD_DIM), np.full((HEAD_DIM, HEAD_DIM), 1.0 / HEAD_DIM)), BF16)
    cos, sin = _rope_tables(np.arange(S), LANES)

    def hm(nh, dt):
        return (jax.ShapeDtypeStruct((B, nh, S, HEAD_DIM), dt),
                pl.BlockSpec((None, nh, tm, HEAD_DIM), lambda b, m: (b, 0, m, 0)))

    def vt():
        rows = HEAD_DIM + ONES_ROWS
        return (jax.ShapeDtypeStruct((B, NSA_GROUPS, rows, S), BF16),
                pl.BlockSpec((None, NSA_GROUPS, rows, tm), lambda b, m: (b, 0, 0, m)))

    def dense(w, dt):
        return (jax.ShapeDtypeStruct((B, S, w), dt), pl.BlockSpec((None, tm, w), lambda b, m: (b, m, 0)))

    outs = [hm(NSA_HEADS, BF16), hm(NSA_GROUPS, BF16), hm(NSA_GROUPS, BF16),
            dense(QKV_B, F32), dense(QKV_B, F32),
            dense(2 * KV_A, F32),
            dense(QKV_B, F32),
            dense(D, BF16), dense(D, BF16), dense(NSA_GROUPS * LANES, F32),
            vt(), vt()]
    const = lambda b, m: (0, 0)
    return pl.pallas_call(
        _inproj_kernel,
        grid=(B, S // tm),
        in_specs=[pl.BlockSpec((None, tm, D), lambda b, m: (b, m, 0)),
                  pl.BlockSpec((1, D), const),
                  pl.BlockSpec((D, ncol), const),
                  pl.BlockSpec(wvt.shape, const),
                  pl.BlockSpec((1, ROPE_W), const),
                  pl.BlockSpec((MXU_N, MXU_N), const),
                  pl.BlockSpec((tm, LANES), lambda b, m: (m, 0)),
                  pl.BlockSpec((tm, LANES), lambda b, m: (m, 0))],
        out_specs=[o[1] for o in outs],
        out_shape=[o[0] for o in outs],
        compiler_params=pltpu.CompilerParams(dimension_semantics=("parallel", "arbitrary"),
                                             vmem_limit_bytes=VMEM_LIMIT),
        name="inproj",
    )(x, norm1_g.reshape(1, D), w_p, wvt, gain, bd, cos, sin)


def _gelu_tanh(x):
    return 0.5 * x * (1.0 + jnp.tanh(np.sqrt(2.0 / np.pi) * (x + 0.044715 * (x * x * x))))


def _compress_kernel(x0_ref, x1_ref, x2_ref, pk_ref, pv_ref, w1k_ref, w2k_ref, w1v_ref, w2vt_ref,
                     gain_ref, cos_ref, sin_ref, kc_ref, vct_ref):
    nc = x0_ref.shape[0] // CMP_STRIDE
    half = CMP_STRIDE * HEAD_DIM
    per_blk = LANES // HEAD_DIM
    xs = [[x_ref[pl.ds(l, nc, stride=CMP_STRIDE), :] for l in range(CMP_STRIDE)] for x_ref in (x0_ref, x1_ref, x2_ref)]

    def hidden(head, pos_ref, w1_ref):
        blk, sub = divmod(head, per_blk)
        cols = slice(sub * HEAD_DIM, (sub + 1) * HEAD_DIM)
        x = [xs[blk][l][:, cols] for l in range(CMP_STRIDE)]
        xa = jnp.concatenate([x[l] + pos_ref[l:l + 1, :] for l in range(CMP_STRIDE)], axis=1)
        xb = jnp.concatenate([x[l] + pos_ref[CMP_STRIDE + l:CMP_STRIDE + l + 1, :] for l in range(CMP_STRIDE)], axis=1)
        a = _dot(xa.astype(BF16), w1_ref[0:half, :])
        b = _dot(xb.astype(BF16), w1_ref[half:2 * half, :])
        pre = a + pltpu.roll(b, nc - 1, 0)
        return _gelu_tanh(pre).astype(BF16)

    for g in range(NSA_GROUPS):
        kc = _dot(hidden(g, pk_ref, w1k_ref), w2k_ref[...])
        vct = _dot_nt(w2vt_ref[...], hidden(NSA_GROUPS + g, pv_ref, w1v_ref))
        ms = jnp.mean(kc * kc, axis=-1, keepdims=True)
        kn = kc * lax.rsqrt(ms + NORM_EPS) * gain_ref[...]
        rot = jnp.concatenate([kn[:, HALF:], kn[:, :HALF]], axis=-1)
        kc_ref[g] = (kn * cos_ref[...] + rot * sin_ref[...]).astype(kc_ref.dtype)
        vct_ref[g] = vct.astype(vct_ref.dtype)


def _compress(kvc_raw, k_norm_cmp, kpos, kw1, kw2, vpos, vw1, vw2):
    B, S, w = kvc_raw.shape
    G, dh = NSA_GROUPS, HEAD_DIM
    assert w == 2 * G * dh == 3 * LANES
    nc = S // CMP_STRIDE
    cos, sin = _rope_tables(np.arange(nc) * CMP_STRIDE + CMP_BLOCK - 1, dh)
    const = lambda b: (0, 0)
    xblk = lambda j: pl.BlockSpec((None, S, LANES), lambda b: (b, 0, j))
    hid = kw1.shape[1]
    return pl.pallas_call(
        _compress_kernel,
        grid=(B,),
        in_specs=[xblk(0), xblk(1), xblk(2),
                  pl.BlockSpec((CMP_BLOCK, dh), const), pl.BlockSpec((CMP_BLOCK, dh), const),
                  pl.BlockSpec((CMP_BLOCK * dh, hid), const), pl.BlockSpec((hid, dh), const),
                  pl.BlockSpec((CMP_BLOCK * dh, hid), const), pl.BlockSpec((dh, hid), const),
                  pl.BlockSpec((1, dh), const), pl.BlockSpec((nc, dh), const), pl.BlockSpec((nc, dh), const)],
        out_specs=[pl.BlockSpec((None, G, nc, dh), lambda b: (b, 0, 0, 0)),
                   pl.BlockSpec((None, G, dh, nc), lambda b: (b, 0, 0, 0))],
        out_shape=[jax.ShapeDtypeStruct((B, G, nc, dh), BF16), jax.ShapeDtypeStruct((B, G, dh, nc), BF16)],
        compiler_params=pltpu.CompilerParams(dimension_semantics=("parallel",), vmem_limit_bytes=VMEM_LIMIT),
        name="compress",
    )(kvc_raw, kvc_raw, kvc_raw, kpos, vpos, kw1.astype(BF16), kw2.astype(BF16), vw1.astype(BF16),
      vw2.T.astype(BF16), k_norm_cmp.reshape(1, dh).astype(F32), cos, sin)


def _nsa_front(q_ref, kc_ref, vct_ref, ks_ref, vst_ref, kw_ref, vwt_ref, ovt_ref, bias_ref,
               *, sub, nsub, tq, n_cmp, n_slc, seq):
    R = NSA_HPG
    qt = pl.program_id(2) * nsub + sub
    t0 = qt * tq
    t0a = pl.multiple_of(t0, tq)
    q2 = q_ref[:, sub * tq:(sub + 1) * tq, :].reshape(R * tq, HEAD_DIM)
    ncp = kc_ref.shape[0]
    t_lane = t0 + lax.broadcasted_iota(jnp.int32, (1, tq), 1)
    span = min(WIN_SIZE + tq, seq)
    start = pl.multiple_of(jnp.maximum(t0 - WIN_SIZE, 0), tq)

    def heads(a):
        return [a[:, r * tq:(r + 1) * tq] for r in range(R)]

    s_c = _dot_nt(kc_ref[...], q2)
    s_d = _dot_nt(ks_ref[pl.ds(t0a, tq), :], q2)
    s_w = _dot_nt(kw_ref[pl.ds(start, span), :], q2)

    c_idx = lax.broadcasted_iota(jnp.int32, (ncp, tq), 0)
    cmask = ((c_idx * CMP_STRIDE + (CMP_BLOCK - 1)) <= t_lane) & (c_idx < n_cmp)
    ps = []
    for sr in heads(s_c):
        sr = jnp.where(cmask, sr, NEG)
        m = jnp.max(sr, axis=0, keepdims=True)
        e = jnp.exp2(sr - m)
        den = jnp.sum(e, axis=0, keepdims=True)
        ps.append(e * jnp.where(m > 0.5 * NEG, 1.0 / den, 0.0))
    o_cmp = _dot(vct_ref[...], jnp.concatenate(ps, axis=1).astype(BF16))

    psum = ps[0]
    for r in range(1, R):
        psum = psum + ps[r]
    p_hi = psum.astype(BF16)
    p_lo = (psum - p_hi.astype(F32)).astype(BF16)
    nb = -(-n_slc // 8) * 8
    imp = (_dot(ovt_ref[...], p_hi) + _dot(ovt_ref[...], p_lo))[:nb]
    n_b = lax.broadcasted_iota(jnp.int32, (nb, tq), 0)
    cur = (t0 + lax.broadcasted_iota(jnp.int32, (nb, tq), 1)) // SLC_BLOCK
    forced = (n_b == 0) | (n_b == cur) | (n_b == cur - 1)
    visible = n_b <= cur

    top_n = min(SLC_TOPK, n_slc)
    n_f = n_b.astype(F32)
    work = jnp.where(forced, -jnp.inf, jnp.where(visible, imp, -jnp.inf))
    picked = jnp.zeros((nb, tq), F32)
    for _ in range(max(top_n - 3, 0)):
        mx = jnp.max(work, axis=0, keepdims=True)
        first = jnp.min(jnp.where(work == mx, n_f, float(nb)), axis=0, keepdims=True)
        hit = n_f == jnp.where(mx > -jnp.inf, first, -1.0)
        picked = jnp.where(hit, 1.0, picked)
        work = jnp.where(hit, -jnp.inf, work)
    chosen = jnp.where(forced, 1.0, jnp.where(cur < top_n, 1.0, picked))
    live = jnp.where(visible, jnp.where(n_b < t0 // SLC_BLOCK, chosen, 0.0), 0.0)
    bias_ref[sub, :nb] = jnp.where(live > 0.5, 0.0, NEG)
    if nb < LANES:
        bias_ref[sub, nb:] = jnp.full((LANES - nb, tq), NEG, F32)

    tri = lax.broadcasted_iota(jnp.int32, (tq, tq), 0) <= lax.broadcasted_iota(jnp.int32, (tq, tq), 1)
    m_d, p_d = [], []
    for sr in heads(s_d):
        sr = jnp.where(tri, sr, NEG)
        m = jnp.max(sr, axis=0, keepdims=True)
        m_d.append(m)
        p_d.append(jnp.exp2(sr - m).astype(BF16))
    acc0 = _dot(vst_ref[:, pl.ds(t0a, tq)], jnp.concatenate(p_d, axis=1))

    pw = []
    if sub * tq >= WIN_SIZE and span == WIN_SIZE + tq:
        for sr in heads(s_w):
            old = jnp.where(tri, NEG, sr[:tq])
            mid = sr[tq:WIN_SIZE]
            new = jnp.where(tri, sr[WIN_SIZE:], NEG)
            m = jnp.maximum(jnp.maximum(jnp.max(old, axis=0, keepdims=True), jnp.max(mid, axis=0, keepdims=True)),
                            jnp.max(new, axis=0, keepdims=True))
            pw.append(jnp.concatenate([jnp.exp2(old - m), jnp.exp2(mid - m), jnp.exp2(new - m)],
                                      axis=0).astype(BF16))
    else:
        diff = t_lane - (start + lax.broadcasted_iota(jnp.int32, (span, tq), 0))
        wmask = (diff >= 0) & (diff < WIN_SIZE)
        for sr in heads(s_w):
            sr = jnp.where(wmask, sr, NEG)
            m = jnp.max(sr, axis=0, keepdims=True)
            pw.append(jnp.exp2(sr - m).astype(BF16))
    acc_w = _dot(vwt_ref[:, pl.ds(start, span)], jnp.concatenate(pw, axis=1))
    o_win = acc_w[:HEAD_DIM] / acc_w[HEAD_DIM:HEAD_DIM + 1]
    return q2, o_cmp, o_win, jnp.concatenate(m_d, axis=1), acc0


def _nsa_kernel(q_ref, kc_ref, vct_ref, ks_ref, vst_ref, kw_ref, vwt_ref, gate_ref, ovt_ref,
                o_ref, bias_ref, s_ref, m_ref, acc_ref, *, nsub, tq, tk, n_cmp, n_slc, seq):
    R = NSA_HPG
    W = R * tq
    bpt = tk // SLC_BLOCK
    last_tile = seq // tk - 1
    step = pl.program_id(2)
    fronts = [_nsa_front(q_ref, kc_ref, vct_ref, ks_ref, vst_ref, kw_ref, vwt_ref, ovt_ref, bias_ref,
                         sub=sub, nsub=nsub, tq=tq, n_cmp=n_cmp, n_slc=n_slc, seq=seq) for sub in range(nsub)]
    q_all = jnp.concatenate([f[0] for f in fronts], axis=0)

    def qk(kt, slot):
        k0 = pl.multiple_of(jnp.minimum(kt, last_tile) * tk, tk)
        s_ref[slot] = _dot_nt(ks_ref[pl.ds(k0, tk), :], q_all)

    def qk_span(kt, slot, span_tile):
        k0 = pl.multiple_of(kt * tk, tk)
        for sub in range(span_tile * tk // tq + 1, nsub):
            rows = min(tk, sub * tq - span_tile * tk)
            s_ref[slot, :rows, sub * W:(sub + 1) * W] = _dot_nt(ks_ref[pl.ds(k0, rows), :],
                                                               q_all[sub * W:(sub + 1) * W])

    def update(kt, slot, first_sub=0, span_tile=None):
        k0 = pl.multiple_of(kt * tk, tk)
        v_t = vst_ref[:, pl.ds(k0, tk)]
        for sub in range(first_sub, nsub):
            nblk = bpt if span_tile is None else min(bpt, (sub * tq - span_tile * tk) // SLC_BLOCK)
            brows = [bias_ref[sub, pl.ds(kt * bpt + j, 1), :] for j in range(nblk)]
            p_all, alphas = [], []
            for r in range(R):
                c0 = sub * W + r * tq
                cols = slice(c0, c0 + tq)
                mo = m_ref[:, cols]
                m8 = None
                for j in range(nblk):
                    blk = s_ref[slot, j * SLC_BLOCK:(j + 1) * SLC_BLOCK, cols]
                    b8 = jnp.max(blk.reshape(SLC_BLOCK // 8, 8, tq), axis=0) + brows[j]
                    m8 = b8 if m8 is None else jnp.maximum(m8, b8)
                mn = jnp.maximum(mo, jnp.max(m8, axis=0, keepdims=True))
                p_all.append(jnp.concatenate(
                    [jnp.exp2(s_ref[slot, j * SLC_BLOCK:(j + 1) * SLC_BLOCK, cols] + (brows[j] - mn)).astype(BF16)
                     for j in range(nblk)], axis=0))
                alphas.append(jnp.exp2(mo - mn))
                m_ref[:, cols] = mn
            pv = _dot(v_t[:, :nblk * SLC_BLOCK], jnp.concatenate(p_all, axis=1))
            cols = slice(sub * W, (sub + 1) * W)
            acc_ref[:, cols] = jnp.concatenate(alphas, axis=1) * acc_ref[:, cols] + pv

    def pair_body(ii, _):
        a = 2 * ii
        qk(a + 1, 1)
        update(a, 0)
        qk(a + 2, 0)
        update(a + 1, 1)
        return 0

    for sub, f in enumerate(fronts):
        m_ref[:, sub * W:(sub + 1) * W] = f[3]
        acc_ref[:, sub * W:(sub + 1) * W] = f[4]
    span_tiles = nsub * tq // tk
    qk(0, 0)
    lax.fori_loop(0, step * span_tiles // 2, pair_body, 0)
    for j in range(span_tiles):
        kt = step * span_tiles + j
        slot = j % 2
        first_sub = j * tk // tq + 1
        if j + 1 < span_tiles:
            qk_span(kt + 1, 1 - slot, j + 1)
        update(kt, slot, first_sub, span_tile=j)
    o_slc = acc_ref[:HEAD_DIM, :] / acc_ref[HEAD_DIM:HEAD_DIM + 1, :]

    for sub in range(nsub):
        rows = slice(sub * tq, (sub + 1) * tq)
        _, o_cmp, o_win, _, _ = fronts[sub]
        gate_t = gate_ref[rows, :].T
        ys = []
        for r in range(R):
            cols = slice(r * tq, (r + 1) * tq)
            ys.append(gate_t[3 * r:3 * r + 1] * o_cmp[:, cols]
                      + gate_t[3 * r + 1:3 * r + 2] * o_slc[:, sub * W + r * tq:sub * W + (r + 1) * tq]
                      + gate_t[3 * r + 2:3 * r + 3] * o_win[:, cols])
        o_ref[rows, :] = jnp.concatenate(ys, axis=0).T.astype(o_ref.dtype)


def _overlap_t(ncp, n_cmp, n_slc):
    cs = np.arange(ncp)[None, :] * CMP_STRIDE
    ss = np.arange(LANES)[:, None] * SLC_BLOCK
    ov = np.clip(np.minimum(cs + CMP_BLOCK, ss + SLC_BLOCK) - np.maximum(cs, ss), 0, None) / CMP_BLOCK
    ov = ov * (np.arange(ncp)[None, :] < n_cmp) * (np.arange(LANES)[:, None] < n_slc)
    return jnp.asarray(ov, BF16)


def _nsa(qa, kc, vct, ks, vst, kw, vwt, gns, tq, tk, nsub):
    B, H, S, dh = qa.shape
    ncp = kc.shape[2]
    n_cmp = (S - CMP_BLOCK) // CMP_STRIDE + 1
    n_slc = S // SLC_BLOCK
    assert n_slc <= LANES and S % (2 * tk) == 0 and tk % SLC_BLOCK == 0 and tq == 2 * SLC_BLOCK and S % (nsub * tq) == 0 and (nsub * tq) % (2 * tk) == 0
    ovt = _overlap_t(ncp, n_cmp, n_slc)
    k_c = pl.BlockSpec((None, None, ncp, dh), lambda b, g, t: (b, g, 0, 0))
    v_c = pl.BlockSpec((None, None, dh, ncp), lambda b, g, t: (b, g, 0, 0))
    k_s = pl.BlockSpec((None, None, S, dh), lambda b, g, t: (b, g, 0, 0))
    v_s = pl.BlockSpec((None, None, vst.shape[2], S), lambda b, g, t: (b, g, 0, 0))
    v_w = pl.BlockSpec((None, None, vwt.shape[2], S), lambda b, g, t: (b, g, 0, 0))
    const = lambda b, g, t: (0, 0)
    kern = functools.partial(_nsa_kernel, tq=tq, tk=tk, n_cmp=n_cmp, n_slc=n_slc, seq=S, nsub=nsub)
    return pl.pallas_call(
        kern,
        grid=(B, NSA_GROUPS, S // (nsub * tq)),
        in_specs=[pl.BlockSpec((None, NSA_HPG, nsub * tq, dh), lambda b, g, t: (b, g, t, 0)),
                  k_c, v_c, k_s, v_s, k_s, v_w,
                  pl.BlockSpec((None, nsub * tq, LANES), lambda b, g, t: (b, t, g)),
                  pl.BlockSpec((LANES, ncp), const)],
        out_specs=pl.BlockSpec((None, nsub * tq, NSA_HPG * dh), lambda b, g, t: (b, t, g)),
        out_shape=jax.ShapeDtypeStruct((B, S, H * dh), BF16),
        scratch_shapes=[pltpu.VMEM((nsub, LANES, tq), F32),
                        pltpu.VMEM((2, tk, nsub * NSA_HPG * tq), F32),
                        pltpu.VMEM((1, nsub * NSA_HPG * tq), F32),
                        pltpu.VMEM((vst.shape[2], nsub * NSA_HPG * tq), F32)],
        compiler_params=pltpu.CompilerParams(dimension_semantics=("parallel", "parallel", "arbitrary"),
                                             vmem_limit_bytes=VMEM_LIMIT),
        name="nsa",
    )(qa, kc, vct, ks, vst, kw, vwt, gns, ovt)


def _dilated_kernel(q_ref, kp_ref, kc_ref, vp_ref, vc_ref, o_ref, lse_ref, vt_ref, *, d, tq, nq):
    span = 2 * tq
    nh = LANES // HEAD_DIM
    row = lax.broadcasted_iota(jnp.int32, (span, tq), 0)
    lane = lax.broadcasted_iota(jnp.int32, (span, tq), 1)
    diff = (lane + tq) - row
    band = (diff >= 0) & (diff <= DIL_SPAN)
    has_prev = pl.program_id(1) > 0
    first = band & ((row >= tq) | has_prev)
    head_of_lane = lax.broadcasted_iota(jnp.int32, (tq, LANES), 1) // HEAD_DIM
    vt_ref[:, LANES:, :] = jnp.ones((nq * d, ONES_ROWS, span), vt_ref.dtype)

    def scores(j, r):
        rows = pl.ds(r + j * tq * d, tq, stride=d)
        qb = q_ref[rows, :]
        if j == 0:
            kb = jnp.concatenate([kp_ref[pl.ds(r, tq, stride=d), :], kc_ref[rows, :]], axis=0)
            vb = jnp.concatenate([vp_ref[pl.ds(r, tq, stride=d), :], vc_ref[rows, :]], axis=0)
        else:
            kv_rows = pl.ds(r + (j - 1) * tq * d, span, stride=d)
            kb, vb = kc_ref[kv_rows, :], vc_ref[kv_rows, :]
        vt_ref[j * d + r, 0:LANES, :] = vb.T.astype(BF16)
        q_bd = jnp.concatenate([jnp.where(head_of_lane == h, qb, 0.0) for h in range(nh)], axis=0).astype(BF16)
        return _dot_nt(kb.astype(BF16), q_bd)

    def finish(j, r, s):
        mask = first if j == 0 else band
        ms, ps = [], []
        for h in range(nh):
            sh = jnp.where(mask, s[:, h * tq:(h + 1) * tq], NEG)
            m = jnp.max(sh, axis=0, keepdims=True)
            ms.append(m)
            ps.append(jnp.exp2(sh - m).astype(BF16))
        pv = _dot(vt_ref[j * d + r], jnp.concatenate(ps, axis=1))
        o_t, l_t = [], []
        for h in range(nh):
            den = pv[LANES:LANES + 1, h * tq:(h + 1) * tq]
            o_t.append(pv[h * HEAD_DIM:(h + 1) * HEAD_DIM, h * tq:(h + 1) * tq] / den)
            l_t.append(jnp.broadcast_to(ms[h] + jnp.log2(den), (HEAD_DIM, tq)))
        rows = pl.ds(r + j * tq * d, tq, stride=d)
        o_ref[rows, :] = jnp.concatenate(o_t, axis=0).T
        lse_ref[rows, :] = jnp.concatenate(l_t, axis=0).T

    pending = None
    for j in range(nq):
        for r in range(d):
            s = scores(j, r)
            if pending is not None:
                finish(*pending)
            pending = (j, r, s)
    finish(*pending)


def _dilated(qb, kb, vb, g, block_tokens):
    B, S, _ = qb.shape
    d = DIL_PAIRS[g][1]
    tq = min(LANES, S // d)
    nq = max(block_tokens // (d * tq), 1)
    assert DIL_PAIRS[g][0] // d == DIL_SPAN and DIL_SPAN <= tq and S % (nq * d * tq) == 0
    gw = DIL_HPG * HEAD_DIM
    nb = gw // LANES
    cur = pl.BlockSpec((None, nq * d * tq, LANES), lambda b, i, c: (b, i, g * nb + c))
    prev = pl.BlockSpec((None, d * tq, LANES), lambda b, i, c: (b, jnp.maximum(i * nq - 1, 0), g * nb + c))
    out_blk = pl.BlockSpec((None, nq * d * tq, LANES), lambda b, i, c: (b, i, c))
    return pl.pallas_call(
        functools.partial(_dilated_kernel, d=d, tq=tq, nq=nq),
        grid=(B, S // (nq * d * tq), nb),
        in_specs=[cur, prev, cur, prev, cur],
        out_specs=[out_blk, out_blk],
        out_shape=[jax.ShapeDtypeStruct((B, S, gw), F32)] * 2,
        scratch_shapes=[pltpu.VMEM((nq * d, LANES + ONES_ROWS, 2 * tq), BF16)],
        compiler_params=pltpu.CompilerParams(dimension_semantics=("parallel", "arbitrary", "arbitrary"),
                                             vmem_limit_bytes=VMEM_LIMIT),
        name=f"dilated{g}",
    )(qb, kb, kb, vb, vb)


def _out_mlp_kernel(x_ref, ya_ref, o0_ref, o1_ref, o2_ref, l0_ref, l1_ref, l2_ref, gma_ref, gmb_ref,
                    woa_ref, wob_ref, wout_ref, g2_ref, wup_ref, wdn_ref, out_ref, *, fc):
    l0, l1, l2 = l0_ref[...], l1_ref[...], l2_ref[...]
    mx = jnp.maximum(jnp.maximum(l0, l1), l2)
    e0, e1, e2 = jnp.exp2(l0 - mx), jnp.exp2(l1 - mx), jnp.exp2(l2 - mx)
    yb = (e0 * o0_ref[...] + e1 * o1_ref[...] + e2 * o2_ref[...]) / (e0 + e1 + e2)
    ta = _dot(ya_ref[...], woa_ref[...])
    tb = _dot(yb.astype(BF16), wob_ref[...])
    mixed = gma_ref[...].astype(F32) * ta + gmb_ref[...].astype(F32) * tb
    x1 = x_ref[...] + _dot(mixed.astype(BF16), wout_ref[...])
    ms = jnp.mean(x1 * x1, axis=-1, keepdims=True)
    h = (x1 * lax.rsqrt(ms + NORM_EPS) * g2_ref[...]).astype(BF16)
    acc = x1
    for off, w in _chunks(wup_ref.shape[1], fc):
        u = jnp.maximum(_dot(h, wup_ref[:, off:off + w]), 0.0)
        acc = acc + _dot((u * u).astype(BF16), wdn_ref[off:off + w, :])
    out_ref[...] = acc


def _out_mlp(x, ya, dil, gma, gmb, w_o_a, w_o_b, w_out, norm2_g, w_up, w_down, tm, fc):
    B, S, D = x.shape
    row = lambda w: pl.BlockSpec((None, tm, w), lambda b, m: (b, m, 0))
    full = lambda a: pl.BlockSpec(a.shape, lambda b, m: (0, 0))
    gw = DIL_HPG * HEAD_DIM
    ws = [w.astype(BF16) for w in (w_o_a, w_o_b, w_out)]
    g2 = norm2_g.reshape(1, D)
    wu, wd = w_up.astype(BF16), w_down.astype(BF16)
    return pl.pallas_call(
        functools.partial(_out_mlp_kernel, fc=fc),
        grid=(B, S // tm),
        in_specs=([row(D), row(ya.shape[-1])] + [row(gw)] * 6 + [row(D), row(D)]
                  + [full(w) for w in ws] + [full(g2), full(wu), full(wd)]),
        out_specs=row(D),
        out_shape=jax.ShapeDtypeStruct((B, S, D), F32),
        compiler_params=pltpu.CompilerParams(dimension_semantics=("parallel", "parallel"),
                                             vmem_limit_bytes=VMEM_LIMIT),
        name="out_mlp",
    )(x, ya, dil[0][0], dil[1][0], dil[2][0], dil[0][1], dil[1][1], dil[2][1], gma, gmb, *ws, g2, wu, wd)


def kernel(x, norm1_g, w_in, q_norm_a, k_norm_cmp, k_norm_slc, k_norm_win, cmp_k_pos, cmp_k_w1, cmp_k_w2,
           cmp_v_pos, cmp_v_w1, cmp_v_w2, q_norm_b, k_norm_b, w_o_a, w_o_b, w_out, norm2_g, w_up, w_down):
    depth = w_in.shape[0]
    for i in range(depth):
        (qa, ks, kw, qb, kb, kvc_raw, vb, gma, gmb, gns, vst, vwt) = _inproj(
            x, norm1_g[i], w_in[i], q_norm_a[i], k_norm_slc[i], k_norm_win[i], q_norm_b[i], k_norm_b[i], tm=INPROJ_ROWS)
        kc, vct = _compress(kvc_raw, k_norm_cmp[i], cmp_k_pos[i], cmp_k_w1[i], cmp_k_w2[i],
                           cmp_v_pos[i], cmp_v_w1[i], cmp_v_w2[i])
        ya = _nsa(qa, kc, vct, ks, vst, kw, vwt, gns, tq=NSA_TQ, tk=NSA_TK, nsub=NSA_TILES_PER_STEP)
        dil = [_dilated(qb, kb, vb, g, block_tokens=min(DIL_BLOCK_TOKENS, x.shape[1]))
               for g in range(DIL_GROUPS)]
        x = _out_mlp(x, ya, dil, gma, gmb, w_o_a[i], w_o_b[i], w_out[i], norm2_g[i], w_up[i], w_down[i],
                     tm=OUT_MLP_ROWS, fc=MLP_FF_CHUNK)
    return x
```

```python
import functools

import numpy as np
import jax
import jax.numpy as jnp
from jax import lax
from jax.experimental import pallas as pl
from jax.experimental.pallas import tpu as pltpu

HEAD_DIM = 64
HALF = HEAD_DIM // 2
ROPE_THETA = 10000.0
NORM_EPS = 1e-6
NEG = -1e30
NSA_HEADS = 12
NSA_GROUPS = 3
NSA_HPG = NSA_HEADS // NSA_GROUPS
CMP_BLOCK = 32
CMP_STRIDE = 16
SLC_BLOCK = 64
SLC_TOPK = 16
WIN_SIZE = 512
DIL_PAIRS = ((128, 1), (512, 4), (2048, 16))
DIL_GROUPS = len(DIL_PAIRS)
DIL_HPG = 4
DIL_HEADS = DIL_GROUPS * DIL_HPG
DIL_SPAN = 128
LOG2E = float(np.log2(np.e))
ONES_ROWS = 16

LANES = 128
MXU_N = 256
VMEM_LIMIT = 56 * 1024 * 1024

BF16 = jnp.bfloat16
F32 = jnp.float32

INPROJ_ROWS = 512
OUT_MLP_ROWS = 512
MLP_FF_CHUNK = 1024
NSA_TQ = 2 * SLC_BLOCK
NSA_TK = 512
NSA_TILES_PER_STEP = 8
DIL_BLOCK_TOKENS = 4096

D_MODEL = 1024
Q_A, KV_A, GATE_A, QKV_B = NSA_HEADS * HEAD_DIM, NSA_GROUPS * HEAD_DIM, 3 * NSA_HEADS, DIL_HEADS * HEAD_DIM
IN_NAMES = ("qa", "kc", "vc", "ks", "vs", "kw", "vw", "gns", "qb", "kb", "vb", "gma", "gmb")
IN_SIZES = (Q_A, KV_A, KV_A, KV_A, KV_A, KV_A, KV_A, GATE_A, QKV_B, QKV_B, QKV_B, D_MODEL, D_MODEL)
ROPE_SEGS = (("qa", Q_A), ("qb", QKV_B), ("kb", QKV_B), ("ks", KV_A), ("kw", KV_A))
PLAIN_SEGS = (("kvc", 2 * KV_A), ("vb", QKV_B))
SIG_SEGS = (("gma", D_MODEL), ("gmb", D_MODEL), ("gns", NSA_GROUPS * LANES))
ROPE_W = sum(w for _, w in ROPE_SEGS)
PLAIN_W = sum(w for _, w in PLAIN_SEGS)
SIG_W = sum(w for _, w in SIG_SEGS)
HEAD_MAJOR = ("qa", "ks", "kw")


def _dot(a, b):
    return jnp.dot(a, b, preferred_element_type=F32)


def _dot_nt(a, b):
    return lax.dot_general(a, b, (((1,), (1,)), ((), ())), preferred_element_type=F32)


def _aligned(x, m):
    return x if isinstance(x, int) else pl.multiple_of(x, m)


def _chunks(total, width):
    out, off = [], 0
    while off < total:
        w = min(width, total - off)
        out.append((off, w))
        off += w
    return out


def _seg_lookup(segs, col):
    off = 0
    for name, w in segs:
        if col < off + w:
            return name, col - off
        off += w
    raise ValueError(col)


def _inproj_kernel(x_ref, g1_ref, w_ref, wvt_ref, gain_ref, bd_ref, cos_ref, sin_ref,
                   qa_ref, ks_ref, kw_ref, qb_ref, kb_ref,
                   kvc_ref, vb_ref, gma_ref, gmb_ref, gns_ref, vst_ref, vwt_ref):
    outs = dict(qa=qa_ref, ks=ks_ref, kw=kw_ref, qb=qb_ref, kb=kb_ref, kvc=kvc_ref,
                vb=vb_ref, gma=gma_ref, gmb=gmb_ref, gns=gns_ref)

    def emit(segs, col, val):
        name, rel = _seg_lookup(segs, col)
        ref = outs[name]
        if name in HEAD_MAJOR:
            for p in range(LANES // HEAD_DIM):
                nm, r = _seg_lookup(segs, col + p * HEAD_DIM)
                outs[nm][r // HEAD_DIM] = val[:, p * HEAD_DIM:(p + 1) * HEAD_DIM].astype(outs[nm].dtype)
        else:
            nm2, _ = _seg_lookup(segs, col + HEAD_DIM)
            if nm2 != name:
                raise ValueError("unaligned dense segment")
            ref[:, rel:rel + LANES] = val.astype(ref.dtype)

    x = x_ref[...]
    ms = jnp.mean(x * x, axis=-1, keepdims=True)
    h = (x * lax.rsqrt(ms + NORM_EPS) * g1_ref[...]).astype(BF16)

    lane = lax.broadcasted_iota(jnp.int32, (x.shape[0], LANES), 1)
    first_half = (lane % HEAD_DIM) < HALF
    cos = cos_ref[...]
    sin = sin_ref[...]
    bd = bd_ref[...]

    def rope_epilogue(off, w, y):
        msq = _dot((y * y).astype(BF16), bd[:w, :w])
        yn = y * lax.rsqrt(msq + NORM_EPS) * gain_ref[:, off:off + w]
        for u in range(w // LANES):
            v = yn[:, u * LANES:(u + 1) * LANES]
            rot = jnp.where(first_half, pltpu.roll(v, LANES - HALF, 1), pltpu.roll(v, HALF, 1))
            emit(ROPE_SEGS, off + u * LANES, v * cos + rot * sin)

    def plain_epilogue(off, w, y):
        for u in range(w // LANES):
            emit(PLAIN_SEGS, off + u * LANES, y[:, u * LANES:(u + 1) * LANES])

    def sig_epilogue(off, w, y):
        y = jax.nn.sigmoid(y)
        for u in range(w // LANES):
            emit(SIG_SEGS, off + u * LANES, y[:, u * LANES:(u + 1) * LANES])

    work = ([(0, off, w, rope_epilogue) for off, w in _chunks(ROPE_W, MXU_N)]
            + [(ROPE_W, off, w, plain_epilogue) for off, w in _chunks(PLAIN_W, MXU_N)]
            + [(ROPE_W + PLAIN_W, off, w, sig_epilogue) for off, w in _chunks(SIG_W, MXU_N)])
    pending = None
    for base, off, w, epilogue in work:
        y = _dot(h, w_ref[:, base + off:base + off + w])
        if pending is not None:
            pending[0](pending[1], pending[2], pending[3])
        pending = (epilogue, off, w, y)
    pending[0](pending[1], pending[2], pending[3])

    v_t = _dot_nt(wvt_ref[...], h)
    ones = jnp.ones((ONES_ROWS, x.shape[0]), vst_ref.dtype)
    for i, ref in enumerate((vst_ref, vwt_ref)):
        for g in range(NSA_GROUPS):
            r0 = (i * NSA_GROUPS + g) * HEAD_DIM
            ref[g, 0:HEAD_DIM, :] = v_t[r0:r0 + HEAD_DIM].astype(ref.dtype)
            ref[g, HEAD_DIM:HEAD_DIM + ONES_ROWS, :] = ones


def _rope_tables(positions, width):
    inv_freq = np.power(ROPE_THETA, -np.arange(HALF, dtype=np.float64) / HALF)
    ang = np.asarray(positions, np.float64)[:, None] * inv_freq[None, :]
    reps = width // HEAD_DIM
    cos = np.tile(np.concatenate([np.cos(ang), np.cos(ang)], axis=1), (1, reps))
    sin = np.tile(np.concatenate([-np.sin(ang), np.sin(ang)], axis=1), (1, reps))
    return jnp.asarray(cos, F32), jnp.asarray(sin, F32)


def _inproj(x, norm1_g, w_in, q_norm_a, k_norm_slc, k_norm_win, q_norm_b, k_norm_b, tm):
    B, S, D = x.shape
    scale = HEAD_DIM ** -0.5
    assert D == D_MODEL and w_in.shape == (D, sum(IN_SIZES))
    sp = np.cumsum((0,) + IN_SIZES)
    w_bf = w_in.astype(BF16)
    col = {n: w_bf[:, int(sp[i]):int(sp[i + 1])] for i, n in enumerate(IN_NAMES)}
    gpg = GATE_A // NSA_GROUPS
    gns = jnp.concatenate(
        [jnp.pad(col["gns"][:, g * gpg:(g + 1) * gpg], ((0, 0), (0, LANES - gpg))) for g in range(NSA_GROUPS)],
        axis=1)
    col["gns"] = gns
    col["kvc"] = jnp.concatenate([col["kc"], col["vc"]], axis=1)
    w_p = jnp.concatenate([col[n] for n, _ in ROPE_SEGS + PLAIN_SEGS + SIG_SEGS], axis=1)
    wvt = jnp.concatenate([col["vs"], col["vw"]], axis=1).T
    ncol = ROPE_W + PLAIN_W + SIG_W
    assert w_p.shape == (D, ncol)

    gain = jnp.concatenate([
        jnp.tile(q_norm_a * (scale * LOG2E), NSA_HEADS), jnp.tile(q_norm_b * (scale * LOG2E), DIL_HEADS),
        jnp.tile(k_norm_b, DIL_HEADS), jnp.tile(k_norm_slc, NSA_GROUPS), jnp.tile(k_norm_win, NSA_GROUPS)
    ]).reshape(1, ROPE_W).astype(F32)
    bd = jnp.asarray(np.kron(np.eye(MXU_N // HEAD_DIM), np.full((HEAD_DIM, HEAD_DIM), 1.0 / HEAD_DIM)), BF16)
    cos, sin = _rope_tables(np.arange(S), LANES)

    def hm(nh, dt):
        return (jax.ShapeDtypeStruct((B, nh, S, HEAD_DIM), dt),
                pl.BlockSpec((None, nh, tm, HEAD_DIM), lambda b, m: (b, 0, m, 0)))

    def vt():
        rows = HEAD_DIM + ONES_ROWS
        return (jax.ShapeDtypeStruct((B, NSA_GROUPS, rows, S), BF16),
                pl.BlockSpec((None, NSA_GROUPS, rows, tm), lambda b, m: (b, 0, 0, m)))

    def dense(w, dt):
        return (jax.ShapeDtypeStruct((B, S, w), dt), pl.BlockSpec((None, tm, w), lambda b, m: (b, m, 0)))

    outs = [hm(NSA_HEADS, BF16), hm(NSA_GROUPS, BF16), hm(NSA_GROUPS, BF16),
            dense(QKV_B, F32), dense(QKV_B, F32),
            dense(2 * KV_A, F32),
            dense(QKV_B, F32),
            dense(D, BF16), dense(D, BF16), dense(NSA_GROUPS * LANES, F32),
            vt(), vt()]
    const = lambda b, m: (0, 0)
    return pl.pallas_call(
        _inproj_kernel,
        grid=(B, S // tm),
        in_specs=[pl.BlockSpec((None, tm, D), lambda b, m: (b, m, 0)),
                  pl.BlockSpec((1, D), const),
                  pl.BlockSpec((D, ncol), const),
                  pl.BlockSpec(wvt.shape, const),
                  pl.BlockSpec((1, ROPE_W), const),
                  pl.BlockSpec((MXU_N, MXU_N), const),
                  pl.BlockSpec((tm, LANES), lambda b, m: (m, 0)),
                  pl.BlockSpec((tm, LANES), lambda b, m: (m, 0))],
        out_specs=[o[1] for o in outs],
        out_shape=[o[0] for o in outs],
        compiler_params=pltpu.CompilerParams(dimension_semantics=("parallel", "arbitrary"),
                                             vmem_limit_bytes=VMEM_LIMIT),
        name="inproj",
    )(x, norm1_g.reshape(1, D), w_p, wvt, gain, bd, cos, sin)


def _gelu_tanh(x):
    return 0.5 * x * (1.0 + jnp.tanh(np.sqrt(2.0 / np.pi) * (x + 0.044715 * (x * x * x))))


def _compress_kernel(x0_ref, x1_ref, x2_ref, pk_ref, pv_ref, w1k_ref, w2k_ref, w1v_ref, w2vt_ref,
                     gain_ref, cos_ref, sin_ref, kc_ref, vct_ref):
    nc = x0_ref.shape[0] // CMP_STRIDE
    half = CMP_STRIDE * HEAD_DIM
    per_blk = LANES // HEAD_DIM
    xs = [[x_ref[pl.ds(l, nc, stride=CMP_STRIDE), :] for l in range(CMP_STRIDE)] for x_ref in (x0_ref, x1_ref, x2_ref)]

    def hidden(head, pos_ref, w1_ref):
        blk, sub = divmod(head, per_blk)
        cols = slice(sub * HEAD_DIM, (sub + 1) * HEAD_DIM)
        x = [xs[blk][l][:, cols] for l in range(CMP_STRIDE)]
        xa = jnp.concatenate([x[l] + pos_ref[l:l + 1, :] for l in range(CMP_STRIDE)], axis=1)
        xb = jnp.concatenate([x[l] + pos_ref[CMP_STRIDE + l:CMP_STRIDE + l + 1, :] for l in range(CMP_STRIDE)], axis=1)
        a = _dot(xa.astype(BF16), w1_ref[0:half, :])
        b = _dot(xb.astype(BF16), w1_ref[half:2 * half, :])
        pre = a + pltpu.roll(b, nc - 1, 0)
        return _gelu_tanh(pre).astype(BF16)

    for g in range(NSA_GROUPS):
        kc = _dot(hidden(g, pk_ref, w1k_ref), w2k_ref[...])
        vct = _dot_nt(w2vt_ref[...], hidden(NSA_GROUPS + g, pv_ref, w1v_ref))
        ms = jnp.mean(kc * kc, axis=-1, keepdims=True)
        kn = kc * lax.rsqrt(ms + NORM_EPS) * gain_ref[...]
        rot = jnp.concatenate([kn[:, HALF:], kn[:, :HALF]], axis=-1)
        kc_ref[g] = (kn * cos_ref[...] + rot * sin_ref[...]).astype(kc_ref.dtype)
        vct_ref[g] = vct.astype(vct_ref.dtype)


def _compress(kvc_raw, k_norm_cmp, kpos, kw1, kw2, vpos, vw1, vw2):
    B, S, w = kvc_raw.shape
    G, dh = NSA_GROUPS, HEAD_DIM
    assert w == 2 * G * dh == 3 * LANES
    nc = S // CMP_STRIDE
    cos, sin = _rope_tables(np.arange(nc) * CMP_STRIDE + CMP_BLOCK - 1, dh)
    const = lambda b: (0, 0)
    xblk = lambda j: pl.BlockSpec((None, S, LANES), lambda b: (b, 0, j))
    hid = kw1.shape[1]
    return pl.pallas_call(
        _compress_kernel,
        grid=(B,),
        in_specs=[xblk(0), xblk(1), xblk(2),
                  pl.BlockSpec((CMP_BLOCK, dh), const), pl.BlockSpec((CMP_BLOCK, dh), const),
                  pl.BlockSpec((CMP_BLOCK * dh, hid), const), pl.BlockSpec((hid, dh), const),
                  pl.BlockSpec((CMP_BLOCK * dh, hid), const), pl.BlockSpec((dh, hid), const),
                  pl.BlockSpec((1, dh), const), pl.BlockSpec((nc, dh), const), pl.BlockSpec((nc, dh), const)],
        out_specs=[pl.BlockSpec((None, G, nc, dh), lambda b: (b, 0, 0, 0)),
                   pl.BlockSpec((None, G, dh, nc), lambda b: (b, 0, 0, 0))],
        out_shape=[jax.ShapeDtypeStruct((B, G, nc, dh), BF16), jax.ShapeDtypeStruct((B, G, dh, nc), BF16)],
        compiler_params=pltpu.CompilerParams(dimension_semantics=("parallel",), vmem_limit_bytes=VMEM_LIMIT),
        name="compress",
    )(kvc_raw, kvc_raw, kvc_raw, kpos, vpos, kw1.astype(BF16), kw2.astype(BF16), vw1.astype(BF16),
      vw2.T.astype(BF16), k_norm_cmp.reshape(1, dh).astype(F32), cos, sin)


def _nsa_front(q_ref, kc_ref, vct_ref, ks_ref, vst_ref, kw_ref, vwt_ref, ovt_ref, bias_ref,
               *, step, sub, nsub, tq, n_cmp, n_slc, seq):
    R = NSA_HPG
    t0 = (step * nsub + sub) * tq
    t0a = t0
    q2 = q_ref[:, sub * tq:(sub + 1) * tq, :].reshape(R * tq, HEAD_DIM)
    ncp = min(kc_ref.shape[0], -(-((t0 + tq - CMP_BLOCK) // CMP_STRIDE + 1) // LANES) * LANES)
    t_lane = t0 + lax.broadcasted_iota(jnp.int32, (1, tq), 1)
    span = min(WIN_SIZE + tq, seq)
    start = max(t0 - WIN_SIZE, 0)

    def heads(a):
        return [a[:, r * tq:(r + 1) * tq] for r in range(R)]

    s_c = _dot_nt(kc_ref[:ncp, :], q2)
    s_d = _dot_nt(ks_ref[pl.ds(t0a, tq), :], q2)
    s_w = _dot_nt(kw_ref[pl.ds(start, span), :], q2)

    c_idx = lax.broadcasted_iota(jnp.int32, (ncp, tq), 0)
    cmask = ((c_idx * CMP_STRIDE + (CMP_BLOCK - 1)) <= t_lane) & (c_idx < n_cmp)
    ps = []
    for sr in heads(s_c):
        sr = jnp.where(cmask, sr, NEG)
        m = jnp.max(sr, axis=0, keepdims=True)
        e = jnp.exp2(sr - m)
        den = jnp.sum(e, axis=0, keepdims=True)
        ps.append(e * jnp.where(m > 0.5 * NEG, 1.0 / den, 0.0))
    o_cmp = _dot(vct_ref[:, :ncp], jnp.concatenate(ps, axis=1).astype(BF16))

    psum = ps[0]
    for r in range(1, R):
        psum = psum + ps[r]
    p_hi = psum.astype(BF16)
    p_lo = (psum - p_hi.astype(F32)).astype(BF16)
    nb = -(-n_slc // 8) * 8
    imp = (_dot(ovt_ref[:, :ncp], p_hi) + _dot(ovt_ref[:, :ncp], p_lo))[:nb]
    n_b = lax.broadcasted_iota(jnp.int32, (nb, tq), 0)
    cur = (t0 + lax.broadcasted_iota(jnp.int32, (nb, tq), 1)) // SLC_BLOCK
    forced = (n_b == 0) | (n_b == cur) | (n_b == cur - 1)
    visible = n_b <= cur

    top_n = min(SLC_TOPK, n_slc)
    n_f = n_b.astype(F32)
    work = jnp.where(forced, -jnp.inf, jnp.where(visible, imp, -jnp.inf))
    picked = jnp.zeros((nb, tq), F32)
    all_chosen = (t0 + tq - 1) // SLC_BLOCK < top_n
    for _ in range(0 if all_chosen else max(top_n - 3, 0)):
        mx = jnp.max(work, axis=0, keepdims=True)
        first = jnp.min(jnp.where(work == mx, n_f, float(nb)), axis=0, keepdims=True)
        hit = n_f == jnp.where(mx > -jnp.inf, first, -1.0)
        picked = jnp.where(hit, 1.0, picked)
        work = jnp.where(hit, -jnp.inf, work)
    chosen = jnp.where(forced, 1.0, jnp.where(cur < top_n, 1.0, picked))
    live = jnp.where(visible, jnp.where(n_b < t0 // SLC_BLOCK, chosen, 0.0), 0.0)
    bias_ref[sub, :nb] = jnp.where(live > 0.5, 0.0, NEG)
    if nb < LANES:
        bias_ref[sub, nb:] = jnp.full((LANES - nb, tq), NEG, F32)

    tri = lax.broadcasted_iota(jnp.int32, (tq, tq), 0) <= lax.broadcasted_iota(jnp.int32, (tq, tq), 1)
    m_d, p_d = [], []
    for sr in heads(s_d):
        sr = jnp.where(tri, sr, NEG)
        m = jnp.max(sr, axis=0, keepdims=True)
        m_d.append(m)
        p_d.append(jnp.exp2(sr - m).astype(BF16))
    acc0 = _dot(vst_ref[:, pl.ds(t0a, tq)], jnp.concatenate(p_d, axis=1))

    pw = []
    if t0 >= WIN_SIZE and span == WIN_SIZE + tq:
        for sr in heads(s_w):
            old = jnp.where(tri, NEG, sr[:tq])
            mid = sr[tq:WIN_SIZE]
            new = jnp.where(tri, sr[WIN_SIZE:], NEG)
            m = jnp.maximum(jnp.maximum(jnp.max(old, axis=0, keepdims=True), jnp.max(mid, axis=0, keepdims=True)),
                            jnp.max(new, axis=0, keepdims=True))
            pw.append(jnp.concatenate([jnp.exp2(old - m), jnp.exp2(mid - m), jnp.exp2(new - m)],
                                      axis=0).astype(BF16))
    else:
        diff = t_lane - (start + lax.broadcasted_iota(jnp.int32, (span, tq), 0))
        wmask = (diff >= 0) & (diff < WIN_SIZE)
        for sr in heads(s_w):
            sr = jnp.where(wmask, sr, NEG)
            m = jnp.max(sr, axis=0, keepdims=True)
            pw.append(jnp.exp2(sr - m).astype(BF16))
    acc_w = _dot(vwt_ref[:, pl.ds(start, span)], jnp.concatenate(pw, axis=1))
    o_win = acc_w[:HEAD_DIM] / acc_w[HEAD_DIM:HEAD_DIM + 1]
    return q2, o_cmp, o_win, jnp.concatenate(m_d, axis=1), acc0


def _nsa_kernel(q_ref, kc_ref, vct_ref, ks_ref, vst_ref, kw_ref, vwt_ref, gate_ref, ovt_ref,
                o_ref, bias_ref, s_ref, m_ref, acc_ref, *, step, nsub, tq, tk, n_cmp, n_slc, seq):
    R = NSA_HPG
    W = R * tq
    bpt = tk // SLC_BLOCK
    fronts = [_nsa_front(q_ref, kc_ref, vct_ref, ks_ref, vst_ref, kw_ref, vwt_ref, ovt_ref, bias_ref,
                         step=step, sub=sub, nsub=nsub, tq=tq, n_cmp=n_cmp, n_slc=n_slc, seq=seq) for sub in range(nsub)]
    q_all = jnp.concatenate([f[0] for f in fronts], axis=0)

    def qk(kt, slot):
        k0 = _aligned(kt * tk, tk)
        s_ref[slot] = _dot_nt(ks_ref[pl.ds(k0, tk), :], q_all)

    def qk_span(kt, slot, span_tile):
        k0 = kt * tk
        for sub in range(span_tile * tk // tq + 1, nsub):
            rows = min(tk, sub * tq - span_tile * tk)
            s_ref[slot, :rows, sub * W:(sub + 1) * W] = _dot_nt(ks_ref[pl.ds(k0, rows), :],
                                                               q_all[sub * W:(sub + 1) * W])

    def update(kt, slot, first_sub=0, span_tile=None):
        k0 = _aligned(kt * tk, tk)
        v_t = vst_ref[:, pl.ds(k0, tk)]
        for sub in range(first_sub, nsub):
            nblk = bpt if span_tile is None else min(bpt, (sub * tq - span_tile * tk) // SLC_BLOCK)
            brows = [bias_ref[sub, pl.ds(kt * bpt + j, 1), :] for j in range(nblk)]
            p_all, alphas = [], []
            for r in range(R):
                c0 = sub * W + r * tq
                cols = slice(c0, c0 + tq)
                mo = m_ref[:, cols]
                m8 = None
                for j in range(nblk):
                    blk = s_ref[slot, j * SLC_BLOCK:(j + 1) * SLC_BLOCK, cols]
                    b8 = jnp.max(blk.reshape(SLC_BLOCK // 8, 8, tq), axis=0) + brows[j]
                    m8 = b8 if m8 is None else jnp.maximum(m8, b8)
                mn = jnp.maximum(mo, jnp.max(m8, axis=0, keepdims=True))
                p_all.append(jnp.concatenate(
                    [jnp.exp2(s_ref[slot, j * SLC_BLOCK:(j + 1) * SLC_BLOCK, cols] + (brows[j] - mn)).astype(BF16)
                     for j in range(nblk)], axis=0))
                alphas.append(jnp.exp2(mo - mn))
                m_ref[:, cols] = mn
            pv = _dot(v_t[:, :nblk * SLC_BLOCK], jnp.concatenate(p_all, axis=1))
            cols = slice(sub * W, (sub + 1) * W)
            acc_ref[:, cols] = jnp.concatenate(alphas, axis=1) * acc_ref[:, cols] + pv

    def pair_body(ii, _):
        a = 2 * ii
        qk(a + 1, 1)
        update(a, 0)
        qk(a + 2, 0)
        update(a + 1, 1)
        return 0

    for sub, f in enumerate(fronts):
        m_ref[:, sub * W:(sub + 1) * W] = f[3]
        acc_ref[:, sub * W:(sub + 1) * W] = f[4]
    span_tiles = nsub * tq // tk
    qk(0, 0)
    for ii in range(step * span_tiles // 2):
        pair_body(ii, 0)
    for j in range(span_tiles):
        kt = step * span_tiles + j
        slot = j % 2
        first_sub = j * tk // tq + 1
        if j + 1 < span_tiles:
            qk_span(kt + 1, 1 - slot, j + 1)
        update(kt, slot, first_sub, span_tile=j)
    o_slc = acc_ref[:HEAD_DIM, :] / acc_ref[HEAD_DIM:HEAD_DIM + 1, :]

    for sub in range(nsub):
        rows = slice(sub * tq, (sub + 1) * tq)
        _, o_cmp, o_win, _, _ = fronts[sub]
        gate_t = gate_ref[rows, :].T
        ys = []
        for r in range(R):
            cols = slice(r * tq, (r + 1) * tq)
            ys.append(gate_t[3 * r:3 * r + 1] * o_cmp[:, cols]
                      + gate_t[3 * r + 1:3 * r + 2] * o_slc[:, sub * W + r * tq:sub * W + (r + 1) * tq]
                      + gate_t[3 * r + 2:3 * r + 3] * o_win[:, cols])
        o_ref[rows, :] = jnp.concatenate(ys, axis=0).T.astype(o_ref.dtype)


def _overlap_t(ncp, n_cmp, n_slc):
    cs = np.arange(ncp)[None, :] * CMP_STRIDE
    ss = np.arange(LANES)[:, None] * SLC_BLOCK
    ov = np.clip(np.minimum(cs + CMP_BLOCK, ss + SLC_BLOCK) - np.maximum(cs, ss), 0, None) / CMP_BLOCK
    ov = ov * (np.arange(ncp)[None, :] < n_cmp) * (np.arange(LANES)[:, None] < n_slc)
    return jnp.asarray(ov, BF16)


def _nsa(qa, kc, vct, ks, vst, kw, vwt, gns, tq, tk, nsub):
    B, H, S, dh = qa.shape
    ncp = kc.shape[2]
    n_cmp = (S - CMP_BLOCK) // CMP_STRIDE + 1
    n_slc = S // SLC_BLOCK
    assert (n_slc <= LANES and S % (2 * tk) == 0 and tk % SLC_BLOCK == 0 and tq == 2 * SLC_BLOCK
            and S % (nsub * tq) == 0 and (nsub * tq) % (2 * tk) == 0)
    ovt = _overlap_t(ncp, n_cmp, n_slc)
    k_c = pl.BlockSpec((None, None, ncp, dh), lambda b, g: (b, g, 0, 0))
    v_c = pl.BlockSpec((None, None, dh, ncp), lambda b, g: (b, g, 0, 0))
    k_s = pl.BlockSpec((None, None, S, dh), lambda b, g: (b, g, 0, 0))
    v_s = pl.BlockSpec((None, None, vst.shape[2], S), lambda b, g: (b, g, 0, 0))
    v_w = pl.BlockSpec((None, None, vwt.shape[2], S), lambda b, g: (b, g, 0, 0))
    const = lambda b, g: (0, 0)
    rows = nsub * tq
    parts = []
    for step in range(S // rows):
        kern = functools.partial(_nsa_kernel, step=step, tq=tq, tk=tk, n_cmp=n_cmp, n_slc=n_slc, seq=S, nsub=nsub)
        parts.append(pl.pallas_call(
            kern,
            grid=(B, NSA_GROUPS),
            in_specs=[pl.BlockSpec((None, NSA_HPG, rows, dh), lambda b, g, step=step: (b, g, step, 0)),
                      k_c, v_c, k_s, v_s, k_s, v_w,
                      pl.BlockSpec((None, rows, LANES), lambda b, g, step=step: (b, step, g)),
                      pl.BlockSpec((LANES, ncp), const)],
            out_specs=pl.BlockSpec((None, rows, NSA_HPG * dh), lambda b, g: (b, 0, g)),
            out_shape=jax.ShapeDtypeStruct((B, rows, H * dh), BF16),
            scratch_shapes=[pltpu.VMEM((nsub, LANES, tq), F32),
                            pltpu.VMEM((2, tk, nsub * NSA_HPG * tq), F32),
                            pltpu.VMEM((1, nsub * NSA_HPG * tq), F32),
                            pltpu.VMEM((vst.shape[2], nsub * NSA_HPG * tq), F32)],
            compiler_params=pltpu.CompilerParams(dimension_semantics=("parallel", "arbitrary"),
                                                 vmem_limit_bytes=VMEM_LIMIT),
            name=f"nsa{step}",
        )(qa, kc, vct, ks, vst, kw, vwt, gns, ovt))
    return jnp.concatenate(parts, axis=1)


def _dilated_kernel(q_ref, kp_ref, kc_ref, vp_ref, vc_ref, o_ref, lse_ref, vt_ref, *, d, tq, nq):
    span = 2 * tq
    nh = LANES // HEAD_DIM
    row = lax.broadcasted_iota(jnp.int32, (span, tq), 0)
    lane = lax.broadcasted_iota(jnp.int32, (span, tq), 1)
    diff = (lane + tq) - row
    band = (diff >= 0) & (diff <= DIL_SPAN)
    has_prev = pl.program_id(1) > 0
    first = band & ((row >= tq) | has_prev)
    head_of_lane = lax.broadcasted_iota(jnp.int32, (tq, LANES), 1) // HEAD_DIM
    vt_ref[:, LANES:, :] = jnp.ones((nq * d, ONES_ROWS, span), vt_ref.dtype)

    def scores(j, r):
        rows = pl.ds(r + j * tq * d, tq, stride=d)
        qb = q_ref[rows, :]
        if j == 0:
            kb = jnp.concatenate([kp_ref[pl.ds(r, tq, stride=d), :], kc_ref[rows, :]], axis=0)
            vb = jnp.concatenate([vp_ref[pl.ds(r, tq, stride=d), :], vc_ref[rows, :]], axis=0)
        else:
            kv_rows = pl.ds(r + (j - 1) * tq * d, span, stride=d)
            kb, vb = kc_ref[kv_rows, :], vc_ref[kv_rows, :]
        vt_ref[j * d + r, 0:LANES, :] = vb.T.astype(BF16)
        q_bd = jnp.concatenate([jnp.where(head_of_lane == h, qb, 0.0) for h in range(nh)], axis=0).astype(BF16)
        return _dot_nt(kb.astype(BF16), q_bd)

    def finish(j, r, s):
        mask = first if j == 0 else band
        ms, ps = [], []
        for h in range(nh):
            sh = jnp.where(mask, s[:, h * tq:(h + 1) * tq], NEG)
            m = jnp.max(sh, axis=0, keepdims=True)
            ms.append(m)
            ps.append(jnp.exp2(sh - m).astype(BF16))
        pv = _dot(vt_ref[j * d + r], jnp.concatenate(ps, axis=1))
        o_t, l_t = [], []
        for h in range(nh):
            den = pv[LANES:LANES + 1, h * tq:(h + 1) * tq]
            o_t.append(pv[h * HEAD_DIM:(h + 1) * HEAD_DIM, h * tq:(h + 1) * tq] / den)
            l_t.append(jnp.broadcast_to(ms[h] + jnp.log2(den), (HEAD_DIM, tq)))
        rows = pl.ds(r + j * tq * d, tq, stride=d)
        o_ref[rows, :] = jnp.concatenate(o_t, axis=0).T
        lse_ref[rows, :] = jnp.concatenate(l_t, axis=0).T

    pending = None
    for j in range(nq):
        for r in range(d):
            s = scores(j, r)
            if pending is not None:
                finish(*pending)
            pending = (j, r, s)
    finish(*pending)


def _dilated(qb, kb, vb, g, block_tokens):
    B, S, _ = qb.shape
    d = DIL_PAIRS[g][1]
    tq = min(LANES, S // d)
    nq = max(block_tokens // (d * tq), 1)
    assert DIL_PAIRS[g][0] // d == DIL_SPAN and DIL_SPAN <= tq and S % (nq * d * tq) == 0
    gw = DIL_HPG * HEAD_DIM
    nb = gw // LANES
    cur = pl.BlockSpec((None, nq * d * tq, LANES), lambda b, i, c: (b, i, g * nb + c))
    prev = pl.BlockSpec((None, d * tq, LANES), lambda b, i, c: (b, jnp.maximum(i * nq - 1, 0), g * nb + c))
    out_blk = pl.BlockSpec((None, nq * d * tq, LANES), lambda b, i, c: (b, i, c))
    return pl.pallas_call(
        functools.partial(_dilated_kernel, d=d, tq=tq, nq=nq),
        grid=(B, S // (nq * d * tq), nb),
        in_specs=[cur, prev, cur, prev, cur],
        out_specs=[out_blk, out_blk],
        out_shape=[jax.ShapeDtypeStruct((B, S, gw), F32)] * 2,
        scratch_shapes=[pltpu.VMEM((nq * d, LANES + ONES_ROWS, 2 * tq), BF16)],
        compiler_params=pltpu.CompilerParams(dimension_semantics=("parallel", "arbitrary", "arbitrary"),
                                             vmem_limit_bytes=VMEM_LIMIT),
        name=f"dilated{g}",
    )(qb, kb, kb, vb, vb)


def _out_mlp_kernel(x_ref, ya_ref, o0_ref, o1_ref, o2_ref, l0_ref, l1_ref, l2_ref, gma_ref, gmb_ref,
                    woa_ref, wob_ref, wout_ref, g2_ref, wup_ref, wdn_ref, out_ref, *, fc):
    l0, l1, l2 = l0_ref[...], l1_ref[...], l2_ref[...]
    mx = jnp.maximum(jnp.maximum(l0, l1), l2)
    e0, e1, e2 = jnp.exp2(l0 - mx), jnp.exp2(l1 - mx), jnp.exp2(l2 - mx)
    yb = (e0 * o0_ref[...] + e1 * o1_ref[...] + e2 * o2_ref[...]) / (e0 + e1 + e2)
    ta = _dot(ya_ref[...], woa_ref[...])
    tb = _dot(yb.astype(BF16), wob_ref[...])
    mixed = gma_ref[...].astype(F32) * ta + gmb_ref[...].astype(F32) * tb
    x1 = x_ref[...] + _dot(mixed.astype(BF16), wout_ref[...])
    ms = jnp.mean(x1 * x1, axis=-1, keepdims=True)
    h = (x1 * lax.rsqrt(ms + NORM_EPS) * g2_ref[...]).astype(BF16)
    acc = x1
    for off, w in _chunks(wup_ref.shape[1], fc):
        u = jnp.maximum(_dot(h, wup_ref[:, off:off + w]), 0.0)
        acc = acc + _dot((u * u).astype(BF16), wdn_ref[off:off + w, :])
    out_ref[...] = acc


def _out_mlp(x, ya, dil, gma, gmb, w_o_a, w_o_b, w_out, norm2_g, w_up, w_down, tm, fc):
    B, S, D = x.shape
    row = lambda w: pl.BlockSpec((None, tm, w), lambda b, m: (b, m, 0))
    full = lambda a: pl.BlockSpec(a.shape, lambda b, m: (0, 0))
    gw = DIL_HPG * HEAD_DIM
    ws = [w.astype(BF16) for w in (w_o_a, w_o_b, w_out)]
    g2 = norm2_g.reshape(1, D)
    wu, wd = w_up.astype(BF16), w_down.astype(BF16)
    return pl.pallas_call(
        functools.partial(_out_mlp_kernel, fc=fc),
        grid=(B, S // tm),
        in_specs=([row(D), row(ya.shape[-1])] + [row(gw)] * 6 + [row(D), row(D)]
                  + [full(w) for w in ws] + [full(g2), full(wu), full(wd)]),
        out_specs=row(D),
        out_shape=jax.ShapeDtypeStruct((B, S, D), F32),
        compiler_params=pltpu.CompilerParams(dimension_semantics=("parallel", "parallel"),
                                             vmem_limit_bytes=VMEM_LIMIT),
        name="out_mlp",
    )(x, ya, dil[0][0], dil[1][0], dil[2][0], dil[0][1], dil[1][1], dil[2][1], gma, gmb, *ws, g2, wu, wd)


def kernel(x, norm1_g, w_in, q_norm_a, k_norm_cmp, k_norm_slc, k_norm_win, cmp_k_pos, cmp_k_w1, cmp_k_w2,
           cmp_v_pos, cmp_v_w1, cmp_v_w2, q_norm_b, k_norm_b, w_o_a, w_o_b, w_out, norm2_g, w_up, w_down):
    depth = w_in.shape[0]
    for i in range(depth):
        (qa, ks, kw, qb, kb, kvc_raw, vb, gma, gmb, gns, vst, vwt) = _inproj(
            x, norm1_g[i], w_in[i], q_norm_a[i], k_norm_slc[i], k_norm_win[i], q_norm_b[i], k_norm_b[i], tm=INPROJ_ROWS)
        kc, vct = _compress(kvc_raw, k_norm_cmp[i], cmp_k_pos[i], cmp_k_w1[i], cmp_k_w2[i],
                           cmp_v_pos[i], cmp_v_w1[i], cmp_v_w2[i])
        ya = _nsa(qa, kc, vct, ks, vst, kw, vwt, gns, tq=NSA_TQ, tk=NSA_TK, nsub=NSA_TILES_PER_STEP)
        dil = [_dilated(qb, kb, vb, g, block_tokens=min(DIL_BLOCK_TOKENS, x.shape[1]))
               for g in range(DIL_GROUPS)]
        x = _out_mlp(x, ya, dil, gma, gmb, w_o_a[i], w_o_b[i], w_out[i], norm2_g[i], w_up[i], w_down[i],
                     tm=OUT_MLP_ROWS, fc=MLP_FF_CHUNK)
    return x
```

```python
import functools

import numpy as np
import jax
import jax.numpy as jnp
from jax import lax
from jax.experimental import pallas as pl
from jax.experimental.pallas import tpu as pltpu

HEAD_DIM = 64
HALF = HEAD_DIM // 2
ROPE_THETA = 10000.0
NORM_EPS = 1e-6
NEG = -1e30
NSA_HEADS = 12
NSA_GROUPS = 3
NSA_HPG = NSA_HEADS // NSA_GROUPS
CMP_BLOCK = 32
CMP_STRIDE = 16
SLC_BLOCK = 64
SLC_TOPK = 16
WIN_SIZE = 512
DIL_PAIRS = ((128, 1), (512, 4), (2048, 16))
DIL_GROUPS = len(DIL_PAIRS)
DIL_HPG = 4
DIL_HEADS = DIL_GROUPS * DIL_HPG
DIL_SPAN = 128
LOG2E = float(np.log2(np.e))
ONES_ROWS = 16

LANES = 128
MXU_N = 256
VMEM_LIMIT = 56 * 1024 * 1024

BF16 = jnp.bfloat16
F32 = jnp.float32

INPROJ_ROWS = 512
OUT_MLP_ROWS = 512
MLP_FF_CHUNK = 1024
NSA_TQ = 2 * SLC_BLOCK
NSA_TK = 512
NSA_TILES_PER_STEP = 8
DIL_BLOCK_TOKENS = 4096

D_MODEL = 1024
Q_A, KV_A, GATE_A, QKV_B = NSA_HEADS * HEAD_DIM, NSA_GROUPS * HEAD_DIM, 3 * NSA_HEADS, DIL_HEADS * HEAD_DIM
IN_NAMES = ("qa", "kc", "vc", "ks", "vs", "kw", "vw", "gns", "qb", "kb", "vb", "gma", "gmb")
IN_SIZES = (Q_A, KV_A, KV_A, KV_A, KV_A, KV_A, KV_A, GATE_A, QKV_B, QKV_B, QKV_B, D_MODEL, D_MODEL)
ROPE_SEGS = (("qa", Q_A), ("qb", QKV_B), ("kb", QKV_B), ("ks", KV_A), ("kw", KV_A))
PLAIN_SEGS = (("kvc", 2 * KV_A), ("vb", QKV_B))
SIG_SEGS = (("gma", D_MODEL), ("gmb", D_MODEL), ("gns", NSA_GROUPS * LANES))
ROPE_W = sum(w for _, w in ROPE_SEGS)
PLAIN_W = sum(w for _, w in PLAIN_SEGS)
SIG_W = sum(w for _, w in SIG_SEGS)
HEAD_MAJOR = ("qa", "ks", "kw")


def _dot(a, b):
    return jnp.dot(a, b, preferred_element_type=F32)


def _dot_nt(a, b):
    return lax.dot_general(a, b, (((1,), (1,)), ((), ())), preferred_element_type=F32)


def _aligned(x, m):
    return x if isinstance(x, int) else pl.multiple_of(x, m)


def _chunks(total, width):
    out, off = [], 0
    while off < total:
        w = min(width, total - off)
        out.append((off, w))
        off += w
    return out


def _seg_lookup(segs, col):
    off = 0
    for name, w in segs:
        if col < off + w:
            return name, col - off
        off += w
    raise ValueError(col)


def _inproj_kernel(x_ref, g1_ref, w_ref, wvt_ref, gain_ref, bd_ref, cos_ref, sin_ref,
                   qa_ref, ks_ref, kw_ref, qb_ref, kb_ref,
                   kvc_ref, vb_ref, gma_ref, gmb_ref, gns_ref, vst_ref, vwt_ref):
    outs = dict(qa=qa_ref, ks=ks_ref, kw=kw_ref, qb=qb_ref, kb=kb_ref, kvc=kvc_ref,
                vb=vb_ref, gma=gma_ref, gmb=gmb_ref, gns=gns_ref)

    def emit(segs, col, val):
        name, rel = _seg_lookup(segs, col)
        ref = outs[name]
        if name in HEAD_MAJOR:
            for p in range(LANES // HEAD_DIM):
                nm, r = _seg_lookup(segs, col + p * HEAD_DIM)
                outs[nm][r // HEAD_DIM] = val[:, p * HEAD_DIM:(p + 1) * HEAD_DIM].astype(outs[nm].dtype)
        else:
            nm2, _ = _seg_lookup(segs, col + HEAD_DIM)
            if nm2 != name:
                raise ValueError("unaligned dense segment")
            ref[:, rel:rel + LANES] = val.astype(ref.dtype)

    x = x_ref[...]
    ms = jnp.mean(x * x, axis=-1, keepdims=True)
    h = (x * lax.rsqrt(ms + NORM_EPS) * g1_ref[...]).astype(BF16)

    lane = lax.broadcasted_iota(jnp.int32, (x.shape[0], LANES), 1)
    first_half = (lane % HEAD_DIM) < HALF
    cos = cos_ref[...]
    sin = sin_ref[...]
    bd = bd_ref[...]

    def rope_epilogue(off, w, y):
        msq = _dot((y * y).astype(BF16), bd[:w, :w])
        yn = y * lax.rsqrt(msq + NORM_EPS) * gain_ref[:, off:off + w]
        for u in range(w // LANES):
            v = yn[:, u * LANES:(u + 1) * LANES]
            rot = jnp.where(first_half, pltpu.roll(v, LANES - HALF, 1), pltpu.roll(v, HALF, 1))
            emit(ROPE_SEGS, off + u * LANES, v * cos + rot * sin)

    def plain_epilogue(off, w, y):
        for u in range(w // LANES):
            emit(PLAIN_SEGS, off + u * LANES, y[:, u * LANES:(u + 1) * LANES])

    def sig_epilogue(off, w, y):
        y = jax.nn.sigmoid(y)
        for u in range(w // LANES):
            emit(SIG_SEGS, off + u * LANES, y[:, u * LANES:(u + 1) * LANES])

    work = ([(0, off, w, rope_epilogue) for off, w in _chunks(ROPE_W, MXU_N)]
            + [(ROPE_W, off, w, plain_epilogue) for off, w in _chunks(PLAIN_W, MXU_N)]
            + [(ROPE_W + PLAIN_W, off, w, sig_epilogue) for off, w in _chunks(SIG_W, MXU_N)])
    pending = None
    for base, off, w, epilogue in work:
        y = _dot(h, w_ref[:, base + off:base + off + w])
        if pending is not None:
            pending[0](pending[1], pending[2], pending[3])
        pending = (epilogue, off, w, y)
    pending[0](pending[1], pending[2], pending[3])

    v_t = _dot_nt(wvt_ref[...], h)
    ones = jnp.ones((ONES_ROWS, x.shape[0]), vst_ref.dtype)
    for i, ref in enumerate((vst_ref, vwt_ref)):
        for g in range(NSA_GROUPS):
            r0 = (i * NSA_GROUPS + g) * HEAD_DIM
            ref[g, 0:HEAD_DIM, :] = v_t[r0:r0 + HEAD_DIM].astype(ref.dtype)
            ref[g, HEAD_DIM:HEAD_DIM + ONES_ROWS, :] = ones


def _rope_tables(positions, width):
    inv_freq = np.power(ROPE_THETA, -np.arange(HALF, dtype=np.float64) / HALF)
    ang = np.asarray(positions, np.float64)[:, None] * inv_freq[None, :]
    reps = width // HEAD_DIM
    cos = np.tile(np.concatenate([np.cos(ang), np.cos(ang)], axis=1), (1, reps))
    sin = np.tile(np.concatenate([-np.sin(ang), np.sin(ang)], axis=1), (1, reps))
    return jnp.asarray(cos, F32), jnp.asarray(sin, F32)


def _inproj(x, norm1_g, w_in, q_norm_a, k_norm_slc, k_norm_win, q_norm_b, k_norm_b, tm):
    B, S, D = x.shape
    scale = HEAD_DIM ** -0.5
    assert D == D_MODEL and w_in.shape == (D, sum(IN_SIZES))
    sp = np.cumsum((0,) + IN_SIZES)
    w_bf = w_in.astype(BF16)
    col = {n: w_bf[:, int(sp[i]):int(sp[i + 1])] for i, n in enumerate(IN_NAMES)}
    gpg = GATE_A // NSA_GROUPS
    gns = jnp.concatenate(
        [jnp.pad(col["gns"][:, g * gpg:(g + 1) * gpg], ((0, 0), (0, LANES - gpg))) for g in range(NSA_GROUPS)],
        axis=1)
    col["gns"] = gns
    col["kvc"] = jnp.concatenate([col["kc"], col["vc"]], axis=1)
    w_p = jnp.concatenate([col[n] for n, _ in ROPE_SEGS + PLAIN_SEGS + SIG_SEGS], axis=1)
    wvt = jnp.concatenate([col["vs"], col["vw"]], axis=1).T
    ncol = ROPE_W + PLAIN_W + SIG_W
    assert w_p.shape == (D, ncol)

    gain = jnp.concatenate([
        jnp.tile(q_norm_a * (scale * LOG2E), NSA_HEADS), jnp.tile(q_norm_b * (scale * LOG2E), DIL_HEADS),
        jnp.tile(k_norm_b, DIL_HEADS), jnp.tile(k_norm_slc, NSA_GROUPS), jnp.tile(k_norm_win, NSA_GROUPS)
    ]).reshape(1, ROPE_W).astype(F32)
    bd = jnp.asarray(np.kron(np.eye(MXU_N // HEAD_DIM), np.full((HEAD_DIM, HEAD_DIM), 1.0 / HEAD_DIM)), BF16)
    cos, sin = _rope_tables(np.arange(S), LANES)

    def hm(nh, dt):
        return (jax.ShapeDtypeStruct((B, nh, S, HEAD_DIM), dt),
                pl.BlockSpec((None, nh, tm, HEAD_DIM), lambda b, m: (b, 0, m, 0)))

    def vt():
        rows = HEAD_DIM + ONES_ROWS
        return (jax.ShapeDtypeStruct((B, NSA_GROUPS, rows, S), BF16),
                pl.BlockSpec((None, NSA_GROUPS, rows, tm), lambda b, m: (b, 0, 0, m)))

    def dense(w, dt):
        return (jax.ShapeDtypeStruct((B, S, w), dt), pl.BlockSpec((None, tm, w), lambda b, m: (b, m, 0)))

    outs = [hm(NSA_HEADS, BF16), hm(NSA_GROUPS, BF16), hm(NSA_GROUPS, BF16),
            dense(QKV_B, F32), dense(QKV_B, F32),
            dense(2 * KV_A, F32),
            dense(QKV_B, F32),
            dense(D, BF16), dense(D, BF16), dense(NSA_GROUPS * LANES, F32),
            vt(), vt()]
    const = lambda b, m: (0, 0)
    return pl.pallas_call(
        _inproj_kernel,
        grid=(B, S // tm),
        in_specs=[pl.BlockSpec((None, tm, D), lambda b, m: (b, m, 0)),
                  pl.BlockSpec((1, D), const),
                  pl.BlockSpec((D, ncol), const),
                  pl.BlockSpec(wvt.shape, const),
                  pl.BlockSpec((1, ROPE_W), const),
                  pl.BlockSpec((MXU_N, MXU_N), const),
                  pl.BlockSpec((tm, LANES), lambda b, m: (m, 0)),
                  pl.BlockSpec((tm, LANES), lambda b, m: (m, 0))],
        out_specs=[o[1] for o in outs],
        out_shape=[o[0] for o in outs],
        compiler_params=pltpu.CompilerParams(dimension_semantics=("parallel", "arbitrary"),
                                             vmem_limit_bytes=VMEM_LIMIT),
        name="inproj",
    )(x, norm1_g.reshape(1, D), w_p, wvt, gain, bd, cos, sin)


def _gelu_tanh(x):
    return 0.5 * x * (1.0 + jnp.tanh(np.sqrt(2.0 / np.pi) * (x + 0.044715 * (x * x * x))))


def _compress_kernel(x0_ref, x1_ref, x2_ref, pk_ref, pv_ref, w1k_ref, w2k_ref, w1v_ref, w2vt_ref,
                     gain_ref, cos_ref, sin_ref, kc_ref, vct_ref):
    nc = x0_ref.shape[0] // CMP_STRIDE
    half = CMP_STRIDE * HEAD_DIM
    per_blk = LANES // HEAD_DIM
    xs = [[x_ref[pl.ds(l, nc, stride=CMP_STRIDE), :] for l in range(CMP_STRIDE)] for x_ref in (x0_ref, x1_ref, x2_ref)]

    def hidden(head, pos_ref, w1_ref):
        blk, sub = divmod(head, per_blk)
        cols = slice(sub * HEAD_DIM, (sub + 1) * HEAD_DIM)
        x = [xs[blk][l][:, cols] for l in range(CMP_STRIDE)]
        xa = jnp.concatenate([x[l] + pos_ref[l:l + 1, :] for l in range(CMP_STRIDE)], axis=1)
        xb = jnp.concatenate([x[l] + pos_ref[CMP_STRIDE + l:CMP_STRIDE + l + 1, :] for l in range(CMP_STRIDE)], axis=1)
        a = _dot(xa.astype(BF16), w1_ref[0:half, :])
        b = _dot(xb.astype(BF16), w1_ref[half:2 * half, :])
        pre = a + pltpu.roll(b, nc - 1, 0)
        return _gelu_tanh(pre).astype(BF16)

    for g in range(NSA_GROUPS):
        kc = _dot(hidden(g, pk_ref, w1k_ref), w2k_ref[...])
        vct = _dot_nt(w2vt_ref[...], hidden(NSA_GROUPS + g, pv_ref, w1v_ref))
        ms = jnp.mean(kc * kc, axis=-1, keepdims=True)
        kn = kc * lax.rsqrt(ms + NORM_EPS) * gain_ref[...]
        rot = jnp.concatenate([kn[:, HALF:], kn[:, :HALF]], axis=-1)
        kc_ref[g] = (kn * cos_ref[...] + rot * sin_ref[...]).astype(kc_ref.dtype)
        vct_ref[g] = vct.astype(vct_ref.dtype)


def _compress(kvc_raw, k_norm_cmp, kpos, kw1, kw2, vpos, vw1, vw2):
    B, S, w = kvc_raw.shape
    G, dh = NSA_GROUPS, HEAD_DIM
    assert w == 2 * G * dh == 3 * LANES
    nc = S // CMP_STRIDE
    cos, sin = _rope_tables(np.arange(nc) * CMP_STRIDE + CMP_BLOCK - 1, dh)
    const = lambda b: (0, 0)
    xblk = lambda j: pl.BlockSpec((None, S, LANES), lambda b: (b, 0, j))
    hid = kw1.shape[1]
    return pl.pallas_call(
        _compress_kernel,
        grid=(B,),
        in_specs=[xblk(0), xblk(1), xblk(2),
                  pl.BlockSpec((CMP_BLOCK, dh), const), pl.BlockSpec((CMP_BLOCK, dh), const),
                  pl.BlockSpec((CMP_BLOCK * dh, hid), const), pl.BlockSpec((hid, dh), const),
                  pl.BlockSpec((CMP_BLOCK * dh, hid), const), pl.BlockSpec((dh, hid), const),
                  pl.BlockSpec((1, dh), const), pl.BlockSpec((nc, dh), const), pl.BlockSpec((nc, dh), const)],
        out_specs=[pl.BlockSpec((None, G, nc, dh), lambda b: (b, 0, 0, 0)),
                   pl.BlockSpec((None, G, dh, nc), lambda b: (b, 0, 0, 0))],
        out_shape=[jax.ShapeDtypeStruct((B, G, nc, dh), BF16), jax.ShapeDtypeStruct((B, G, dh, nc), BF16)],
        compiler_params=pltpu.CompilerParams(dimension_semantics=("parallel",), vmem_limit_bytes=VMEM_LIMIT),
        name="compress",
    )(kvc_raw, kvc_raw, kvc_raw, kpos, vpos, kw1.astype(BF16), kw2.astype(BF16), vw1.astype(BF16),
      vw2.T.astype(BF16), k_norm_cmp.reshape(1, dh).astype(F32), cos, sin)


def _nsa_front(q_ref, kc_ref, vct_ref, ks_ref, vst_ref, kw_ref, vwt_ref, ovt_ref, bias_ref,
               *, step, sub, nsub, tq, n_cmp, n_slc, seq):
    R = NSA_HPG
    t0 = (step * nsub + sub) * tq
    t0a = t0
    q2 = q_ref[:, sub * tq:(sub + 1) * tq, :].reshape(R * tq, HEAD_DIM)
    ncp = min(kc_ref.shape[0], -(-((t0 + tq - CMP_BLOCK) // CMP_STRIDE + 1) // LANES) * LANES)
    t_lane = t0 + lax.broadcasted_iota(jnp.int32, (1, tq), 1)
    span = min(WIN_SIZE + tq, seq)
    start = max(t0 - WIN_SIZE, 0)

    def heads(a):
        return [a[:, r * tq:(r + 1) * tq] for r in range(R)]

    s_c = _dot_nt(kc_ref[:ncp, :], q2)
    s_d = _dot_nt(ks_ref[pl.ds(t0a, tq), :], q2)
    s_w = _dot_nt(kw_ref[pl.ds(start, span), :], q2)

    c_idx = lax.broadcasted_iota(jnp.int32, (ncp, tq), 0)
    cmask = ((c_idx * CMP_STRIDE + (CMP_BLOCK - 1)) <= t_lane) & (c_idx < n_cmp)
    ps = []
    for sr in heads(s_c):
        sr = jnp.where(cmask, sr, NEG)
        m = jnp.max(sr, axis=0, keepdims=True)
        e = jnp.exp2(sr - m)
        den = jnp.sum(e, axis=0, keepdims=True)
        ps.append(e * jnp.where(m > 0.5 * NEG, 1.0 / den, 0.0))
    o_cmp = _dot(vct_ref[:, :ncp], jnp.concatenate(ps, axis=1).astype(BF16))

    psum = ps[0]
    for r in range(1, R):
        psum = psum + ps[r]
    p_hi = psum.astype(BF16)
    p_lo = (psum - p_hi.astype(F32)).astype(BF16)
    nb = -(-n_slc // 8) * 8
    imp = (_dot(ovt_ref[:, :ncp], p_hi) + _dot(ovt_ref[:, :ncp], p_lo))[:nb]
    n_b = lax.broadcasted_iota(jnp.int32, (nb, tq), 0)
    cur = (t0 + lax.broadcasted_iota(jnp.int32, (nb, tq), 1)) // SLC_BLOCK
    forced = (n_b == 0) | (n_b == cur) | (n_b == cur - 1)
    visible = n_b <= cur

    top_n = min(SLC_TOPK, n_slc)
    n_f = n_b.astype(F32)
    work = jnp.where(forced, -jnp.inf, jnp.where(visible, imp, -jnp.inf))
    picked = jnp.zeros((nb, tq), F32)
    all_chosen = (t0 + tq - 1) // SLC_BLOCK < top_n
    for _ in range(0 if all_chosen else max(top_n - 3, 0)):
        mx = jnp.max(work, axis=0, keepdims=True)
        first = jnp.min(jnp.where(work == mx, n_f, float(nb)), axis=0, keepdims=True)
        hit = n_f == jnp.where(mx > -jnp.inf, first, -1.0)
        picked = jnp.where(hit, 1.0, picked)
        work = jnp.where(hit, -jnp.inf, work)
    chosen = jnp.where(forced, 1.0, jnp.where(cur < top_n, 1.0, picked))
    live = jnp.where(visible, jnp.where(n_b < t0 // SLC_BLOCK, chosen, 0.0), 0.0)
    bias_ref[sub, :nb] = jnp.where(live > 0.5, 0.0, NEG)
    if nb < LANES:
        bias_ref[sub, nb:] = jnp.full((LANES - nb, tq), NEG, F32)

    tri = lax.broadcasted_iota(jnp.int32, (tq, tq), 0) <= lax.broadcasted_iota(jnp.int32, (tq, tq), 1)
    m_d, p_d = [], []
    for sr in heads(s_d):
        sr = jnp.where(tri, sr, NEG)
        m = jnp.max(sr, axis=0, keepdims=True)
        m_d.append(m)
        p_d.append(jnp.exp2(sr - m).astype(BF16))
    acc0 = _dot(vst_ref[:, pl.ds(t0a, tq)], jnp.concatenate(p_d, axis=1))

    pw = []
    if t0 >= WIN_SIZE and span == WIN_SIZE + tq:
        for sr in heads(s_w):
            old = jnp.where(tri, NEG, sr[:tq])
            mid = sr[tq:WIN_SIZE]
            new = jnp.where(tri, sr[WIN_SIZE:], NEG)
            m = jnp.maximum(jnp.maximum(jnp.max(old, axis=0, keepdims=True), jnp.max(mid, axis=0, keepdims=True)),
                            jnp.max(new, axis=0, keepdims=True))
            pw.append(jnp.concatenate([jnp.exp2(old - m), jnp.exp2(mid - m), jnp.exp2(new - m)],
                                      axis=0).astype(BF16))
    else:
        diff = t_lane - (start + lax.broadcasted_iota(jnp.int32, (span, tq), 0))
        wmask = (diff >= 0) & (diff < WIN_SIZE)
        for sr in heads(s_w):
            sr = jnp.where(wmask, sr, NEG)
            m = jnp.max(sr, axis=0, keepdims=True)
            pw.append(jnp.exp2(sr - m).astype(BF16))
    acc_w = _dot(vwt_ref[:, pl.ds(start, span)], jnp.concatenate(pw, axis=1))
    o_win = acc_w[:HEAD_DIM] / acc_w[HEAD_DIM:HEAD_DIM + 1]
    return q2, o_cmp, o_win, jnp.concatenate(m_d, axis=1), acc0


def _nsa_kernel(q_ref, kc_ref, vct_ref, ks_ref, vst_ref, kw_ref, vwt_ref, gate_ref, ovt_ref,
                o_ref, bias_ref, s_ref, m_ref, acc_ref, *, step, nsub, tq, tk, n_cmp, n_slc, seq):
    R = NSA_HPG
    W = R * tq
    bpt = tk // SLC_BLOCK
    fronts = [_nsa_front(q_ref, kc_ref, vct_ref, ks_ref, vst_ref, kw_ref, vwt_ref, ovt_ref, bias_ref,
                         step=step, sub=sub, nsub=nsub, tq=tq, n_cmp=n_cmp, n_slc=n_slc, seq=seq) for sub in range(nsub)]
    q_all = jnp.concatenate([f[0] for f in fronts], axis=0)

    def qk(kt, slot):
        k0 = _aligned(kt * tk, tk)
        s_ref[slot] = _dot_nt(ks_ref[pl.ds(k0, tk), :], q_all)

    def qk_span(kt, slot, span_tile):
        k0 = kt * tk
        for sub in range(span_tile * tk // tq + 1, nsub):
            rows = min(tk, sub * tq - span_tile * tk)
            s_ref[slot, :rows, sub * W:(sub + 1) * W] = _dot_nt(ks_ref[pl.ds(k0, rows), :],
                                                               q_all[sub * W:(sub + 1) * W])

    def update(kt, slot, first_sub=0, span_tile=None):
        k0 = _aligned(kt * tk, tk)
        v_t = vst_ref[:, pl.ds(k0, tk)]
        for sub in range(first_sub, nsub):
            nblk = bpt if span_tile is None else min(bpt, (sub * tq - span_tile * tk) // SLC_BLOCK)
            brows = [bias_ref[sub, pl.ds(kt * bpt + j, 1), :] for j in range(nblk)]
            p_all, alphas = [], []
            for r in range(R):
                c0 = sub * W + r * tq
                cols = slice(c0, c0 + tq)
                mo = m_ref[:, cols]
                m8 = None
                for j in range(nblk):
                    blk = s_ref[slot, j * SLC_BLOCK:(j + 1) * SLC_BLOCK, cols]
                    b8 = jnp.max(blk.reshape(SLC_BLOCK // 8, 8, tq), axis=0) + brows[j]
                    m8 = b8 if m8 is None else jnp.maximum(m8, b8)
                mn = jnp.maximum(mo, jnp.max(m8, axis=0, keepdims=True))
                p_all.append(jnp.concatenate(
                    [jnp.exp2(s_ref[slot, j * SLC_BLOCK:(j + 1) * SLC_BLOCK, cols] + (brows[j] - mn)).astype(BF16)
                     for j in range(nblk)], axis=0))
                alphas.append(jnp.exp2(mo - mn))
                m_ref[:, cols] = mn
            pv = _dot(v_t[:, :nblk * SLC_BLOCK], jnp.concatenate(p_all, axis=1))
            cols = slice(sub * W, (sub + 1) * W)
            acc_ref[:, cols] = jnp.concatenate(alphas, axis=1) * acc_ref[:, cols] + pv

    def pair_body(ii, _):
        a = 2 * ii
        qk(a + 1, 1)
        update(a, 0)
        qk(a + 2, 0)
        update(a + 1, 1)
        return 0

    for sub, f in enumerate(fronts):
        m_ref[:, sub * W:(sub + 1) * W] = f[3]
        acc_ref[:, sub * W:(sub + 1) * W] = f[4]
    span_tiles = nsub * tq // tk
    qk(0, 0)
    for ii in range(step * span_tiles // 2):
        pair_body(ii, 0)
    for j in range(span_tiles):
        kt = step * span_tiles + j
        slot = j % 2
        first_sub = j * tk // tq + 1
        if j + 1 < span_tiles:
            qk_span(kt + 1, 1 - slot, j + 1)
        update(kt, slot, first_sub, span_tile=j)
    o_slc = acc_ref[:HEAD_DIM, :] / acc_ref[HEAD_DIM:HEAD_DIM + 1, :]

    for sub in range(nsub):
        rows = slice(sub * tq, (sub + 1) * tq)
        _, o_cmp, o_win, _, _ = fronts[sub]
        gate_t = gate_ref[rows, :].T
        ys = []
        for r in range(R):
            cols = slice(r * tq, (r + 1) * tq)
            ys.append(gate_t[3 * r:3 * r + 1] * o_cmp[:, cols]
                      + gate_t[3 * r + 1:3 * r + 2] * o_slc[:, sub * W + r * tq:sub * W + (r + 1) * tq]
                      + gate_t[3 * r + 2:3 * r + 3] * o_win[:, cols])
        o_ref[rows, :] = jnp.concatenate(ys, axis=0).T.astype(o_ref.dtype)


def _overlap_t(ncp, n_cmp, n_slc):
    cs = np.arange(ncp)[None, :] * CMP_STRIDE
    ss = np.arange(LANES)[:, None] * SLC_BLOCK
    ov = np.clip(np.minimum(cs + CMP_BLOCK, ss + SLC_BLOCK) - np.maximum(cs, ss), 0, None) / CMP_BLOCK
    ov = ov * (np.arange(ncp)[None, :] < n_cmp) * (np.arange(LANES)[:, None] < n_slc)
    return jnp.asarray(ov, BF16)


def _nsa(qa, kc, vct, ks, vst, kw, vwt, gns, tq, tk, nsub):
    B, H, S, dh = qa.shape
    ncp = kc.shape[2]
    n_cmp = (S - CMP_BLOCK) // CMP_STRIDE + 1
    n_slc = S // SLC_BLOCK
    assert (n_slc <= LANES and S % (2 * tk) == 0 and tk % SLC_BLOCK == 0 and tq == 2 * SLC_BLOCK
            and S % (nsub * tq) == 0 and (nsub * tq) % (2 * tk) == 0)
    ovt = _overlap_t(ncp, n_cmp, n_slc)
    k_c = pl.BlockSpec((None, None, ncp, dh), lambda b, g: (b, g, 0, 0))
    v_c = pl.BlockSpec((None, None, dh, ncp), lambda b, g: (b, g, 0, 0))
    k_s = pl.BlockSpec((None, None, S, dh), lambda b, g: (b, g, 0, 0))
    v_s = pl.BlockSpec((None, None, vst.shape[2], S), lambda b, g: (b, g, 0, 0))
    v_w = pl.BlockSpec((None, None, vwt.shape[2], S), lambda b, g: (b, g, 0, 0))
    const = lambda b, g: (0, 0)
    rows = nsub * tq
    parts = []
    for step in range(S // rows):
        kern = functools.partial(_nsa_kernel, step=step, tq=tq, tk=tk, n_cmp=n_cmp, n_slc=n_slc, seq=S, nsub=nsub)
        parts.append(pl.pallas_call(
            kern,
            grid=(B, NSA_GROUPS),
            in_specs=[pl.BlockSpec((None, NSA_HPG, rows, dh), lambda b, g, step=step: (b, g, step, 0)),
                      k_c, v_c, k_s, v_s, k_s, v_w,
                      pl.BlockSpec((None, rows, LANES), lambda b, g, step=step: (b, step, g)),
                      pl.BlockSpec((LANES, ncp), const)],
            out_specs=pl.BlockSpec((None, rows, NSA_HPG * dh), lambda b, g: (b, 0, g)),
            out_shape=jax.ShapeDtypeStruct((B, rows, H * dh), BF16),
            scratch_shapes=[pltpu.VMEM((nsub, LANES, tq), F32),
                            pltpu.VMEM((2, tk, nsub * NSA_HPG * tq), F32),
                            pltpu.VMEM((1, nsub * NSA_HPG * tq), F32),
                            pltpu.VMEM((vst.shape[2], nsub * NSA_HPG * tq), F32)],
            compiler_params=pltpu.CompilerParams(dimension_semantics=("parallel", "arbitrary"),
                                                 vmem_limit_bytes=VMEM_LIMIT),
            name=f"nsa{step}",
        )(qa, kc, vct, ks, vst, kw, vwt, gns, ovt))
    return parts


def _dilated_kernel(q_ref, kp_ref, kc_ref, vp_ref, vc_ref, o_ref, lse_ref, vt_ref, *, d, tq, nq):
    span = 2 * tq
    nh = LANES // HEAD_DIM
    row = lax.broadcasted_iota(jnp.int32, (span, tq), 0)
    lane = lax.broadcasted_iota(jnp.int32, (span, tq), 1)
    diff = (lane + tq) - row
    band = (diff >= 0) & (diff <= DIL_SPAN)
    has_prev = pl.program_id(1) > 0
    first = band & ((row >= tq) | has_prev)
    head_of_lane = lax.broadcasted_iota(jnp.int32, (tq, LANES), 1) // HEAD_DIM
    vt_ref[:, LANES:, :] = jnp.ones((nq * d, ONES_ROWS, span), vt_ref.dtype)

    def scores(j, r):
        rows = pl.ds(r + j * tq * d, tq, stride=d)
        qb = q_ref[rows, :]
        if j == 0:
            kb = jnp.concatenate([kp_ref[pl.ds(r, tq, stride=d), :], kc_ref[rows, :]], axis=0)
            vb = jnp.concatenate([vp_ref[pl.ds(r, tq, stride=d), :], vc_ref[rows, :]], axis=0)
        else:
            kv_rows = pl.ds(r + (j - 1) * tq * d, span, stride=d)
            kb, vb = kc_ref[kv_rows, :], vc_ref[kv_rows, :]
        vt_ref[j * d + r, 0:LANES, :] = vb.T.astype(BF16)
        q_bd = jnp.concatenate([jnp.where(head_of_lane == h, qb, 0.0) for h in range(nh)], axis=0).astype(BF16)
        return _dot_nt(kb.astype(BF16), q_bd)

    def finish(j, r, s):
        mask = first if j == 0 else band
        ms, ps = [], []
        for h in range(nh):
            sh = jnp.where(mask, s[:, h * tq:(h + 1) * tq], NEG)
            m = jnp.max(sh, axis=0, keepdims=True)
            ms.append(m)
            ps.append(jnp.exp2(sh - m).astype(BF16))
        pv = _dot(vt_ref[j * d + r], jnp.concatenate(ps, axis=1))
        o_t, l_t = [], []
        for h in range(nh):
            den = pv[LANES:LANES + 1, h * tq:(h + 1) * tq]
            o_t.append(pv[h * HEAD_DIM:(h + 1) * HEAD_DIM, h * tq:(h + 1) * tq] / den)
            l_t.append(jnp.broadcast_to(ms[h] + jnp.log2(den), (HEAD_DIM, tq)))
        rows = pl.ds(r + j * tq * d, tq, stride=d)
        o_ref[rows, :] = jnp.concatenate(o_t, axis=0).T
        lse_ref[rows, :] = jnp.concatenate(l_t, axis=0).T

    pending = None
    for j in range(nq):
        for r in range(d):
            s = scores(j, r)
            if pending is not None:
                finish(*pending)
            pending = (j, r, s)
    finish(*pending)


def _dilated(qb, kb, vb, g, block_tokens):
    B, S, _ = qb.shape
    d = DIL_PAIRS[g][1]
    tq = min(LANES, S // d)
    nq = max(block_tokens // (d * tq), 1)
    assert DIL_PAIRS[g][0] // d == DIL_SPAN and DIL_SPAN <= tq and S % (nq * d * tq) == 0
    gw = DIL_HPG * HEAD_DIM
    nb = gw // LANES
    cur = pl.BlockSpec((None, nq * d * tq, LANES), lambda b, i, c: (b, i, g * nb + c))
    prev = pl.BlockSpec((None, d * tq, LANES), lambda b, i, c: (b, jnp.maximum(i * nq - 1, 0), g * nb + c))
    out_blk = pl.BlockSpec((None, nq * d * tq, LANES), lambda b, i, c: (b, i, c))
    return pl.pallas_call(
        functools.partial(_dilated_kernel, d=d, tq=tq, nq=nq),
        grid=(B, S // (nq * d * tq), nb),
        in_specs=[cur, prev, cur, prev, cur],
        out_specs=[out_blk, out_blk],
        out_shape=[jax.ShapeDtypeStruct((B, S, gw), F32)] * 2,
        scratch_shapes=[pltpu.VMEM((nq * d, LANES + ONES_ROWS, 2 * tq), BF16)],
        compiler_params=pltpu.CompilerParams(dimension_semantics=("parallel", "arbitrary", "arbitrary"),
                                             vmem_limit_bytes=VMEM_LIMIT),
        name=f"dilated{g}",
    )(qb, kb, kb, vb, vb)


def _out_mlp_kernel(x_ref, *refs, fc, n_parts, blocks_per_part):
    ya_refs = refs[:n_parts]
    (o0_ref, o1_ref, o2_ref, l0_ref, l1_ref, l2_ref, gma_ref, gmb_ref,
     woa_ref, wob_ref, wout_ref, g2_ref, wup_ref, wdn_ref, out_ref) = refs[n_parts:]
    part = pl.program_id(1) // blocks_per_part
    ya = ya_refs[0][...]
    for p in range(1, n_parts):
        ya = jnp.where(part == p, ya_refs[p][...], ya)
    l0, l1, l2 = l0_ref[...], l1_ref[...], l2_ref[...]
    mx = jnp.maximum(jnp.maximum(l0, l1), l2)
    e0, e1, e2 = jnp.exp2(l0 - mx), jnp.exp2(l1 - mx), jnp.exp2(l2 - mx)
    yb = (e0 * o0_ref[...] + e1 * o1_ref[...] + e2 * o2_ref[...]) / (e0 + e1 + e2)
    ta = _dot(ya, woa_ref[...])
    tb = _dot(yb.astype(BF16), wob_ref[...])
    mixed = gma_ref[...].astype(F32) * ta + gmb_ref[...].astype(F32) * tb
    x1 = x_ref[...] + _dot(mixed.astype(BF16), wout_ref[...])
    ms = jnp.mean(x1 * x1, axis=-1, keepdims=True)
    h = (x1 * lax.rsqrt(ms + NORM_EPS) * g2_ref[...]).astype(BF16)
    acc = x1
    for off, w in _chunks(wup_ref.shape[1], fc):
        u = jnp.maximum(_dot(h, wup_ref[:, off:off + w]), 0.0)
        acc = acc + _dot((u * u).astype(BF16), wdn_ref[off:off + w, :])
    out_ref[...] = acc


def _out_mlp(x, ya_parts, dil, gma, gmb, w_o_a, w_o_b, w_out, norm2_g, w_up, w_down, tm, fc):
    B, S, D = x.shape
    row = lambda w: pl.BlockSpec((None, tm, w), lambda b, m: (b, m, 0))
    full = lambda a: pl.BlockSpec(a.shape, lambda b, m: (0, 0))
    gw = DIL_HPG * HEAD_DIM
    ws = [w.astype(BF16) for w in (w_o_a, w_o_b, w_out)]
    g2 = norm2_g.reshape(1, D)
    wu, wd = w_up.astype(BF16), w_down.astype(BF16)
    bpp = ya_parts[0].shape[1] // tm
    assert ya_parts[0].shape[1] % tm == 0

    def part_spec(p):
        return pl.BlockSpec((None, tm, ya_parts[p].shape[-1]),
                            lambda b, m: (b, jnp.clip(m - p * bpp, 0, bpp - 1), 0))

    return pl.pallas_call(
        functools.partial(_out_mlp_kernel, fc=fc, n_parts=len(ya_parts), blocks_per_part=bpp),
        grid=(B, S // tm),
        in_specs=([row(D)] + [part_spec(p) for p in range(len(ya_parts))] + [row(gw)] * 6 + [row(D), row(D)]
                  + [full(w) for w in ws] + [full(g2), full(wu), full(wd)]),
        out_specs=row(D),
        out_shape=jax.ShapeDtypeStruct((B, S, D), F32),
        compiler_params=pltpu.CompilerParams(dimension_semantics=("parallel", "parallel"),
                                             vmem_limit_bytes=VMEM_LIMIT),
        name="out_mlp",
    )(x, *ya_parts, dil[0][0], dil[1][0], dil[2][0], dil[0][1], dil[1][1], dil[2][1], gma, gmb, *ws, g2, wu, wd)


def kernel(x, norm1_g, w_in, q_norm_a, k_norm_cmp, k_norm_slc, k_norm_win, cmp_k_pos, cmp_k_w1, cmp_k_w2,
           cmp_v_pos, cmp_v_w1, cmp_v_w2, q_norm_b, k_norm_b, w_o_a, w_o_b, w_out, norm2_g, w_up, w_down):
    depth = w_in.shape[0]
    for i in range(depth):
        (qa, ks, kw, qb, kb, kvc_raw, vb, gma, gmb, gns, vst, vwt) = _inproj(
            x, norm1_g[i], w_in[i], q_norm_a[i], k_norm_slc[i], k_norm_win[i], q_norm_b[i], k_norm_b[i], tm=INPROJ_ROWS)
        kc, vct = _compress(kvc_raw, k_norm_cmp[i], cmp_k_pos[i], cmp_k_w1[i], cmp_k_w2[i],
                           cmp_v_pos[i], cmp_v_w1[i], cmp_v_w2[i])
        ya = _nsa(qa, kc, vct, ks, vst, kw, vwt, gns, tq=NSA_TQ, tk=NSA_TK, nsub=NSA_TILES_PER_STEP)
        dil = [_dilated(qb, kb, vb, g, block_tokens=min(DIL_BLOCK_TOKENS, x.shape[1]))
               for g in range(DIL_GROUPS)]
        x = _out_mlp(x, ya, dil, gma, gmb, w_o_a[i], w_o_b[i], w_out[i], norm2_g[i], w_up[i], w_down[i],
                     tm=OUT_MLP_ROWS, fc=MLP_FF_CHUNK)
    return x
```
